```python
import math
import jax, jax.numpy as jnp
from jax import lax
import numpy as np

D_MODEL = 1024
BATCH = 16
SEQ = 4096
DEPTH = 2

EPS = 1e-6
QBLK = 128
A_HEADS = 4
A_DIM = 64
A_VDIM = 2 * A_DIM
B_HEADS = 8
B_NOPE = 64
B_ROPE = 32
B_VDIM = 64
B_Q_LORA = 384
B_KV_LORA = 256
ROPE_THETA = 10000.0
C_HEADS = 8
C_KV_HEADS = 2
C_DIM = 64
WINDOW = 128
A_OUT = A_HEADS * A_VDIM
B_OUT = B_HEADS * B_VDIM
C_OUT = C_HEADS * C_DIM
N_BRANCH = 3
IN_SPLITS = (
    2 * A_HEADS * A_DIM,
    2 * A_HEADS * A_DIM,
    A_HEADS * A_VDIM,
    B_Q_LORA,
    B_KV_LORA,
    B_ROPE,
    C_HEADS * C_DIM,
    C_KV_HEADS * C_DIM,
    C_KV_HEADS * C_DIM,
    N_BRANCH * D_MODEL,
)
IN_COLS = 6048
N_EXPERTS = 16
EC_CAPACITY = 2
D_FF_EXPERT = 1024

kernel_name = "hybrid_gated_diff_mla_swa_ec_moe"


def rms_norm(x, g):
    xf = x.astype(jnp.float32)
    y = xf * lax.rsqrt(jnp.mean(xf * xf, axis=-1, keepdims=True) + EPS)
    return (y * g.astype(jnp.float32)).astype(x.dtype)


def alibi_slopes(n):
    return 2.0 ** (-8.0 * jnp.arange(1, n + 1, dtype=jnp.float32) / n)


def rope_tables(positions):
    inv = 1.0 / (ROPE_THETA ** (jnp.arange(0, B_ROPE, 2, dtype=jnp.float32) / B_ROPE))
    ang = positions.astype(jnp.float32)[..., None] * inv
    return jnp.cos(ang), jnp.sin(ang)


def apply_rope(x, cos, sin):
    x1, x2 = jnp.split(x.astype(jnp.float32), 2, axis=-1)
    out = jnp.concatenate([x1 * cos - x2 * sin, x1 * sin + x2 * cos], axis=-1)
    return out.astype(x.dtype)


def sweep_query_blocks(fn, *qs):
    b, s = qs[0].shape[:2]
    nb = s // QBLK
    split = lambda a: jnp.moveaxis(a.reshape((b, nb, QBLK) + a.shape[2:]), 1, 0)
    out = lax.map(lambda args: fn(*args), tuple(split(a) for a in qs))
    out = jnp.moveaxis(out, 0, 1)
    return out.reshape((b, s) + out.shape[3:])


def band_blocks(a):
    b, s = a.shape[:2]
    nb = s // WINDOW
    pad = [(0, 0), (WINDOW, WINDOW)] + [(0, 0)] * (a.ndim - 2)
    ap = jnp.pad(a, pad).reshape((b, nb + 2, WINDOW) + a.shape[2:])
    return jnp.concatenate([ap[:, :-2], ap[:, 1:-1], ap[:, 2:]], axis=2)


def diff_attention(qa, ka, va, positions, qk_g, lam_params, out_g, layer_idx):
    b, s = qa.shape[:2]
    qa = rms_norm(qa.reshape(b, s, 2, A_HEADS, A_DIM), qk_g[0])
    ka = rms_norm(ka.reshape(b, s, 2, A_HEADS, A_DIM), qk_g[1])
    va = va.reshape(b, s, A_HEADS, A_VDIM)
    lam_init = 0.8 - 0.6 * math.exp(-0.3 * layer_idx)
    lp = lam_params.astype(jnp.float32)
    lam = jnp.exp(jnp.sum(lp[0] * lp[1])) - jnp.exp(jnp.sum(lp[2] * lp[3])) + lam_init
    slopes = alibi_slopes(A_HEADS)
    scale = A_DIM ** -0.5

    def block(qb, pb):
        sc = jnp.einsum('bqmhd,bkmhd->bmhqk', qb, ka).astype(jnp.float32) * scale
        dist = jnp.abs(pb[:, :, None] - positions[:, None, :]).astype(jnp.float32)
        sc = sc - (slopes[None, :, None, None] * dist[:, None])[:, None]
        p = jax.nn.softmax(sc, axis=-1)
        attn = p[:, 0] - lam * p[:, 1]
        return jnp.einsum('bhqk,bkhe->bqhe', attn.astype(va.dtype), va)

    oa = sweep_query_blocks(block, qa, positions)
    oa = rms_norm(oa, out_g) * (1.0 - lam_init)
    return oa.reshape(b, s, A_OUT)


def latent_attention(cq, ckv, kr, positions, cq_g, w_uq, ckv_g, w_ukv, qk_g):
    b, s = cq.shape[:2]
    q = (rms_norm(cq, cq_g) @ w_uq).reshape(b, s, B_HEADS, B_NOPE + B_ROPE)
    kv = (rms_norm(ckv, ckv_g) @ w_ukv).reshape(b, s, B_HEADS, B_NOPE + B_VDIM)
    cos, sin = rope_tables(positions)
    q_nope, q_rope = q[..., :B_NOPE], q[..., B_NOPE:]
    q_rope = apply_rope(q_rope, cos[:, :, None], sin[:, :, None])
    k_nope, v = kv[..., :B_NOPE], kv[..., B_NOPE:]
    k_rope = apply_rope(kr, cos, sin)[:, :, None]
    q = rms_norm(jnp.concatenate([q_nope, q_rope], axis=-1), qk_g[0])
    k = jnp.concatenate([k_nope, jnp.broadcast_to(k_rope, (b, s, B_HEADS, B_ROPE))], axis=-1)
    k = rms_norm(k, qk_g[1])
    scale = (B_NOPE + B_ROPE) ** -0.5

    def block(qb):
        sc = jnp.einsum('bqhd,bkhd->bhqk', qb, k).astype(jnp.float32) * scale
        p = jax.nn.softmax(sc, axis=-1)
        return jnp.einsum('bhqk,bkhe->bqhe', p.astype(v.dtype), v)

    ob = sweep_query_blocks(block, q)
    return ob.reshape(b, s, B_OUT)


def window_gqa(qc, kc, vc, qk_g, sink):
    b, s = qc.shape[:2]
    nb = s // WINDOW
    rep = C_HEADS // C_KV_HEADS
    q = rms_norm(qc.reshape(b, s, C_KV_HEADS, rep, C_DIM), qk_g[0])
    k = rms_norm(kc.reshape(b, s, C_KV_HEADS, C_DIM), qk_g[1])
    v = vc.reshape(b, s, C_KV_HEADS, C_DIM)
    qb = q.reshape(b, nb, WINDOW, C_KV_HEADS, rep, C_DIM)
    kb = band_blocks(k)
    vb = band_blocks(v)
    r = jnp.arange(WINDOW)[None, :, None]
    c = jnp.arange(3 * WINDOW)[None, None, :]
    blk = jnp.arange(nb)[:, None, None]
    rel = c - WINDOW - r
    kidx = (blk - 1) * WINDOW + c
    mask = (jnp.abs(rel) <= WINDOW) & (kidx >= 0) & (kidx < s)
    slopes = alibi_slopes(C_HEADS).reshape(C_KV_HEADS, rep)
    bias = -slopes[:, :, None, None] * jnp.abs(rel[0]).astype(jnp.float32)
    sc = jnp.einsum('bnqgrd,bnkgd->bgrnqk', qb, kb).astype(jnp.float32) * (C_DIM ** -0.5)
    sc = jnp.where(mask[None, None, None], sc + bias[None, :, :, None], -1e30)
    sk = sink.astype(jnp.float32).reshape(C_KV_HEADS, rep)[None, :, :, None, None, None]
    m = jnp.maximum(jnp.max(sc, axis=-1, keepdims=True), sk)
    e = jnp.exp(sc - m)
    p = e / (jnp.sum(e, axis=-1, keepdims=True) + jnp.exp(sk - m))
    oc = jnp.einsum('bgrnqk,bnkgd->bnqgrd', p.astype(vb.dtype), vb)
    return oc.reshape(b, s, C_OUT)


def hybrid_mixer(h, positions, layer_idx, w_in, diff_qk_g, diff_lambda, diff_out_g,
                 mla_cq_g, w_uq, mla_ckv_g, w_ukv, mla_qk_g, swa_qk_g, swa_sink,
                 w_branch_a, w_branch_b, w_branch_c, w_o):
    b, s, d = h.shape
    offsets = []
    acc = 0
    for width in IN_SPLITS[:-1]:
        acc += width
        offsets.append(acc)
    proj = h @ w_in
    qa, ka, va, cq, ckv, kr, qc, kc, vc, gates = jnp.split(proj, offsets, axis=-1)
    oa = diff_attention(qa, ka, va, positions, diff_qk_g, diff_lambda, diff_out_g, layer_idx)
    ob = latent_attention(cq, ckv, kr, positions, mla_cq_g, w_uq, mla_ckv_g, w_ukv, mla_qk_g)
    oc = window_gqa(qc, kc, vc, swa_qk_g, swa_sink)
    g = jax.nn.sigmoid(gates.astype(jnp.float32)).reshape(b, s, N_BRANCH, d).astype(h.dtype)
    merged = g[:, :, 0] * (oa @ w_branch_a) + g[:, :, 1] * (ob @ w_branch_b) + g[:, :, 2] * (oc @ w_branch_c)
    return merged @ w_o


def expert_choice_ffn(h, w_router, w_gate, w_up, w_down):
    b, s, d = h.shape
    cap = max(1, EC_CAPACITY * s // N_EXPERTS)
    aff = jax.nn.softmax((h @ w_router).astype(jnp.float32), axis=-1)
    top_aff, top_idx = lax.top_k(jnp.swapaxes(aff, 1, 2), cap)
    xin = jax.vmap(lambda hb, ib: hb[ib])(h, top_idx)
    hid = jax.nn.silu(jnp.einsum('becd,edf->becf', xin, w_gate)) * jnp.einsum('becd,edf->becf', xin, w_up)
    y = jnp.einsum('becf,efd->becd', hid, w_down) * top_aff[..., None].astype(h.dtype)
    flat = (top_idx + (jnp.arange(b, dtype=jnp.int32) * s)[:, None, None]).reshape(-1)
    out = jax.ops.segment_sum(y.reshape(-1, d), flat, num_segments=b * s)
    return out.reshape(b, s, d)


def setup_inputs(seed: int = 0) -> dict:
    key = jax.random.key(seed)
    ks = jax.random.split(key, 24)
    f32 = jnp.float32
    nrm = lambda k, shape, fan_in: jax.random.normal(k, shape, f32) * (fan_in ** -0.5)
    gain = lambda k, shape: 1.0 + 0.1 * jax.random.normal(k, shape, f32)
    x = jax.random.normal(ks[0], (BATCH, SEQ, D_MODEL), f32)
    offs = jax.random.randint(ks[1], (BATCH, 1), 0, 1024, dtype=jnp.int32)
    positions = offs + jnp.arange(SEQ, dtype=jnp.int32)[None, :]
    return {
        "x": x,
        "positions": positions,
        "norm_mix_g": gain(ks[2], (DEPTH, D_MODEL)),
        "w_in": nrm(ks[3], (DEPTH, D_MODEL, IN_COLS), D_MODEL),
        "diff_qk_g": gain(ks[4], (DEPTH, 2, A_DIM)),
        "diff_lambda": 0.1 * jax.random.normal(ks[5], (DEPTH, 4, A_DIM), f32),
        "diff_out_g": gain(ks[6], (DEPTH, A_VDIM)),
        "mla_cq_g": gain(ks[7], (DEPTH, B_Q_LORA)),
        "w_uq": nrm(ks[8], (DEPTH, B_Q_LORA, B_HEADS * (B_NOPE + B_ROPE)), B_Q_LORA),
        "mla_ckv_g": gain(ks[9], (DEPTH, B_KV_LORA)),
        "w_ukv": nrm(ks[10], (DEPTH, B_KV_LORA, B_HEADS * (B_NOPE + B_VDIM)), B_KV_LORA),
        "mla_qk_g": gain(ks[11], (DEPTH, 2, B_NOPE + B_ROPE)),
        "swa_qk_g": gain(ks[12], (DEPTH, 2, C_DIM)),
        "swa_sink": 0.5 * jax.random.normal(ks[13], (DEPTH, C_HEADS), f32),
        "w_branch_a": nrm(ks[14], (DEPTH, A_OUT, D_MODEL), A_OUT),
        "w_branch_b": nrm(ks[15], (DEPTH, B_OUT, D_MODEL), B_OUT),
        "w_branch_c": nrm(ks[16], (DEPTH, C_OUT, D_MODEL), C_OUT),
        "w_o": nrm(ks[17], (DEPTH, D_MODEL, D_MODEL), D_MODEL),
        "norm_ffn_g": gain(ks[18], (DEPTH, D_MODEL)),
        "w_router": nrm(ks[19], (DEPTH, D_MODEL, N_EXPERTS), D_MODEL),
        "w_exp_gate": nrm(ks[20], (DEPTH, N_EXPERTS, D_MODEL, D_FF_EXPERT), D_MODEL),
        "w_exp_up": nrm(ks[21], (DEPTH, N_EXPERTS, D_MODEL, D_FF_EXPERT), D_MODEL),
        "w_exp_down": nrm(ks[22], (DEPTH, N_EXPERTS, D_FF_EXPERT, D_MODEL), D_FF_EXPERT),
    }


def reference(x, positions, norm_mix_g, w_in, diff_qk_g, diff_lambda, diff_out_g,
              mla_cq_g, w_uq, mla_ckv_g, w_ukv, mla_qk_g, swa_qk_g, swa_sink,
              w_branch_a, w_branch_b, w_branch_c, w_o, norm_ffn_g, w_router,
              w_exp_gate, w_exp_up, w_exp_down):
    for l in range(DEPTH):
        h = rms_norm(x, norm_mix_g[l])
        x = x + hybrid_mixer(h, positions, l, w_in[l], diff_qk_g[l], diff_lambda[l], diff_out_g[l],
                             mla_cq_g[l], w_uq[l], mla_ckv_g[l], w_ukv[l], mla_qk_g[l],
                             swa_qk_g[l], swa_sink[l], w_branch_a[l], w_branch_b[l],
                             w_branch_c[l], w_o[l])
        h = rms_norm(x, norm_ffn_g[l])
        x = x + expert_choice_ffn(h, w_router[l], w_exp_gate[l], w_exp_up[l], w_exp_down[l])
    return x
```

```python
import functools
import math

import numpy as np
import jax
import jax.numpy as jnp
from jax import lax
from jax.experimental import pallas as pl
from jax.experimental.pallas import tpu as pltpu

F32 = jnp.float32
BF16 = jnp.bfloat16

D_MODEL = 1024
EPS = 1e-6
A_HEADS = 4
A_DIM = 64
A_VDIM = 128
B_HEADS = 8
B_NOPE = 64
B_ROPE = 32
B_VDIM = 64
B_QK = B_NOPE + B_ROPE
B_Q_LORA = 384
B_KV_LORA = 256
ROPE_THETA = 10000.0
C_HEADS = 8
C_KV_HEADS = 2
C_REP = C_HEADS // C_KV_HEADS
C_DIM = 64
WINDOW = 128
N_EXPERTS = 16
EC_CAPACITY = 2
D_FF = 1024
LANES = 128

_QA = 0
_KA = 512
_VA = 1024
_CQ = 1536
_CKV = 1920
_QC = 2176
_KC = 2688
_VC = 2944
_KR = 3200
_KRR = 3328
_PROJ_COLS = 3456

VMEM_LIMIT = 56 * 1024 * 1024

_NT = (((1,), (1,)), ((), ()))
_TN = (((0,), (0,)), ((), ()))


def _cparams(sem):
    return pltpu.CompilerParams(dimension_semantics=sem, vmem_limit_bytes=VMEM_LIMIT)


def _dot(a, b):
    return jnp.dot(a, b, preferred_element_type=F32)


def _seg_sum(x2, e):
    hi = x2.astype(BF16)
    lo = (x2 - hi.astype(F32)).astype(BF16)
    return _dot(hi, e) + _dot(lo, e)


def _row_rms(x, g):
    return x * lax.rsqrt(jnp.mean(x * x, axis=-1, keepdims=True) + EPS) * g


def _proj_kernel(x_ref, gmix_ref, w_ref, wuq_ref, wukv_ref, e64_ref, e128_ref,
                 gqa_ref, gka_ref, gcq_ref, gckv_ref, gqb_ref, gkb_ref, gqc_ref, gkc_ref,
                 cos_ref, sin_ref,
                 qa_o, ka_o, va_o, qb_o, kb_o, vb_o, qc_o, kc_o, vc_o):
    hb = _row_rms(x_ref[0], gmix_ref[...]).astype(BF16)

    def proj(a, n):
        return _dot(hb, w_ref[:, a:a + n])

    e64 = e64_ref[...]
    e128 = e128_ref[...]

    def seg_norm(v, e, width, g):
        return v * lax.rsqrt(_seg_sum(v * v, e) * (1.0 / width) + EPS) * g

    def store_slots(o_ref, v, n):
        for j in range(n):
            o_ref[0, j] = v[:, LANES * j:LANES * (j + 1)].astype(o_ref.dtype)

    store_slots(qa_o, seg_norm(proj(_QA, 512), e64, A_DIM, gqa_ref[...]), A_HEADS)
    store_slots(ka_o, seg_norm(proj(_KA, 512), e64, A_DIM, gka_ref[...]), A_HEADS)
    store_slots(va_o, proj(_VA, 512), A_HEADS)

    cos_t = cos_ref[0]
    sin_t = sin_ref[0]
    cos8 = jnp.concatenate([cos_t] * B_HEADS, axis=1)
    sin8 = jnp.concatenate([sin_t] * B_HEADS, axis=1)
    cq = _row_rms(proj(_CQ, B_Q_LORA), gcq_ref[...]).astype(BF16)
    q2 = _dot(cq, wuq_ref[...])
    qb = q2[:, :1024] * cos8 + q2[:, 1024:] * sin8
    store_slots(qb_o, seg_norm(qb, e128, B_QK, gqb_ref[...]), B_HEADS)
    ckv = _row_rms(proj(_CKV, B_KV_LORA), gckv_ref[...]).astype(BF16)
    kv = _dot(ckv, wukv_ref[...])
    kr = proj(_KR, LANES) * cos_t + proj(_KRR, LANES) * sin_t
    kb = kv[:, :1024] + jnp.concatenate([kr] * B_HEADS, axis=1)
    store_slots(kb_o, seg_norm(kb, e128, B_QK, gkb_ref[...]), B_HEADS)
    store_slots(vb_o, kv[:, 1024:], B_HEADS // 2)

    store_slots(qc_o, seg_norm(proj(_QC, 512), e64, C_DIM, gqc_ref[...]), C_HEADS // 2)
    store_slots(kc_o, seg_norm(proj(_KC, 256), e64[:256, :256], C_DIM, gkc_ref[...]), C_KV_HEADS)
    store_slots(vc_o, proj(_VC, 256), C_KV_HEADS)


def _proj_call(x, cos_t, sin_t, p, tm):
    b, s, d = x.shape
    full = lambda a: pl.BlockSpec(a.shape, lambda i, j: (0,) * a.ndim)
    slot = lambda n: pl.BlockSpec((1, n, tm, LANES), lambda i, j: (i, 0, j, 0))
    tok = lambda w: pl.BlockSpec((1, tm, w), lambda i, j: (i, j, 0))
    consts = [p["gmix"], p["w_in"], p["w_uq"], p["w_ukv"], p["e64"], p["e128"],
              p["gqa"], p["gka"], p["gcq"], p["gckv"], p["gqb"], p["gkb"], p["gqc"], p["gkc"]]
    slots = [A_HEADS, A_HEADS, A_HEADS, B_HEADS, B_HEADS, B_HEADS // 2, C_HEADS // 2, C_KV_HEADS, C_KV_HEADS]
    return pl.pallas_call(
        _proj_kernel,
        grid=(b, s // tm),
        in_specs=[tok(d)] + [full(a) for a in consts] + [tok(LANES), tok(LANES)],
        out_specs=[slot(n) for n in slots],
        out_shape=[jax.ShapeDtypeStruct((b, n, s, LANES), BF16) for n in slots],
        compiler_params=_cparams(("parallel", "parallel")),
        name="proj",
    )(x, *consts, cos_t, sin_t)


def _attn_a_kernel(q_ref, k_ref, v_ref, pc_ref, pr_ref, slope_ref, lam_ref, og_ref, o_ref, *, lam_init):
    q = q_ref[0, 0]
    tq = q.shape[0]
    lane = lax.broadcasted_iota(jnp.int32, q.shape, 1)
    zero = jnp.zeros_like(q)
    qz = jnp.concatenate([jnp.where(lane < A_DIM, q, zero), jnp.where(lane >= A_DIM, q, zero)], axis=0)
    s = lax.dot_general(qz, k_ref[0, 0], _NT, preferred_element_type=F32)
    bias = slope_ref[0][:, :1] * jnp.abs(pc_ref[0] - pr_ref[0])

    def softmax_parts(sc):
        m = jnp.max(sc, axis=-1, keepdims=True)
        e = jnp.exp(sc - m)
        return e, 1.0 / jnp.sum(e, axis=-1, keepdims=True)

    e0, r0 = softmax_parts(s[:tq] - bias)
    e1, r1 = softmax_parts(s[tq:] - bias)
    lp = lam_ref[...]
    lam = (jnp.exp(jnp.sum(lp[0:1] * lp[1:2], axis=-1, keepdims=True))
           - jnp.exp(jnp.sum(lp[2:3] * lp[3:4], axis=-1, keepdims=True)) + lam_init)
    attn = (e0 * r0 - e1 * (lam * r1)).astype(BF16)
    o = _dot(attn, v_ref[0, 0])
    o_ref[0] = (_row_rms(o, og_ref[...]) * (1.0 - lam_init)).astype(o_ref.dtype)


def _attn_a_call(qa, ka, va, pos_c, pos_r, slopes, lam_p, out_g, lam_init, tq):
    b, h, s, _ = qa.shape
    kv_spec = pl.BlockSpec((1, 1, s, LANES), lambda i, j, t: (i, j, 0, 0))
    return pl.pallas_call(
        functools.partial(_attn_a_kernel, lam_init=lam_init),
        grid=(b, h, s // tq),
        in_specs=[pl.BlockSpec((1, 1, tq, LANES), lambda i, j, t: (i, j, t, 0)), kv_spec, kv_spec,
                  pl.BlockSpec((1, tq, 1), lambda i, j, t: (i, t, 0)),
                  pl.BlockSpec((1, 1, s), lambda i, j, t: (i, 0, 0)),
                  pl.BlockSpec((1, 1, LANES), lambda i, j, t: (j, 0, 0)),
                  pl.BlockSpec(lam_p.shape, lambda i, j, t: (0, 0)),
                  pl.BlockSpec(out_g.shape, lambda i, j, t: (0, 0))],
        out_specs=pl.BlockSpec((1, tq, LANES), lambda i, j, t: (i, t, j)),
        out_shape=jax.ShapeDtypeStruct((b, s, h * LANES), BF16),
        compiler_params=_cparams(("parallel", "parallel", "arbitrary")),
        name="attn_a",
    )(qa, ka, va, pos_c, pos_r, slopes, lam_p, out_g)


def _attn_b_kernel(q_ref, k_ref, v_ref, o_ref):
    v = v_ref[0, 0]
    outs = []
    for j in range(2):
        s = lax.dot_general(q_ref[0, j], k_ref[0, j], _NT, preferred_element_type=F32)
        m = jnp.max(s, axis=-1, keepdims=True)
        e = jnp.exp(s - m)
        r = 1.0 / jnp.sum(e, axis=-1, keepdims=True)
        outs.append(_dot(e.astype(BF16), v) * r)
    lane = lax.broadcasted_iota(jnp.int32, outs[0].shape, 1)
    o_ref[0] = jnp.where(lane < B_VDIM, outs[0], outs[1]).astype(o_ref.dtype)


def _attn_b_call(qb, kb, vb, tq):
    b, h, s, _ = qb.shape
    return pl.pallas_call(
        _attn_b_kernel,
        grid=(b, h // 2, s // tq),
        in_specs=[pl.BlockSpec((1, 2, tq, LANES), lambda i, j, t: (i, j, t, 0)),
                  pl.BlockSpec((1, 2, s, LANES), lambda i, j, t: (i, j, 0, 0)),
                  pl.BlockSpec((1, 1, s, LANES), lambda i, j, t: (i, j, 0, 0))],
        out_specs=pl.BlockSpec((1, tq, LANES), lambda i, j, t: (i, t, j)),
        out_shape=jax.ShapeDtypeStruct((b, s, (h // 2) * LANES), BF16),
        compiler_params=_cparams(("parallel", "parallel", "arbitrary")),
        name="attn_b",
    )(qb, kb, vb)


def _attn_c_kernel(q_ref, kp_ref, ko_ref, kn_ref, vp_ref, vo_ref, vn_ref, slope_ref, sink_ref, o_ref, *, seq):
    n = pl.program_id(2)
    w = WINDOW
    kcat = jnp.concatenate([kp_ref[0, 0], ko_ref[0, 0], kn_ref[0, 0]], axis=0)
    vcat = jnp.concatenate([vp_ref[0, 0], vo_ref[0, 0], vn_ref[0, 0]], axis=0)
    lane = lax.broadcasted_iota(jnp.int32, (w, LANES), 1)
    parts = []
    for p in range(2):
        q = q_ref[0, p]
        zero = jnp.zeros_like(q)
        parts += [jnp.where(lane < C_DIM, q, zero), jnp.where(lane >= C_DIM, q, zero)]
    qz = jnp.concatenate(parts, axis=0)
    s = lax.dot_general(qz, kcat, _NT, preferred_element_type=F32)
    r_idx = lax.broadcasted_iota(jnp.int32, (w, 3 * w), 0)
    c_idx = lax.broadcasted_iota(jnp.int32, (w, 3 * w), 1)
    rel = c_idx - w - r_idx
    kidx = (n - 1) * w + c_idx
    arel = jnp.abs(rel)
    valid = jnp.where(arel <= w, jnp.where(kidx >= 0, jnp.where(kidx < seq, 1, 0), 0), 0) > 0
    dist = arel.astype(F32)
    slopes = slope_ref[0]
    sinks = sink_ref[0]
    outs = []
    for r in range(C_REP):
        sc = jnp.where(valid, s[r * w:(r + 1) * w] - slopes[r:r + 1, :1] * dist, -1e30)
        sk = sinks[r:r + 1, :1]
        m = jnp.maximum(jnp.max(sc, axis=-1, keepdims=True), sk)
        e = jnp.exp(sc - m)
        den = jnp.sum(e, axis=-1, keepdims=True) + jnp.exp(sk - m)
        outs.append(_dot(e.astype(BF16), vcat) * (1.0 / den))
    pair0 = jnp.where(lane < C_DIM, outs[0], outs[1])
    pair1 = jnp.where(lane < C_DIM, outs[2], outs[3])
    o_ref[0] = jnp.concatenate([pair0, pair1], axis=1).astype(o_ref.dtype)


def _attn_c_call(qc, kc, vc, slopes, sinks):
    b, _, s, _ = qc.shape
    nb = s // WINDOW
    blk = lambda f: pl.BlockSpec((1, 1, WINDOW, LANES), f)
    prev = blk(lambda i, g, n: (i, g, jnp.maximum(n - 1, 0), 0))
    own = blk(lambda i, g, n: (i, g, n, 0))
    nxt = blk(lambda i, g, n: (i, g, jnp.minimum(n + 1, nb - 1), 0))
    per_group = pl.BlockSpec((1, C_REP, LANES), lambda i, g, n: (g, 0, 0))
    return pl.pallas_call(
        functools.partial(_attn_c_kernel, seq=s),
        grid=(b, C_KV_HEADS, nb),
        in_specs=[pl.BlockSpec((1, 2, WINDOW, LANES), lambda i, g, n: (i, g, n, 0)),
                  prev, own, nxt, prev, own, nxt, per_group, per_group],
        out_specs=pl.BlockSpec((1, WINDOW, 2 * LANES), lambda i, g, n: (i, n, g)),
        out_shape=jax.ShapeDtypeStruct((b, s, C_HEADS * C_DIM), BF16),
        compiler_params=_cparams(("parallel", "parallel", "arbitrary")),
        name="attn_c",
    )(qc, kc, kc, kc, vc, vc, vc, slopes, sinks)


def _merge_kernel(x_ref, gmix_ref, wg_ref, oa_ref, ob_ref, oc_ref, wa_ref, wb_ref, wc_ref, wo_ref,
                  gffn_ref, wr_ref, xn_o, h2_o, aff_o):
    d = D_MODEL
    x = x_ref[0]
    hb = _row_rms(x, gmix_ref[...]).astype(BF16)
    g = 1.0 / (1.0 + jnp.exp(-_dot(hb, wg_ref[...])))
    merged = (g[:, :d] * _dot(oa_ref[0], wa_ref[...]) + g[:, d:2 * d] * _dot(ob_ref[0], wb_ref[...])
              + g[:, 2 * d:] * _dot(oc_ref[0], wc_ref[...]))
    xn = x + _dot(merged.astype(BF16), wo_ref[...])
    xn_o[0] = xn
    h2 = _row_rms(xn, gffn_ref[...])
    h2_o[0] = h2.astype(BF16)
    logits = lax.dot_general(wr_ref[...], h2, _NT, preferred_element_type=F32,
                             precision=lax.Precision.HIGHEST)
    ex = jnp.exp(logits - jnp.max(logits, axis=0, keepdims=True))
    aff_o[0] = ex / jnp.sum(ex, axis=0, keepdims=True)


def _merge_call(x, oa, ob, oc, p, tm):
    b, s, d = x.shape
    full = lambda a: pl.BlockSpec(a.shape, lambda i, j: (0,) * a.ndim)
    tok = lambda w: pl.BlockSpec((1, tm, w), lambda i, j: (i, j, 0))
    return pl.pallas_call(
        _merge_kernel,
        grid=(b, s // tm),
        in_specs=[tok(d), full(p["gmix"]), full(p["w_gate"]), tok(512), tok(512), tok(512),
                  full(p["w_a"]), full(p["w_b"]), full(p["w_c"]), full(p["w_o"]), full(p["gffn"]), full(p["w_rt"])],
        out_specs=[tok(d), tok(d), pl.BlockSpec((1, N_EXPERTS, tm), lambda i, j: (i, 0, j))],
        out_shape=[jax.ShapeDtypeStruct((b, s, d), F32), jax.ShapeDtypeStruct((b, s, d), BF16),
                   jax.ShapeDtypeStruct((b, N_EXPERTS, s), F32)],
        compiler_params=_cparams(("parallel", "parallel")),
        name="merge",
    )(x, p["gmix"], p["w_gate"], oa, ob, oc, p["w_a"], p["w_b"], p["w_c"], p["w_o"], p["gffn"], p["w_rt"])


def _cumsum_lanes(mask01, chunk):
    rows, s = mask01.shape
    tri = jnp.where(lax.broadcasted_iota(jnp.int32, (chunk, chunk), 0)
                    <= lax.broadcasted_iota(jnp.int32, (chunk, chunk), 1), 1.0, 0.0).astype(BF16)
    carry = jnp.zeros((rows, 1), F32)
    outs = []
    for c in range(s // chunk):
        cs = _dot(mask01[:, c * chunk:(c + 1) * chunk], tri) + carry
        outs.append(cs)
        carry = cs[:, chunk - 1:chunk]
    return jnp.concatenate(outs, axis=1)


def _topk_kernel(aff_ref, pos_o, *, cap, chunk):
    bits = pltpu.bitcast(aff_ref[0], jnp.int32)
    rows = bits.shape[0]
    capf = float(cap)

    def count(mask):
        return jnp.sum(jnp.where(mask, 1.0, 0.0), axis=1, keepdims=True)

    def body(_, c):
        lo, hi = c
        mid = lo + ((hi - lo + 1) >> 1)
        ok = count(bits >= mid) >= capf
        return jnp.where(ok, mid, lo), jnp.where(ok, hi, mid - 1)

    lo0 = jnp.zeros((rows, 1), jnp.int32)
    hi0 = jnp.full((rows, 1), 0x7F800000, jnp.int32)
    thr, _ = lax.fori_loop(0, 31, body, (lo0, hi0))
    gt = bits > thr
    eq = bits == thr
    need = capf - count(gt)
    eq_rank = _cumsum_lanes(jnp.where(eq, 1.0, 0.0).astype(BF16), chunk)
    sel = jnp.where(gt, 1.0, jnp.where(eq, jnp.where(eq_rank <= need, 1.0, 0.0), 0.0))
    slot = _cumsum_lanes(sel.astype(BF16), chunk) - 1.0
    pos_o[0] = jnp.where(sel > 0.0, slot, -1.0).astype(jnp.int32)


def _topk_call(aff_t, cap):
    b, e, s = aff_t.shape
    spec = pl.BlockSpec((1, e, s), lambda i: (i, 0, 0))
    return pl.pallas_call(
        functools.partial(_topk_kernel, cap=cap, chunk=min(512, s)),
        grid=(b,),
        in_specs=[spec],
        out_specs=spec,
        out_shape=jax.ShapeDtypeStruct((b, e, s), jnp.int32),
        compiler_params=_cparams(("parallel",)),
        name="topk",
    )(aff_t)


def _gather_kernel(pos_ref, aff_ref, h_ref, xin_o, asel_o, *, cap):
    pos = pos_ref[0, 0]
    hit = lax.broadcasted_iota(jnp.int32, (cap, pos.shape[1]), 0) == pos
    onehot = jnp.where(hit, 1.0, 0.0).astype(BF16)
    xin_o[0, 0] = _dot(onehot, h_ref[0]).astype(BF16)
    asel = jnp.sum(jnp.where(hit, aff_ref[0, 0], 0.0), axis=1, keepdims=True)
    asel_o[0, 0] = jnp.broadcast_to(asel, (cap, LANES))


def _gather_call(pos4, aff4, h2, cap):
    b, e, _, s = pos4.shape
    d = h2.shape[-1]
    row = pl.BlockSpec((1, 1, 1, s), lambda i, j: (i, j, 0, 0))
    return pl.pallas_call(
        functools.partial(_gather_kernel, cap=cap),
        grid=(b, e),
        in_specs=[row, row, pl.BlockSpec((1, s, d), lambda i, j: (i, 0, 0))],
        out_specs=[pl.BlockSpec((1, 1, cap, d), lambda i, j: (i, j, 0, 0)),
                   pl.BlockSpec((1, 1, cap, LANES), lambda i, j: (i, j, 0, 0))],
        out_shape=[jax.ShapeDtypeStruct((b, e, cap, d), BF16), jax.ShapeDtypeStruct((b, e, cap, LANES), F32)],
        compiler_params=_cparams(("parallel", "arbitrary")),
        name="gather",
    )(pos4, aff4, h2)


def _expert_kernel(xin_ref, asel_ref, wg_ref, wu_ref, wd_ref, y_o):
    xin = xin_ref[0, 0]
    g = _dot(xin, wg_ref[0])
    u = _dot(xin, wu_ref[0])
    hid = (g * (1.0 / (1.0 + jnp.exp(-g))) * u).astype(BF16)
    y_o[0, 0] = (_dot(hid, wd_ref[0]) * asel_ref[0, 0][:, :1]).astype(y_o.dtype)


def _expert_call(xin, asel, wg, wu, wd):
    b, e, cap, d = xin.shape
    f = wg.shape[-1]
    tokens = lambda w: pl.BlockSpec((1, 1, cap, w), lambda j, i: (i, j, 0, 0))
    return pl.pallas_call(
        _expert_kernel,
        grid=(e, b),
        in_specs=[tokens(d), tokens(LANES),
                  pl.BlockSpec((1, d, f), lambda j, i: (j, 0, 0)),
                  pl.BlockSpec((1, d, f), lambda j, i: (j, 0, 0)),
                  pl.BlockSpec((1, f, d), lambda j, i: (j, 0, 0))],
        out_specs=tokens(d),
        out_shape=jax.ShapeDtypeStruct((b, e, cap, d), BF16),
        compiler_params=_cparams(("parallel", "arbitrary")),
        name="expert",
    )(xin, asel, wg, wu, wd)


def _combine_kernel(x_ref, pos_ref, y_ref, o_ref, *, cap):
    @pl.when(pl.program_id(2) == 0)
    def _():
        o_ref[...] = x_ref[...]

    pos = pos_ref[0, 0]
    hit = lax.broadcasted_iota(jnp.int32, (cap, pos.shape[1]), 0) == pos
    onehot = jnp.where(hit, 1.0, 0.0).astype(BF16)
    o_ref[0] += lax.dot_general(onehot, y_ref[0, 0], _TN, preferred_element_type=F32)


def _combine_call(xn, pos4, y, tt):
    b, s, d = xn.shape
    e, cap = y.shape[1], y.shape[2]
    tok = pl.BlockSpec((1, tt, d), lambda i, t, j: (i, t, 0))
    return pl.pallas_call(
        functools.partial(_combine_kernel, cap=cap),
        grid=(b, s // tt, e),
        in_specs=[tok, pl.BlockSpec((1, 1, 1, tt), lambda i, t, j: (i, j, 0, t)),
                  pl.BlockSpec((1, 1, cap, d), lambda i, t, j: (i, j, 0, 0))],
        out_specs=tok,
        out_shape=jax.ShapeDtypeStruct((b, s, d), F32),
        compiler_params=_cparams(("parallel", "parallel", "arbitrary")),
        name="combine",
    )(xn, pos4, y)


def _block_diag(n, blk):
    i = np.arange(n)
    return jnp.asarray((i[:, None] // blk) == (i[None, :] // blk), dtype=BF16)


def _head_slots(w, heads, width):
    r = w.shape[0]
    return jnp.pad(w.reshape(r, heads, width), ((0, 0), (0, 0), (0, LANES - width))).reshape(r, heads * LANES)


def _rotate_half_cols(w):
    half = B_ROPE // 2
    return jnp.concatenate([-w[..., half:], w[..., :half]], axis=-1)


def _layer_params(l, w_in, norm_mix_g, diff_qk_g, mla_cq_g, w_uq, mla_ckv_g, w_ukv, mla_qk_g, swa_qk_g,
                  w_branch_a, w_branch_b, w_branch_c, w_o, norm_ffn_g, w_router):
    d = D_MODEL
    wi = w_in[l]
    off = np.cumsum([0, 512, 512, 512, B_Q_LORA, B_KV_LORA, B_ROPE, 512, 128, 128, 3 * d])
    piece = lambda k: wi[:, off[k]:off[k + 1]]
    maps_to_heads = lambda w: w.reshape(d, 2, A_HEADS, A_DIM).transpose(0, 2, 1, 3).reshape(d, 512)
    dup = lambda w: jnp.concatenate([w.reshape(d, C_KV_HEADS, 1, C_DIM)] * 2, axis=2).reshape(d, 256)
    kr = piece(5)
    rope_slot = lambda w: jnp.pad(w, ((0, 0), (B_NOPE, LANES - B_QK)))
    w_proj = jnp.concatenate(
        [maps_to_heads(piece(0)), maps_to_heads(piece(1)), piece(2), piece(3), piece(4), piece(6),
         dup(piece(7)), dup(piece(8)), rope_slot(kr), rope_slot(_rotate_half_cols(kr))], axis=1).astype(BF16)
    assert w_proj.shape == (d, _PROJ_COLS)

    wq = w_uq[l].reshape(B_Q_LORA, B_HEADS, B_QK)
    wq_rot = jnp.concatenate([jnp.zeros_like(wq[..., :B_NOPE]), _rotate_half_cols(wq[..., B_NOPE:])], axis=-1)
    w_uq_x = jnp.concatenate([_head_slots(wq.reshape(B_Q_LORA, -1), B_HEADS, B_QK),
                              _head_slots(wq_rot.reshape(B_Q_LORA, -1), B_HEADS, B_QK)], axis=1).astype(BF16)
    wkv = w_ukv[l].reshape(B_KV_LORA, B_HEADS, B_NOPE + B_VDIM)
    w_ukv_x = jnp.concatenate([_head_slots(wkv[..., :B_NOPE].reshape(B_KV_LORA, -1), B_HEADS, B_NOPE),
                               wkv[..., B_NOPE:].reshape(B_KV_LORA, -1)], axis=1).astype(BF16)

    row = lambda v: v.reshape(1, -1).astype(F32)
    slot_gain = lambda g: jnp.tile(jnp.pad(g, (0, LANES - B_QK)), B_HEADS)
    return {
        "gmix": row(norm_mix_g[l]), "w_in": w_proj, "w_uq": w_uq_x, "w_ukv": w_ukv_x,
        "e64": _block_diag(512, 64), "e128": _block_diag(1024, LANES),
        "gqa": row(jnp.tile(diff_qk_g[l, 0], 8) * (A_DIM ** -0.5)), "gka": row(jnp.tile(diff_qk_g[l, 1], 8)),
        "gcq": row(mla_cq_g[l]), "gckv": row(mla_ckv_g[l]),
        "gqb": row(slot_gain(mla_qk_g[l, 0]) * (B_QK ** -0.5)), "gkb": row(slot_gain(mla_qk_g[l, 1])),
        "gqc": row(jnp.tile(swa_qk_g[l, 0], 8) * (C_DIM ** -0.5)), "gkc": row(jnp.tile(swa_qk_g[l, 1], 4)),
        "w_gate": piece(9).astype(BF16),
        "w_a": w_branch_a[l].astype(BF16), "w_b": w_branch_b[l].astype(BF16), "w_c": w_branch_c[l].astype(BF16),
        "w_o": w_o[l].astype(BF16), "gffn": row(norm_ffn_g[l]), "w_rt": w_router[l].T.astype(F32),
    }


def _rope_slot_tables(positions):
    inv = 1.0 / (ROPE_THETA ** (jnp.arange(0, B_ROPE, 2, dtype=F32) / B_ROPE))
    ang = positions.astype(F32)[..., None] * inv
    cos, sin = jnp.cos(ang), jnp.sin(ang)
    ones = jnp.ones(ang.shape[:-1] + (B_NOPE,), F32)
    pad = jnp.zeros(ang.shape[:-1] + (LANES - B_QK,), F32)
    return (jnp.concatenate([ones, cos, cos, pad], axis=-1),
            jnp.concatenate([jnp.zeros_like(ones), sin, sin, pad], axis=-1))


def _alibi_slopes(n):
    return 2.0 ** (-8.0 * jnp.arange(1, n + 1, dtype=F32) / n)


def kernel(x, positions, norm_mix_g, w_in, diff_qk_g, diff_lambda, diff_out_g, mla_cq_g, w_uq, mla_ckv_g, w_ukv,
           mla_qk_g, swa_qk_g, swa_sink, w_branch_a, w_branch_b, w_branch_c, w_o, norm_ffn_g, w_router,
           w_exp_gate, w_exp_up, w_exp_down):
    b, s, d = x.shape
    depth = w_in.shape[0]
    cap = max(1, EC_CAPACITY * s // N_EXPERTS)
    tm_proj = min(256, s)
    tq = min(128, s)
    tm_merge = min(256, s)
    tt = min(1024, s)

    cos_t, sin_t = _rope_slot_tables(positions)
    pos_f = positions.astype(F32)
    pos_c, pos_r = pos_f[:, :, None], pos_f[:, None, :]
    lane_bcast = lambda v: jnp.broadcast_to(v[..., None], v.shape + (LANES,)).astype(F32)
    slopes_a = lane_bcast(_alibi_slopes(A_HEADS)[:, None])
    slopes_c = lane_bcast(_alibi_slopes(C_HEADS).reshape(C_KV_HEADS, C_REP))

    for l in range(depth):
        p = _layer_params(l, w_in, norm_mix_g, diff_qk_g, mla_cq_g, w_uq, mla_ckv_g, w_ukv, mla_qk_g, swa_qk_g,
                          w_branch_a, w_branch_b, w_branch_c, w_o, norm_ffn_g, w_router)
        qa, ka, va, qb, kb, vb, qc, kc, vc = _proj_call(x, cos_t, sin_t, p, tm_proj)
        lam_init = 0.8 - 0.6 * math.exp(-0.3 * l)
        oa = _attn_a_call(qa, ka, va, pos_c, pos_r, slopes_a, diff_lambda[l].astype(F32),
                          diff_out_g[l].reshape(1, -1).astype(F32), lam_init, tq)
        ob = _attn_b_call(qb, kb, vb, tq)
        oc = _attn_c_call(qc, kc, vc, slopes_c, lane_bcast(swa_sink[l].reshape(C_KV_HEADS, C_REP)))
        xn, h2, aff_t = _merge_call(x, oa, ob, oc, p, tm_merge)
        pos4 = _topk_call(aff_t, cap).reshape(b, N_EXPERTS, 1, s)
        xin, asel = _gather_call(pos4, aff_t.reshape(b, N_EXPERTS, 1, s), h2, cap)
        y = _expert_call(xin, asel, w_exp_gate[l].astype(BF16), w_exp_up[l].astype(BF16),
                         w_exp_down[l].astype(BF16))
        x = _combine_call(xn, pos4, y, tt)
    return x
```

```python
import functools
import math

import numpy as np
import jax
import jax.numpy as jnp
from jax import lax
from jax.experimental import pallas as pl
from jax.experimental.pallas import tpu as pltpu

F32 = jnp.float32
BF16 = jnp.bfloat16

D_MODEL = 1024
EPS = 1e-6
A_HEADS = 4
A_DIM = 64
A_VDIM = 128
B_HEADS = 8
B_NOPE = 64
B_ROPE = 32
B_VDIM = 64
B_QK = B_NOPE + B_ROPE
B_Q_LORA = 384
B_KV_LORA = 256
ROPE_THETA = 10000.0
C_HEADS = 8
C_KV_HEADS = 2
C_REP = C_HEADS // C_KV_HEADS
C_DIM = 64
WINDOW = 128
N_EXPERTS = 16
EC_CAPACITY = 2
D_FF = 1024
LOG2E = math.log2(math.e)
SEG_TILE = 256
LANES = 128

_QA = 0
_KA = 512
_VA = 1024
_CQ = 1536
_CKV = 1920
_QC = 2176
_KC = 2688
_VC = 2944
_KR = 3200
_KRR = 3328
_PROJ_COLS = 3456

VMEM_LIMIT = 56 * 1024 * 1024

_NT = (((1,), (1,)), ((), ()))
_TN = (((0,), (0,)), ((), ()))


def _cparams(sem):
    return pltpu.CompilerParams(dimension_semantics=sem, vmem_limit_bytes=VMEM_LIMIT)


def _dot(a, b):
    return jnp.dot(a, b, preferred_element_type=F32)


def _seg_sum(x2, e):
    hi = x2.astype(BF16)
    lo = (x2 - hi.astype(F32)).astype(BF16)
    slabs = [_dot(hi[:, c:c + SEG_TILE], e) + _dot(lo[:, c:c + SEG_TILE], e)
             for c in range(0, x2.shape[1], SEG_TILE)]
    return slabs[0] if len(slabs) == 1 else jnp.concatenate(slabs, axis=1)


def _row_rms(x, g):
    return x * lax.rsqrt(jnp.mean(x * x, axis=-1, keepdims=True) + EPS) * g


def _proj_kernel(x_ref, gmix_ref, w_ref, wuq_ref, wukv_ref, e64_ref, e128_ref,
                 gqa_ref, gka_ref, gcq_ref, gckv_ref, gqb_ref, gkb_ref, gqc_ref, gkc_ref,
                 cos_ref, sin_ref,
                 qa_o, ka_o, va_o, qb_o, kb_o, vb_o, qc_o, kc_o, vc_o):
    hb = _row_rms(x_ref[0], gmix_ref[...]).astype(BF16)

    def proj(a, n):
        return _dot(hb, w_ref[:, a:a + n])

    e64 = e64_ref[...]
    e128 = e128_ref[...]

    def seg_norm(v, e, width, g):
        return v * lax.rsqrt(_seg_sum(v * v, e) * (1.0 / width) + EPS) * g

    def store_slots(o_ref, v, n):
        for j in range(n):
            o_ref[0, j] = v[:, LANES * j:LANES * (j + 1)].astype(o_ref.dtype)

    ones_slot = jnp.where(lax.broadcasted_iota(jnp.int32, (hb.shape[0], LANES), 1) == 0, 1.0, 0.0)

    def store_value_slots(o_ref, v, n):
        for j in range(n):
            o_ref[0, j] = jnp.concatenate([v[:, LANES * j:LANES * (j + 1)], ones_slot], axis=1).astype(o_ref.dtype)

    store_slots(qa_o, seg_norm(proj(_QA, 512), e64, A_DIM, gqa_ref[...]), A_HEADS)
    store_slots(ka_o, seg_norm(proj(_KA, 512), e64, A_DIM, gka_ref[...]), A_HEADS)
    store_value_slots(va_o, proj(_VA, 512), A_HEADS)

    cos_t = cos_ref[0]
    sin_t = sin_ref[0]
    cos8 = jnp.concatenate([cos_t] * B_HEADS, axis=1)
    sin8 = jnp.concatenate([sin_t] * B_HEADS, axis=1)
    cq = _row_rms(proj(_CQ, B_Q_LORA), gcq_ref[...]).astype(BF16)
    q2 = _dot(cq, wuq_ref[...])
    qb = q2[:, :1024] * cos8 + q2[:, 1024:] * sin8
    store_slots(qb_o, seg_norm(qb, e128, B_QK, gqb_ref[...]), B_HEADS)
    ckv = _row_rms(proj(_CKV, B_KV_LORA), gckv_ref[...]).astype(BF16)
    kv = _dot(ckv, wukv_ref[...])
    kr = proj(_KR, LANES) * cos_t + proj(_KRR, LANES) * sin_t
    kb = kv[:, :1024] + jnp.concatenate([kr] * B_HEADS, axis=1)
    store_slots(kb_o, seg_norm(kb, e128, B_QK, gkb_ref[...]), B_HEADS)
    store_value_slots(vb_o, kv[:, 1024:], B_HEADS // 2)

    store_slots(qc_o, seg_norm(proj(_QC, 512), e64, C_DIM, gqc_ref[...]), C_HEADS // 2)
    store_slots(kc_o, seg_norm(proj(_KC, 256), e64, C_DIM, gkc_ref[...]), C_KV_HEADS)
    store_slots(vc_o, proj(_VC, 256), C_KV_HEADS)


def _proj_call(x, cos_t, sin_t, p, tm):
    b, s, d = x.shape
    full = lambda a: pl.BlockSpec(a.shape, lambda i, j: (0,) * a.ndim, pipeline_mode=pl.Buffered(1))
    slot = lambda nw: pl.BlockSpec((1, nw[0], tm, nw[1]), lambda i, j: (i, 0, j, 0))
    tok = lambda w: pl.BlockSpec((1, tm, w), lambda i, j: (i, j, 0))
    consts = [p["gmix"], p["w_in"], p["w_uq"], p["w_ukv"], p["e64"], p["e128"],
              p["gqa"], p["gka"], p["gcq"], p["gckv"], p["gqb"], p["gkb"], p["gqc"], p["gkc"]]
    slots = [(A_HEADS, LANES), (A_HEADS, LANES), (A_HEADS, 2 * LANES), (B_HEADS, LANES), (B_HEADS, LANES),
             (B_HEADS // 2, 2 * LANES), (C_HEADS // 2, LANES), (C_KV_HEADS, LANES), (C_KV_HEADS, LANES)]
    return pl.pallas_call(
        _proj_kernel,
        grid=(b, s // tm),
        in_specs=[tok(d)] + [full(a) for a in consts] + [tok(LANES), tok(LANES)],
        out_specs=[slot(nw) for nw in slots],
        out_shape=[jax.ShapeDtypeStruct((b, nw[0], s, nw[1]), BF16) for nw in slots],
        compiler_params=_cparams(("parallel", "parallel")),
        name="proj",
    )(x, *consts, cos_t, sin_t)


def _attn_a_kernel(q_ref, k_ref, v_ref, pc_ref, pr_ref, slope_ref, lam_ref, og_ref, o_ref, s_scr, *, lam_init, ck):
    q = q_ref[0, 0]
    tq = q.shape[0]
    seq = k_ref.shape[2]
    lane = lax.broadcasted_iota(jnp.int32, q.shape, 1)
    zero = jnp.zeros_like(q)
    qm = [jnp.where(lane < A_DIM, q, zero), jnp.where(lane >= A_DIM, q, zero)]
    slope = slope_ref[0][:, :1]
    pos_q = pc_ref[0]
    mx = [jnp.full((tq, LANES), -jnp.inf, F32)] * 2
    for c in range(seq // ck):
        cols = slice(c * ck, (c + 1) * ck)
        k_c = k_ref[0, 0, cols, :]
        bias = slope * jnp.abs(pos_q - pr_ref[0, :, cols])
        for m in range(2):
            sc = lax.dot_general(qm[m], k_c, _NT, preferred_element_type=F32) - bias
            s_scr[m, :, cols] = sc
            for j in range(ck // LANES):
                mx[m] = jnp.maximum(mx[m], sc[:, j * LANES:(j + 1) * LANES])
    v = v_ref[0, 0]
    acc = []
    for m in range(2):
        e = jnp.exp2(s_scr[m] - jnp.max(mx[m], axis=-1, keepdims=True)).astype(BF16)
        acc.append(_dot(e, v))
    lp = lam_ref[...]
    lam = (jnp.exp(jnp.sum(lp[0:1] * lp[1:2], axis=-1, keepdims=True))
           - jnp.exp(jnp.sum(lp[2:3] * lp[3:4], axis=-1, keepdims=True)) + lam_init)
    o = (acc[0][:, :LANES] * (1.0 / acc[0][:, LANES:LANES + 1])
         - acc[1][:, :LANES] * (lam / acc[1][:, LANES:LANES + 1]))
    o_ref[0] = (_row_rms(o, og_ref[...]) * (1.0 - lam_init)).astype(o_ref.dtype)


def _attn_a_call(qa, ka, va, pos_c, pos_r, slopes, lam_p, out_g, lam_init, tq):
    b, h, s, _ = qa.shape
    kv_spec = lambda w: pl.BlockSpec((1, 1, s, w), lambda i, j, t: (i, j, 0, 0))
    return pl.pallas_call(
        functools.partial(_attn_a_kernel, lam_init=lam_init, ck=min(512, s)),
        grid=(b, h, s // tq),
        scratch_shapes=[pltpu.VMEM((2, tq, s), F32)],
        in_specs=[pl.BlockSpec((1, 1, tq, LANES), lambda i, j, t: (i, j, t, 0)), kv_spec(LANES), kv_spec(2 * LANES),
                  pl.BlockSpec((1, tq, 1), lambda i, j, t: (i, t, 0)),
                  pl.BlockSpec((1, 1, s), lambda i, j, t: (i, 0, 0)),
                  pl.BlockSpec((1, 1, LANES), lambda i, j, t: (j, 0, 0)),
                  pl.BlockSpec(lam_p.shape, lambda i, j, t: (0, 0)),
                  pl.BlockSpec(out_g.shape, lambda i, j, t: (0, 0))],
        out_specs=pl.BlockSpec((1, tq, LANES), lambda i, j, t: (i, t, j)),
        out_shape=jax.ShapeDtypeStruct((b, s, h * LANES), BF16),
        compiler_params=_cparams(("parallel", "parallel", "arbitrary")),
        name="attn_a",
    )(qa, ka, va, pos_c, pos_r, slopes, lam_p, out_g)


def _attn_b_kernel(q_ref, k_ref, v_ref, o_ref):
    tq = q_ref.shape[2]
    es = []
    for j in range(2):
        s = lax.dot_general(q_ref[0, j], k_ref[0, j], _NT, preferred_element_type=F32)
        es.append(jnp.exp2(s - jnp.max(s, axis=-1, keepdims=True)).astype(BF16))
    acc = _dot(jnp.concatenate(es, axis=0), v_ref[0, 0])
    lane = lax.broadcasted_iota(jnp.int32, (tq, LANES), 1)
    o_ref[0] = jnp.where(lane < B_VDIM, acc[:tq, :LANES] * (1.0 / acc[:tq, LANES:LANES + 1]),
                         acc[tq:, :LANES] * (1.0 / acc[tq:, LANES:LANES + 1])).astype(o_ref.dtype)


def _attn_b_call(qb, kb, vb, tq):
    b, h, s, _ = qb.shape
    return pl.pallas_call(
        _attn_b_kernel,
        grid=(b, h // 2, s // tq),
        in_specs=[pl.BlockSpec((1, 2, tq, LANES), lambda i, j, t: (i, j, t, 0)),
                  pl.BlockSpec((1, 2, s, LANES), lambda i, j, t: (i, j, 0, 0)),
                  pl.BlockSpec((1, 1, s, 2 * LANES), lambda i, j, t: (i, j, 0, 0))],
        out_specs=pl.BlockSpec((1, tq, LANES), lambda i, j, t: (i, t, j)),
        out_shape=jax.ShapeDtypeStruct((b, s, (h // 2) * LANES), BF16),
        compiler_params=_cparams(("parallel", "parallel", "arbitrary")),
        name="attn_b",
    )(qb, kb, vb)


def _attn_c_kernel(q_ref, kp_ref, ko_ref, kn_ref, vp_ref, vo_ref, vn_ref, slope_ref, sink_ref, o_ref, *, seq):
    w = WINDOW
    nsub = ko_ref.shape[2] // w
    n0 = pl.program_id(2) * nsub
    kcat = jnp.concatenate([kp_ref[0, 0], ko_ref[0, 0], kn_ref[0, 0]], axis=0)
    vcat = jnp.concatenate([vp_ref[0, 0], vo_ref[0, 0], vn_ref[0, 0]], axis=0)
    lane = lax.broadcasted_iota(jnp.int32, (w, LANES), 1)
    r_idx = lax.broadcasted_iota(jnp.int32, (w, 3 * w), 0)
    c_idx = lax.broadcasted_iota(jnp.int32, (w, 3 * w), 1)
    arel = jnp.abs(c_idx - w - r_idx)
    dist = arel.astype(F32)
    slopes = slope_ref[0]
    sinks = sink_ref[0]
    for i in range(nsub):
        rows = slice(i * w, (i + 1) * w)
        parts = []
        for p in range(2):
            q = q_ref[0, p, rows, :]
            zero = jnp.zeros_like(q)
            parts += [jnp.where(lane < C_DIM, q, zero), jnp.where(lane >= C_DIM, q, zero)]
        qz = jnp.concatenate(parts, axis=0)
        s = lax.dot_general(qz, kcat[i * w:(i + 3) * w], _NT, preferred_element_type=F32)
        kidx = (n0 + i - 1) * w + c_idx
        valid = jnp.where(arel <= w, jnp.where(kidx >= 0, jnp.where(kidx < seq, 1, 0), 0), 0) > 0
        v_i = vcat[i * w:(i + 3) * w]
        outs = []
        for r in range(C_REP):
            sc = jnp.where(valid, s[r * w:(r + 1) * w] - slopes[r:r + 1, :1] * dist, -1e30)
            sk = sinks[r:r + 1, :1]
            m = jnp.maximum(jnp.max(sc, axis=-1, keepdims=True), sk)
            e = jnp.exp(sc - m)
            den = jnp.sum(e, axis=-1, keepdims=True) + jnp.exp(sk - m)
            outs.append(_dot(e.astype(BF16), v_i) * (1.0 / den))
        pair0 = jnp.where(lane < C_DIM, outs[0], outs[1])
        pair1 = jnp.where(lane < C_DIM, outs[2], outs[3])
        o_ref[0, rows, :] = jnp.concatenate([pair0, pair1], axis=1).astype(o_ref.dtype)


def _attn_c_call(qc, kc, vc, slopes, sinks):
    b, _, s, _ = qc.shape
    nb = s // WINDOW
    nsub = min(4, nb)
    tq = nsub * WINDOW
    halo = lambda f: pl.BlockSpec((1, 1, WINDOW, LANES), f)
    prev = halo(lambda i, g, n: (i, g, jnp.maximum(n * nsub - 1, 0), 0))
    own = pl.BlockSpec((1, 1, tq, LANES), lambda i, g, n: (i, g, n, 0))
    nxt = halo(lambda i, g, n: (i, g, jnp.minimum((n + 1) * nsub, nb - 1), 0))
    per_group = pl.BlockSpec((1, C_REP, LANES), lambda i, g, n: (g, 0, 0))
    return pl.pallas_call(
        functools.partial(_attn_c_kernel, seq=s),
        grid=(b, C_KV_HEADS, s // tq),
        in_specs=[pl.BlockSpec((1, 2, tq, LANES), lambda i, g, n: (i, g, n, 0)),
                  prev, own, nxt, prev, own, nxt, per_group, per_group],
        out_specs=pl.BlockSpec((1, tq, 2 * LANES), lambda i, g, n: (i, n, g)),
        out_shape=jax.ShapeDtypeStruct((b, s, C_HEADS * C_DIM), BF16),
        compiler_params=_cparams(("parallel", "parallel", "arbitrary")),
        name="attn_c",
    )(qc, kc, kc, kc, vc, vc, vc, slopes, sinks)


def _merge_kernel(x_ref, gmix_ref, wg_ref, oa_ref, ob_ref, oc_ref, wa_ref, wb_ref, wc_ref, wo_ref,
                  gffn_ref, wr_ref, xn_o, h2_o, aff_o):
    d = D_MODEL
    x = x_ref[0]
    hb = _row_rms(x, gmix_ref[...]).astype(BF16)
    g = 1.0 / (1.0 + jnp.exp(-_dot(hb, wg_ref[...])))
    merged = (g[:, :d] * _dot(oa_ref[0], wa_ref[...]) + g[:, d:2 * d] * _dot(ob_ref[0], wb_ref[...])
              + g[:, 2 * d:] * _dot(oc_ref[0], wc_ref[...]))
    xn = x + _dot(merged.astype(BF16), wo_ref[...])
    xn_o[0] = xn
    h2 = _row_rms(xn, gffn_ref[...])
    h2_o[0] = h2.astype(BF16)
    logits = lax.dot_general(wr_ref[...], h2, _NT, preferred_element_type=F32,
                             precision=lax.Precision.HIGHEST)
    ex = jnp.exp(logits - jnp.max(logits, axis=0, keepdims=True))
    aff_o[0] = ex / jnp.sum(ex, axis=0, keepdims=True)


def _merge_call(x, oa, ob, oc, p, tm):
    b, s, d = x.shape
    full = lambda a: pl.BlockSpec(a.shape, lambda i, j: (0,) * a.ndim, pipeline_mode=pl.Buffered(1))
    tok = lambda w: pl.BlockSpec((1, tm, w), lambda i, j: (i, j, 0))
    return pl.pallas_call(
        _merge_kernel,
        grid=(b, s // tm),
        in_specs=[tok(d), full(p["gmix"]), full(p["w_gate"]), tok(512), tok(512), tok(512),
                  full(p["w_a"]), full(p["w_b"]), full(p["w_c"]), full(p["w_o"]), full(p["gffn"]), full(p["w_rt"])],
        out_specs=[tok(d), tok(d), pl.BlockSpec((1, N_EXPERTS, tm), lambda i, j: (i, 0, j))],
        out_shape=[jax.ShapeDtypeStruct((b, s, d), F32), jax.ShapeDtypeStruct((b, s, d), BF16),
                   jax.ShapeDtypeStruct((b, N_EXPERTS, s), F32)],
        compiler_params=_cparams(("parallel", "parallel")),
        name="merge",
    )(x, p["gmix"], p["w_gate"], oa, ob, oc, p["w_a"], p["w_b"], p["w_c"], p["w_o"], p["gffn"], p["w_rt"])


def _cumsum_lanes(mask01, chunk):
    rows, s = mask01.shape
    tri = jnp.where(lax.broadcasted_iota(jnp.int32, (chunk, chunk), 0)
                    <= lax.broadcasted_iota(jnp.int32, (chunk, chunk), 1), 1.0, 0.0).astype(BF16)
    carry = jnp.zeros((rows, 1), F32)
    outs = []
    for c in range(s // chunk):
        cs = _dot(mask01[:, c * chunk:(c + 1) * chunk], tri) + carry
        outs.append(cs)
        carry = cs[:, chunk - 1:chunk]
    return jnp.concatenate(outs, axis=1)


def _topk_kernel(aff_ref, pos_o, *, cap, chunk):
    bits = pltpu.bitcast(aff_ref[0], jnp.int32)
    rows = bits.shape[0]
    capf = float(cap)

    def count(mask):
        return jnp.sum(jnp.where(mask, 1.0, 0.0), axis=1, keepdims=True)

    def body(_, c):
        lo, hi = c
        mid = lo + ((hi - lo + 1) >> 1)
        ok = count(bits >= mid) >= capf
        return jnp.where(ok, mid, lo), jnp.where(ok, hi, mid - 1)

    lo0 = jnp.zeros((rows, 1), jnp.int32)
    hi0 = jnp.full((rows, 1), 0x7F800000, jnp.int32)
    thr, _ = lax.fori_loop(0, 31, body, (lo0, hi0))
    gt = bits > thr
    eq = bits == thr
    need = capf - count(gt)
    eq_rank = _cumsum_lanes(jnp.where(eq, 1.0, 0.0).astype(BF16), chunk)
    sel = jnp.where(gt, 1.0, jnp.where(eq, jnp.where(eq_rank <= need, 1.0, 0.0), 0.0))
    slot = _cumsum_lanes(sel.astype(BF16), chunk) - 1.0
    pos_o[0] = jnp.where(sel > 0.0, slot, -1.0).astype(jnp.int32)


def _topk_call(aff_t, cap):
    b, e, s = aff_t.shape
    spec = pl.BlockSpec((1, e, s), lambda i: (i, 0, 0))
    return pl.pallas_call(
        functools.partial(_topk_kernel, cap=cap, chunk=min(512, s)),
        grid=(b,),
        in_specs=[spec],
        out_specs=spec,
        out_shape=jax.ShapeDtypeStruct((b, e, s), jnp.int32),
        compiler_params=_cparams(("parallel",)),
        name="topk",
    )(aff_t)


def _gather_kernel(pos_ref, aff_ref, h_ref, xin_o, asel_o, *, cap):
    pos = pos_ref[0, 0]
    hit = lax.broadcasted_iota(jnp.int32, (cap, pos.shape[1]), 0) == pos
    onehot = jnp.where(hit, 1.0, 0.0).astype(BF16)
    xin_o[0, 0] = _dot(onehot, h_ref[0]).astype(BF16)
    asel = jnp.sum(jnp.where(hit, aff_ref[0, 0], 0.0), axis=1, keepdims=True)
    asel_o[0, 0] = jnp.broadcast_to(asel, (cap, LANES))


def _gather_call(pos4, aff4, h2, cap):
    b, e, _, s = pos4.shape
    d = h2.shape[-1]
    row = pl.BlockSpec((1, 1, 1, s), lambda i, j: (i, j, 0, 0))
    return pl.pallas_call(
        functools.partial(_gather_kernel, cap=cap),
        grid=(b, e),
        in_specs=[row, row, pl.BlockSpec((1, s, d), lambda i, j: (i, 0, 0))],
        out_specs=[pl.BlockSpec((1, 1, cap, d), lambda i, j: (i, j, 0, 0)),
                   pl.BlockSpec((1, 1, cap, LANES), lambda i, j: (i, j, 0, 0))],
        out_shape=[jax.ShapeDtypeStruct((b, e, cap, d), BF16), jax.ShapeDtypeStruct((b, e, cap, LANES), F32)],
        compiler_params=_cparams(("parallel", "arbitrary")),
        name="gather",
    )(pos4, aff4, h2)


def _expert_kernel(xin_ref, asel_ref, wg_ref, wu_ref, wd_ref, y_o):
    xin = xin_ref[0, 0]
    g = _dot(xin, wg_ref[0])
    u = _dot(xin, wu_ref[0])
    hid = (g * (1.0 / (1.0 + jnp.exp(-g))) * u).astype(BF16)
    y_o[0, 0] = (_dot(hid, wd_ref[0]) * asel_ref[0, 0][:, :1]).astype(y_o.dtype)


def _expert_call(xin, asel, wg, wu, wd):
    b, e, cap, d = xin.shape
    f = wg.shape[-1]
    tokens = lambda w: pl.BlockSpec((1, 1, cap, w), lambda j, i: (i, j, 0, 0))
    return pl.pallas_call(
        _expert_kernel,
        grid=(e, b),
        in_specs=[tokens(d), tokens(LANES),
                  pl.BlockSpec((1, d, f), lambda j, i: (j, 0, 0)),
                  pl.BlockSpec((1, d, f), lambda j, i: (j, 0, 0)),
                  pl.BlockSpec((1, f, d), lambda j, i: (j, 0, 0))],
        out_specs=tokens(d),
        out_shape=jax.ShapeDtypeStruct((b, e, cap, d), BF16),
        compiler_params=_cparams(("parallel", "arbitrary")),
        name="expert",
    )(xin, asel, wg, wu, wd)


def _combine_kernel(x_ref, pos_ref, y_ref, o_ref, *, cap):
    @pl.when(pl.program_id(2) == 0)
    def _():
        o_ref[...] = x_ref[...]

    pos = pos_ref[0, 0]
    hit = lax.broadcasted_iota(jnp.int32, (cap, pos.shape[1]), 0) == pos
    onehot = jnp.where(hit, 1.0, 0.0).astype(BF16)
    o_ref[0] += lax.dot_general(onehot, y_ref[0, 0], _TN, preferred_element_type=F32)


def _combine_call(xn, pos4, y, tt):
    b, s, d = xn.shape
    e, cap = y.shape[1], y.shape[2]
    tok = pl.BlockSpec((1, tt, d), lambda i, t, j: (i, t, 0))
    return pl.pallas_call(
        functools.partial(_combine_kernel, cap=cap),
        grid=(b, s // tt, e),
        in_specs=[tok, pl.BlockSpec((1, 1, 1, tt), lambda i, t, j: (i, j, 0, t)),
                  pl.BlockSpec((1, 1, cap, d), lambda i, t, j: (i, j, 0, 0))],
        out_specs=tok,
        out_shape=jax.ShapeDtypeStruct((b, s, d), F32),
        compiler_params=_cparams(("parallel", "parallel", "arbitrary")),
        name="combine",
    )(xn, pos4, y)


def _block_diag(n, blk):
    i = np.arange(n)
    return jnp.asarray((i[:, None] // blk) == (i[None, :] // blk), dtype=BF16)


def _head_slots(w, heads, width):
    r = w.shape[0]
    return jnp.pad(w.reshape(r, heads, width), ((0, 0), (0, 0), (0, LANES - width))).reshape(r, heads * LANES)


def _rotate_half_cols(w):
    half = B_ROPE // 2
    return jnp.concatenate([-w[..., half:], w[..., :half]], axis=-1)


def _layer_params(l, w_in, norm_mix_g, diff_qk_g, mla_cq_g, w_uq, mla_ckv_g, w_ukv, mla_qk_g, swa_qk_g,
                  w_branch_a, w_branch_b, w_branch_c, w_o, norm_ffn_g, w_router):
    d = D_MODEL
    wi = w_in[l]
    off = np.cumsum([0, 512, 512, 512, B_Q_LORA, B_KV_LORA, B_ROPE, 512, 128, 128, 3 * d])
    piece = lambda k: wi[:, off[k]:off[k + 1]]
    maps_to_heads = lambda w: w.reshape(d, 2, A_HEADS, A_DIM).transpose(0, 2, 1, 3).reshape(d, 512)
    dup = lambda w: jnp.concatenate([w.reshape(d, C_KV_HEADS, 1, C_DIM)] * 2, axis=2).reshape(d, 256)
    kr = piece(5)
    rope_slot = lambda w: jnp.pad(w, ((0, 0), (B_NOPE, LANES - B_QK)))
    w_proj = jnp.concatenate(
        [maps_to_heads(piece(0)), maps_to_heads(piece(1)), piece(2), piece(3), piece(4), piece(6),
         dup(piece(7)), dup(piece(8)), rope_slot(kr), rope_slot(_rotate_half_cols(kr))], axis=1).astype(BF16)
    assert w_proj.shape == (d, _PROJ_COLS)

    wq = w_uq[l].reshape(B_Q_LORA, B_HEADS, B_QK)
    wq_rot = jnp.concatenate([jnp.zeros_like(wq[..., :B_NOPE]), _rotate_half_cols(wq[..., B_NOPE:])], axis=-1)
    w_uq_x = jnp.concatenate([_head_slots(wq.reshape(B_Q_LORA, -1), B_HEADS, B_QK),
                              _head_slots(wq_rot.reshape(B_Q_LORA, -1), B_HEADS, B_QK)], axis=1).astype(BF16)
    wkv = w_ukv[l].reshape(B_KV_LORA, B_HEADS, B_NOPE + B_VDIM)
    w_ukv_x = jnp.concatenate([_head_slots(wkv[..., :B_NOPE].reshape(B_KV_LORA, -1), B_HEADS, B_NOPE),
                               wkv[..., B_NOPE:].reshape(B_KV_LORA, -1)], axis=1).astype(BF16)

    row = lambda v: v.reshape(1, -1).astype(F32)
    slot_gain = lambda g: jnp.tile(jnp.pad(g, (0, LANES - B_QK)), B_HEADS)
    return {
        "gmix": row(norm_mix_g[l]), "w_in": w_proj, "w_uq": w_uq_x, "w_ukv": w_ukv_x,
        "e64": _block_diag(SEG_TILE, 64), "e128": _block_diag(SEG_TILE, LANES),
        "gqa": row(jnp.tile(diff_qk_g[l, 0], 8) * (A_DIM ** -0.5 * LOG2E)), "gka": row(jnp.tile(diff_qk_g[l, 1], 8)),
        "gcq": row(mla_cq_g[l]), "gckv": row(mla_ckv_g[l]),
        "gqb": row(slot_gain(mla_qk_g[l, 0]) * (B_QK ** -0.5 * LOG2E)), "gkb": row(slot_gain(mla_qk_g[l, 1])),
        "gqc": row(jnp.tile(swa_qk_g[l, 0], 8) * (C_DIM ** -0.5)), "gkc": row(jnp.tile(swa_qk_g[l, 1], 4)),
        "w_gate": piece(9).astype(BF16),
        "w_a": w_branch_a[l].astype(BF16), "w_b": w_branch_b[l].astype(BF16), "w_c": w_branch_c[l].astype(BF16),
        "w_o": w_o[l].astype(BF16), "gffn": row(norm_ffn_g[l]), "w_rt": w_router[l].T.astype(F32),
    }


def _rope_slot_tables(positions):
    inv = 1.0 / (ROPE_THETA ** (jnp.arange(0, B_ROPE, 2, dtype=F32) / B_ROPE))
    ang = positions.astype(F32)[..., None] * inv
    cos, sin = jnp.cos(ang), jnp.sin(ang)
    ones = jnp.ones(ang.shape[:-1] + (B_NOPE,), F32)
    pad = jnp.zeros(ang.shape[:-1] + (LANES - B_QK,), F32)
    return (jnp.concatenate([ones, cos, cos, pad], axis=-1),
            jnp.concatenate([jnp.zeros_like(ones), sin, sin, pad], axis=-1))


def _alibi_slopes(n):
    return 2.0 ** (-8.0 * jnp.arange(1, n + 1, dtype=F32) / n)


def kernel(x, positions, norm_mix_g, w_in, diff_qk_g, diff_lambda, diff_out_g, mla_cq_g, w_uq, mla_ckv_g, w_ukv,
           mla_qk_g, swa_qk_g, swa_sink, w_branch_a, w_branch_b, w_branch_c, w_o, norm_ffn_g, w_router,
           w_exp_gate, w_exp_up, w_exp_down):
    b, s, d = x.shape
    depth = w_in.shape[0]
    cap = max(1, EC_CAPACITY * s // N_EXPERTS)
    tm_proj = min(512, s)
    tq = min(256, s)
    tq_a = min(256, s)
    tm_merge = min(512, s)
    tt = min(1024, s)

    cos_t, sin_t = _rope_slot_tables(positions)
    pos_f = positions.astype(F32)
    pos_c, pos_r = pos_f[:, :, None], pos_f[:, None, :]
    lane_bcast = lambda v: jnp.broadcast_to(v[..., None], v.shape + (LANES,)).astype(F32)
    slopes_a = lane_bcast(_alibi_slopes(A_HEADS)[:, None] * LOG2E)
    slopes_c = lane_bcast(_alibi_slopes(C_HEADS).reshape(C_KV_HEADS, C_REP))

    for l in range(depth):
        p = _layer_params(l, w_in, norm_mix_g, diff_qk_g, mla_cq_g, w_uq, mla_ckv_g, w_ukv, mla_qk_g, swa_qk_g,
                          w_branch_a, w_branch_b, w_branch_c, w_o, norm_ffn_g, w_router)
        qa, ka, va, qb, kb, vb, qc, kc, vc = _proj_call(x, cos_t, sin_t, p, tm_proj)
        lam_init = 0.8 - 0.6 * math.exp(-0.3 * l)
        oa = _attn_a_call(qa, ka, va, pos_c, pos_r, slopes_a, diff_lambda[l].astype(F32),
                          diff_out_g[l].reshape(1, -1).astype(F32), lam_init, tq_a)
        ob = _attn_b_call(qb, kb, vb, tq)
        oc = _attn_c_call(qc, kc, vc, slopes_c, lane_bcast(swa_sink[l].reshape(C_KV_HEADS, C_REP)))
        xn, h2, aff_t = _merge_call(x, oa, ob, oc, p, tm_merge)
        pos4 = _topk_call(aff_t, cap).reshape(b, N_EXPERTS, 1, s)
        xin, asel = _gather_call(pos4, aff_t.reshape(b, N_EXPERTS, 1, s), h2, cap)
        y = _expert_call(xin, asel, w_exp_gate[l].astype(BF16), w_exp_up[l].astype(BF16),
                         w_exp_down[l].astype(BF16))
        x = _combine_call(xn, pos4, y, tt)
    return x
```

```python
import functools
import math

import numpy as np
import jax
import jax.numpy as jnp
from jax import lax
from jax.experimental import pallas as pl
from jax.experimental.pallas import tpu as pltpu

F32 = jnp.float32
BF16 = jnp.bfloat16

D_MODEL = 1024
EPS = 1e-6
A_HEADS = 4
A_DIM = 64
A_VDIM = 128
B_HEADS = 8
B_NOPE = 64
B_ROPE = 32
B_VDIM = 64
B_QK = B_NOPE + B_ROPE
B_Q_LORA = 384
B_KV_LORA = 256
ROPE_THETA = 10000.0
C_HEADS = 8
C_KV_HEADS = 2
C_REP = C_HEADS // C_KV_HEADS
C_DIM = 64
WINDOW = 128
N_EXPERTS = 16
EC_CAPACITY = 2
D_FF = 1024
LOG2E = math.log2(math.e)
SEG_TILE = 256
LANES = 128

_QA = 0
_KA = 512
_VA = 1024
_CQ = 1536
_CKV = 1920
_QC = 2176
_KC = 2688
_VC = 2944
_KR = 3200
_KRR = 3328
_PROJ_COLS = 3456

VMEM_LIMIT = 56 * 1024 * 1024

_NT = (((1,), (1,)), ((), ()))
_TN = (((0,), (0,)), ((), ()))


def _cparams(sem):
    return pltpu.CompilerParams(dimension_semantics=sem, vmem_limit_bytes=VMEM_LIMIT)


def _dot(a, b):
    return jnp.dot(a, b, preferred_element_type=F32)


def _seg_sum(x2, e):
    hi = x2.astype(BF16)
    lo = (x2 - hi.astype(F32)).astype(BF16)
    slabs = [_dot(hi[:, c:c + SEG_TILE], e) + _dot(lo[:, c:c + SEG_TILE], e)
             for c in range(0, x2.shape[1], SEG_TILE)]
    return slabs[0] if len(slabs) == 1 else jnp.concatenate(slabs, axis=1)


def _row_rms(x, g):
    return x * lax.rsqrt(jnp.mean(x * x, axis=-1, keepdims=True) + EPS) * g


def _proj_kernel(x_ref, gmix_ref, w_ref, wuq_ref, wukv_ref, e64_ref, e128_ref,
                 gqa_ref, gka_ref, gcq_ref, gckv_ref, gqb_ref, gkb_ref, gqc_ref, gkc_ref,
                 cos_ref, sin_ref,
                 qa_o, ka_o, va_o, qb_o, kb_o, vb_o, qc_o, kc_o, vc_o):
    hb = _row_rms(x_ref[0], gmix_ref[...]).astype(BF16)

    def proj(a, n):
        return _dot(hb, w_ref[:, a:a + n])

    e64 = e64_ref[...]
    e128 = e128_ref[...]

    def seg_norm(v, e, width, g):
        return v * lax.rsqrt(_seg_sum(v * v, e) * (1.0 / width) + EPS) * g

    def store_slots(o_ref, v, n):
        for j in range(n):
            o_ref[0, j] = v[:, LANES * j:LANES * (j + 1)].astype(o_ref.dtype)

    ones_slot = jnp.where(lax.broadcasted_iota(jnp.int32, (hb.shape[0], LANES), 1) == 0, 1.0, 0.0)

    def store_value_slots(o_ref, v, n):
        for j in range(n):
            o_ref[0, j] = jnp.concatenate([v[:, LANES * j:LANES * (j + 1)], ones_slot], axis=1).astype(o_ref.dtype)

    store_slots(qa_o, seg_norm(proj(_QA, 512), e64, A_DIM, gqa_ref[...]), A_HEADS)
    store_slots(ka_o, seg_norm(proj(_KA, 512), e64, A_DIM, gka_ref[...]), A_HEADS)
    store_value_slots(va_o, proj(_VA, 512), A_HEADS)

    cos_t = cos_ref[0]
    sin_t = sin_ref[0]
    cos8 = jnp.concatenate([cos_t] * B_HEADS, axis=1)
    sin8 = jnp.concatenate([sin_t] * B_HEADS, axis=1)
    cq = _row_rms(proj(_CQ, B_Q_LORA), gcq_ref[...]).astype(BF16)
    q2 = _dot(cq, wuq_ref[...])
    qb = q2[:, :1024] * cos8 + q2[:, 1024:] * sin8
    store_slots(qb_o, seg_norm(qb, e128, B_QK, gqb_ref[...]), B_HEADS)
    ckv = _row_rms(proj(_CKV, B_KV_LORA), gckv_ref[...]).astype(BF16)
    kv = _dot(ckv, wukv_ref[...])
    kr = proj(_KR, LANES) * cos_t + proj(_KRR, LANES) * sin_t
    kb = kv[:, :1024] + jnp.concatenate([kr] * B_HEADS, axis=1)
    store_slots(kb_o, seg_norm(kb, e128, B_QK, gkb_ref[...]), B_HEADS)
    store_value_slots(vb_o, kv[:, 1024:], B_HEADS // 2)

    store_slots(qc_o, seg_norm(proj(_QC, 512), e64, C_DIM, gqc_ref[...]), C_HEADS // 2)
    store_slots(kc_o, seg_norm(proj(_KC, 256), e64, C_DIM, gkc_ref[...]), C_KV_HEADS)
    store_value_slots(vc_o, proj(_VC, 256), C_KV_HEADS)


def _proj_call(x, cos_t, sin_t, p, tm):
    b, s, d = x.shape
    full = lambda a: pl.BlockSpec(a.shape, lambda i, j: (0,) * a.ndim, pipeline_mode=pl.Buffered(1))
    slot = lambda nw: pl.BlockSpec((1, nw[0], tm, nw[1]), lambda i, j: (i, 0, j, 0))
    tok = lambda w: pl.BlockSpec((1, tm, w), lambda i, j: (i, j, 0))
    consts = [p["gmix"], p["w_in"], p["w_uq"], p["w_ukv"], p["e64"], p["e128"],
              p["gqa"], p["gka"], p["gcq"], p["gckv"], p["gqb"], p["gkb"], p["gqc"], p["gkc"]]
    slots = [(A_HEADS, LANES), (A_HEADS, LANES), (A_HEADS, 2 * LANES), (B_HEADS, LANES), (B_HEADS, LANES),
             (B_HEADS // 2, 2 * LANES), (C_HEADS // 2, LANES), (C_KV_HEADS, LANES), (C_KV_HEADS, 2 * LANES)]
    return pl.pallas_call(
        _proj_kernel,
        grid=(b, s // tm),
        in_specs=[tok(d)] + [full(a) for a in consts] + [tok(LANES), tok(LANES)],
        out_specs=[slot(nw) for nw in slots],
        out_shape=[jax.ShapeDtypeStruct((b, nw[0], s, nw[1]), BF16) for nw in slots],
        compiler_params=_cparams(("parallel", "parallel")),
        name="proj",
    )(x, *consts, cos_t, sin_t)


def _attn_a_kernel(q_ref, k_ref, v_ref, pc_ref, pr_ref, slope_ref, lam_ref, og_ref, o_ref, s_scr, *, lam_init, ck):
    q = q_ref[0, 0]
    tq = q.shape[0]
    seq = k_ref.shape[2]
    lane = lax.broadcasted_iota(jnp.int32, q.shape, 1)
    zero = jnp.zeros_like(q)
    qm = [jnp.where(lane < A_DIM, q, zero), jnp.where(lane >= A_DIM, q, zero)]
    slope = slope_ref[0][:, :1]
    pos_q = pc_ref[0]
    mx = [jnp.full((tq, LANES), -jnp.inf, F32)] * 2
    for c in range(seq // ck):
        cols = slice(c * ck, (c + 1) * ck)
        k_c = k_ref[0, 0, cols, :]
        bias = slope * jnp.abs(pos_q - pr_ref[0, :, cols])
        for m in range(2):
            sc = lax.dot_general(qm[m], k_c, _NT, preferred_element_type=F32) - bias
            s_scr[m, :, cols] = sc
            for j in range(ck // LANES):
                mx[m] = jnp.maximum(mx[m], sc[:, j * LANES:(j + 1) * LANES])
    v = v_ref[0, 0]
    acc = []
    for m in range(2):
        e = jnp.exp2(s_scr[m] - jnp.max(mx[m], axis=-1, keepdims=True)).astype(BF16)
        acc.append(_dot(e, v))
    lp = lam_ref[...]
    lam = (jnp.exp(jnp.sum(lp[0:1] * lp[1:2], axis=-1, keepdims=True))
           - jnp.exp(jnp.sum(lp[2:3] * lp[3:4], axis=-1, keepdims=True)) + lam_init)
    o = (acc[0][:, :LANES] * (1.0 / acc[0][:, LANES:LANES + 1])
         - acc[1][:, :LANES] * (lam / acc[1][:, LANES:LANES + 1]))
    o_ref[0] = (_row_rms(o, og_ref[...]) * (1.0 - lam_init)).astype(o_ref.dtype)


def _attn_a_call(qa, ka, va, pos_c, pos_r, slopes, lam_p, out_g, lam_init, tq):
    b, h, s, _ = qa.shape
    kv_spec = lambda w: pl.BlockSpec((1, 1, s, w), lambda i, j, t: (i, j, 0, 0))
    return pl.pallas_call(
        functools.partial(_attn_a_kernel, lam_init=lam_init, ck=min(512, s)),
        grid=(b, h, s // tq),
        scratch_shapes=[pltpu.VMEM((2, tq, s), F32)],
        in_specs=[pl.BlockSpec((1, 1, tq, LANES), lambda i, j, t: (i, j, t, 0)), kv_spec(LANES), kv_spec(2 * LANES),
                  pl.BlockSpec((1, tq, 1), lambda i, j, t: (i, t, 0)),
                  pl.BlockSpec((1, 1, s), lambda i, j, t: (i, 0, 0)),
                  pl.BlockSpec((1, 1, LANES), lambda i, j, t: (j, 0, 0)),
                  pl.BlockSpec(lam_p.shape, lambda i, j, t: (0, 0)),
                  pl.BlockSpec(out_g.shape, lambda i, j, t: (0, 0))],
        out_specs=pl.BlockSpec((1, tq, LANES), lambda i, j, t: (i, t, j)),
        out_shape=jax.ShapeDtypeStruct((b, s, h * LANES), BF16),
        compiler_params=_cparams(("parallel", "parallel", "arbitrary")),
        name="attn_a",
    )(qa, ka, va, pos_c, pos_r, slopes, lam_p, out_g)


def _attn_b_kernel(q_ref, k_ref, v_ref, o_ref):
    tq = q_ref.shape[2]
    es = []
    for j in range(2):
        s = lax.dot_general(q_ref[0, j], k_ref[0, j], _NT, preferred_element_type=F32)
        es.append(jnp.exp2(s - jnp.max(s, axis=-1, keepdims=True)).astype(BF16))
    acc = _dot(jnp.concatenate(es, axis=0), v_ref[0, 0])
    lane = lax.broadcasted_iota(jnp.int32, (tq, LANES), 1)
    o_ref[0] = jnp.where(lane < B_VDIM, acc[:tq, :LANES] * (1.0 / acc[:tq, LANES:LANES + 1]),
                         acc[tq:, :LANES] * (1.0 / acc[tq:, LANES:LANES + 1])).astype(o_ref.dtype)


def _attn_b_call(qb, kb, vb, tq):
    b, h, s, _ = qb.shape
    return pl.pallas_call(
        _attn_b_kernel,
        grid=(b, h // 2, s // tq),
        in_specs=[pl.BlockSpec((1, 2, tq, LANES), lambda i, j, t: (i, j, t, 0)),
                  pl.BlockSpec((1, 2, s, LANES), lambda i, j, t: (i, j, 0, 0)),
                  pl.BlockSpec((1, 1, s, 2 * LANES), lambda i, j, t: (i, j, 0, 0))],
        out_specs=pl.BlockSpec((1, tq, LANES), lambda i, j, t: (i, t, j)),
        out_shape=jax.ShapeDtypeStruct((b, s, (h // 2) * LANES), BF16),
        compiler_params=_cparams(("parallel", "parallel", "arbitrary")),
        name="attn_b",
    )(qb, kb, vb)


def _attn_c_kernel(q_ref, kp_ref, ko_ref, kn_ref, vp_ref, vo_ref, vn_ref, slope_ref, sink_ref, o_ref, *, seq):
    w = WINDOW
    nsub = ko_ref.shape[2] // w
    n0 = pl.program_id(2) * nsub
    kcat = jnp.concatenate([kp_ref[0, 0], ko_ref[0, 0], kn_ref[0, 0]], axis=0)
    vcat = jnp.concatenate([vp_ref[0, 0], vo_ref[0, 0], vn_ref[0, 0]], axis=0)
    lane = lax.broadcasted_iota(jnp.int32, (w, LANES), 1)
    r_idx = lax.broadcasted_iota(jnp.int32, (w, 3 * w), 0)
    c_idx = lax.broadcasted_iota(jnp.int32, (w, 3 * w), 1)
    arel = jnp.abs(c_idx - w - r_idx)
    dist = arel.astype(F32)
    slopes = slope_ref[0]
    sinks = sink_ref[0]
    bias4 = jnp.concatenate([jnp.where(arel <= w, -slopes[r:r + 1, :1] * dist, -1e30) for r in range(C_REP)], axis=0)
    sink4 = jnp.concatenate([jnp.broadcast_to(sinks[r:r + 1, :1], (w, 1)) for r in range(C_REP)], axis=0)
    c_row = lax.broadcasted_iota(jnp.int32, (1, 3 * w), 1)
    scs = []
    for i in range(nsub):
        parts = []
        for p in range(2):
            q = q_ref[0, p, i * w:(i + 1) * w, :]
            zero = jnp.zeros_like(q)
            parts += [jnp.where(lane < C_DIM, q, zero), jnp.where(lane >= C_DIM, q, zero)]
        qz = jnp.concatenate(parts, axis=0)
        s = lax.dot_general(qz, kcat[i * w:(i + 3) * w], _NT, preferred_element_type=F32)
        kidx = (n0 + i - 1) * w + c_row
        edge = jnp.where(kidx >= 0, jnp.where(kidx < seq, 0.0, -1e30), -1e30)
        scs.append(s + bias4 + edge)
    sc = jnp.concatenate(scs, axis=0)
    sk = jnp.concatenate([sink4] * nsub, axis=0)
    m = jnp.maximum(jnp.max(sc, axis=-1, keepdims=True), sk)
    e = jnp.exp(sc - m).astype(BF16)
    tail = jnp.exp(sk - m)
    for i in range(nsub):
        rows = slice(i * 4 * w, (i + 1) * 4 * w)
        acc = _dot(e[rows], vcat[i * w:(i + 3) * w])
        o = acc[:, :LANES] * (1.0 / (acc[:, LANES:LANES + 1] + tail[rows]))
        pair0 = jnp.where(lane < C_DIM, o[0:w], o[w:2 * w])
        pair1 = jnp.where(lane < C_DIM, o[2 * w:3 * w], o[3 * w:4 * w])
        o_ref[0, i * w:(i + 1) * w, :] = jnp.concatenate([pair0, pair1], axis=1).astype(o_ref.dtype)


def _attn_c_call(qc, kc, vc, slopes, sinks):
    b, _, s, _ = qc.shape
    nb = s // WINDOW
    nsub = min(4, nb)
    tq = nsub * WINDOW
    prev = lambda wd: pl.BlockSpec((1, 1, WINDOW, wd), lambda i, g, n: (i, g, jnp.maximum(n * nsub - 1, 0), 0))
    own = lambda wd: pl.BlockSpec((1, 1, tq, wd), lambda i, g, n: (i, g, n, 0))
    nxt = lambda wd: pl.BlockSpec((1, 1, WINDOW, wd), lambda i, g, n: (i, g, jnp.minimum((n + 1) * nsub, nb - 1), 0))
    kw, vw = kc.shape[-1], vc.shape[-1]
    per_group = pl.BlockSpec((1, C_REP, LANES), lambda i, g, n: (g, 0, 0))
    return pl.pallas_call(
        functools.partial(_attn_c_kernel, seq=s),
        grid=(b, C_KV_HEADS, s // tq),
        in_specs=[pl.BlockSpec((1, 2, tq, LANES), lambda i, g, n: (i, g, n, 0)),
                  prev(kw), own(kw), nxt(kw), prev(vw), own(vw), nxt(vw), per_group, per_group],
        out_specs=pl.BlockSpec((1, tq, 2 * LANES), lambda i, g, n: (i, n, g)),
        out_shape=jax.ShapeDtypeStruct((b, s, C_HEADS * C_DIM), BF16),
        compiler_params=_cparams(("parallel", "parallel", "arbitrary")),
        name="attn_c",
    )(qc, kc, kc, kc, vc, vc, vc, slopes, sinks)


def _merge_kernel(x_ref, gmix_ref, wg_ref, oa_ref, ob_ref, oc_ref, wa_ref, wb_ref, wc_ref, wo_ref,
                  gffn_ref, wr_ref, xn_o, h2_o, aff_o):
    d = D_MODEL
    x = x_ref[0]
    hb = _row_rms(x, gmix_ref[...]).astype(BF16)
    g = 1.0 / (1.0 + jnp.exp(-_dot(hb, wg_ref[...])))
    merged = (g[:, :d] * _dot(oa_ref[0], wa_ref[...]) + g[:, d:2 * d] * _dot(ob_ref[0], wb_ref[...])
              + g[:, 2 * d:] * _dot(oc_ref[0], wc_ref[...]))
    xn = x + _dot(merged.astype(BF16), wo_ref[...])
    xn_o[0] = xn
    h2 = _row_rms(xn, gffn_ref[...])
    h2_o[0] = h2.astype(BF16)
    logits = lax.dot_general(wr_ref[...], h2, _NT, preferred_element_type=F32,
                             precision=lax.Precision.HIGHEST)
    ex = jnp.exp(logits - jnp.max(logits, axis=0, keepdims=True))
    aff_o[0] = ex / jnp.sum(ex, axis=0, keepdims=True)


def _merge_call(x, oa, ob, oc, p, tm):
    b, s, d = x.shape
    full = lambda a: pl.BlockSpec(a.shape, lambda i, j: (0,) * a.ndim, pipeline_mode=pl.Buffered(1))
    tok = lambda w: pl.BlockSpec((1, tm, w), lambda i, j: (i, j, 0))
    return pl.pallas_call(
        _merge_kernel,
        grid=(b, s // tm),
        in_specs=[tok(d), full(p["gmix"]), full(p["w_gate"]), tok(512), tok(512), tok(512),
                  full(p["w_a"]), full(p["w_b"]), full(p["w_c"]), full(p["w_o"]), full(p["gffn"]), full(p["w_rt"])],
        out_specs=[tok(d), tok(d), pl.BlockSpec((1, N_EXPERTS, tm), lambda i, j: (i, 0, j))],
        out_shape=[jax.ShapeDtypeStruct((b, s, d), F32), jax.ShapeDtypeStruct((b, s, d), BF16),
                   jax.ShapeDtypeStruct((b, N_EXPERTS, s), F32)],
        compiler_params=_cparams(("parallel", "parallel")),
        name="merge",
    )(x, p["gmix"], p["w_gate"], oa, ob, oc, p["w_a"], p["w_b"], p["w_c"], p["w_o"], p["gffn"], p["w_rt"])


def _cumsum_lanes(mask01, chunk):
    rows, s = mask01.shape
    tri = jnp.where(lax.broadcasted_iota(jnp.int32, (chunk, chunk), 0)
                    <= lax.broadcasted_iota(jnp.int32, (chunk, chunk), 1), 1.0, 0.0).astype(BF16)
    carry = jnp.zeros((rows, 1), F32)
    outs = []
    for c in range(s // chunk):
        cs = _dot(mask01[:, c * chunk:(c + 1) * chunk], tri) + carry
        outs.append(cs)
        carry = cs[:, chunk - 1:chunk]
    return jnp.concatenate(outs, axis=1)


def _topk_kernel(aff_ref, pos_o, *, cap, chunk):
    bits = pltpu.bitcast(aff_ref[0], jnp.int32)
    rows = bits.shape[0]
    capf = float(cap)

    def count(mask):
        return jnp.sum(jnp.where(mask, 1.0, 0.0), axis=1, keepdims=True)

    def body(_, c):
        lo, hi = c
        mid = lo + ((hi - lo + 1) >> 1)
        ok = count(bits >= mid) >= capf
        return jnp.where(ok, mid, lo), jnp.where(ok, hi, mid - 1)

    lo0 = jnp.zeros((rows, 1), jnp.int32)
    hi0 = jnp.full((rows, 1), 0x7F800000, jnp.int32)
    thr, _ = lax.fori_loop(0, 31, body, (lo0, hi0))
    gt = bits > thr
    eq = bits == thr
    need = capf - count(gt)
    eq_rank = _cumsum_lanes(jnp.where(eq, 1.0, 0.0).astype(BF16), chunk)
    sel = jnp.where(gt, 1.0, jnp.where(eq, jnp.where(eq_rank <= need, 1.0, 0.0), 0.0))
    slot = _cumsum_lanes(sel.astype(BF16), chunk) - 1.0
    pos_o[0] = jnp.where(sel > 0.0, slot, -1.0).astype(jnp.int32)


def _topk_call(aff_t, cap):
    b, e, s = aff_t.shape
    spec = pl.BlockSpec((1, e, s), lambda i: (i, 0, 0))
    return pl.pallas_call(
        functools.partial(_topk_kernel, cap=cap, chunk=min(512, s)),
        grid=(b,),
        in_specs=[spec],
        out_specs=spec,
        out_shape=jax.ShapeDtypeStruct((b, e, s), jnp.int32),
        compiler_params=_cparams(("parallel",)),
        name="topk",
    )(aff_t)


def _gather_kernel(pos_ref, aff_ref, h_ref, xin_o, asel_o, *, cap):
    pos = pos_ref[0, 0]
    hit = lax.broadcasted_iota(jnp.int32, (cap, pos.shape[1]), 0) == pos
    onehot = jnp.where(hit, 1.0, 0.0).astype(BF16)
    xin_o[0, 0] = _dot(onehot, h_ref[0]).astype(BF16)
    asel = jnp.sum(jnp.where(hit, aff_ref[0, 0], 0.0), axis=1, keepdims=True)
    asel_o[0, 0] = jnp.broadcast_to(asel, (cap, LANES))


def _gather_call(pos4, aff4, h2, cap):
    b, e, _, s = pos4.shape
    d = h2.shape[-1]
    row = pl.BlockSpec((1, 1, 1, s), lambda i, j: (i, j, 0, 0))
    return pl.pallas_call(
        functools.partial(_gather_kernel, cap=cap),
        grid=(b, e),
        in_specs=[row, row, pl.BlockSpec((1, s, d), lambda i, j: (i, 0, 0))],
        out_specs=[pl.BlockSpec((1, 1, cap, d), lambda i, j: (i, j, 0, 0)),
                   pl.BlockSpec((1, 1, cap, LANES), lambda i, j: (i, j, 0, 0))],
        out_shape=[jax.ShapeDtypeStruct((b, e, cap, d), BF16), jax.ShapeDtypeStruct((b, e, cap, LANES), F32)],
        compiler_params=_cparams(("parallel", "arbitrary")),
        name="gather",
    )(pos4, aff4, h2)


def _expert_kernel(xin_ref, asel_ref, wg_ref, wu_ref, wd_ref, y_o):
    xin = xin_ref[0, 0]
    g = _dot(xin, wg_ref[0])
    u = _dot(xin, wu_ref[0])
    hid = (g * (1.0 / (1.0 + jnp.exp(-g))) * u).astype(BF16)
    y_o[0, 0] = (_dot(hid, wd_ref[0]) * asel_ref[0, 0][:, :1]).astype(y_o.dtype)


def _expert_call(xin, asel, wg, wu, wd):
    b, e, cap, d = xin.shape
    f = wg.shape[-1]
    tokens = lambda w: pl.BlockSpec((1, 1, cap, w), lambda j, i: (i, j, 0, 0))
    return pl.pallas_call(
        _expert_kernel,
        grid=(e, b),
        in_specs=[tokens(d), tokens(LANES),
                  pl.BlockSpec((1, d, f), lambda j, i: (j, 0, 0)),
                  pl.BlockSpec((1, d, f), lambda j, i: (j, 0, 0)),
                  pl.BlockSpec((1, f, d), lambda j, i: (j, 0, 0))],
        out_specs=tokens(d),
        out_shape=jax.ShapeDtypeStruct((b, e, cap, d), BF16),
        compiler_params=_cparams(("parallel", "arbitrary")),
        name="expert",
    )(xin, asel, wg, wu, wd)


def _combine_kernel(x_ref, pos_ref, y_ref, o_ref, *, cap):
    @pl.when(pl.program_id(2) == 0)
    def _():
        o_ref[...] = x_ref[...]

    pos = pos_ref[0, 0]
    hit = lax.broadcasted_iota(jnp.int32, (cap, pos.shape[1]), 0) == pos
    onehot = jnp.where(hit, 1.0, 0.0).astype(BF16)
    o_ref[0] += lax.dot_general(onehot, y_ref[0, 0], _TN, preferred_element_type=F32)


def _combine_call(xn, pos4, y, tt):
    b, s, d = xn.shape
    e, cap = y.shape[1], y.shape[2]
    tok = pl.BlockSpec((1, tt, d), lambda i, t, j: (i, t, 0))
    return pl.pallas_call(
        functools.partial(_combine_kernel, cap=cap),
        grid=(b, s // tt, e),
        in_specs=[tok, pl.BlockSpec((1, 1, 1, tt), lambda i, t, j: (i, j, 0, t)),
                  pl.BlockSpec((1, 1, cap, d), lambda i, t, j: (i, j, 0, 0))],
        out_specs=tok,
        out_shape=jax.ShapeDtypeStruct((b, s, d), F32),
        compiler_params=_cparams(("parallel", "parallel", "arbitrary")),
        name="combine",
    )(xn, pos4, y)


def _block_diag(n, blk):
    i = np.arange(n)
    return jnp.asarray((i[:, None] // blk) == (i[None, :] // blk), dtype=BF16)


def _head_slots(w, heads, width):
    r = w.shape[0]
    return jnp.pad(w.reshape(r, heads, width), ((0, 0), (0, 0), (0, LANES - width))).reshape(r, heads * LANES)


def _rotate_half_cols(w):
    half = B_ROPE // 2
    return jnp.concatenate([-w[..., half:], w[..., :half]], axis=-1)


def _layer_params(l, w_in, norm_mix_g, diff_qk_g, mla_cq_g, w_uq, mla_ckv_g, w_ukv, mla_qk_g, swa_qk_g,
                  w_branch_a, w_branch_b, w_branch_c, w_o, norm_ffn_g, w_router):
    d = D_MODEL
    wi = w_in[l]
    off = np.cumsum([0, 512, 512, 512, B_Q_LORA, B_KV_LORA, B_ROPE, 512, 128, 128, 3 * d])
    piece = lambda k: wi[:, off[k]:off[k + 1]]
    maps_to_heads = lambda w: w.reshape(d, 2, A_HEADS, A_DIM).transpose(0, 2, 1, 3).reshape(d, 512)
    dup = lambda w: jnp.concatenate([w.reshape(d, C_KV_HEADS, 1, C_DIM)] * 2, axis=2).reshape(d, 256)
    kr = piece(5)
    rope_slot = lambda w: jnp.pad(w, ((0, 0), (B_NOPE, LANES - B_QK)))
    w_proj = jnp.concatenate(
        [maps_to_heads(piece(0)), maps_to_heads(piece(1)), piece(2), piece(3), piece(4), piece(6),
         dup(piece(7)), dup(piece(8)), rope_slot(kr), rope_slot(_rotate_half_cols(kr))], axis=1).astype(BF16)
    assert w_proj.shape == (d, _PROJ_COLS)

    wq = w_uq[l].reshape(B_Q_LORA, B_HEADS, B_QK)
    wq_rot = jnp.concatenate([jnp.zeros_like(wq[..., :B_NOPE]), _rotate_half_cols(wq[..., B_NOPE:])], axis=-1)
    w_uq_x = jnp.concatenate([_head_slots(wq.reshape(B_Q_LORA, -1), B_HEADS, B_QK),
                              _head_slots(wq_rot.reshape(B_Q_LORA, -1), B_HEADS, B_QK)], axis=1).astype(BF16)
    wkv = w_ukv[l].reshape(B_KV_LORA, B_HEADS, B_NOPE + B_VDIM)
    w_ukv_x = jnp.concatenate([_head_slots(wkv[..., :B_NOPE].reshape(B_KV_LORA, -1), B_HEADS, B_NOPE),
                               wkv[..., B_NOPE:].reshape(B_KV_LORA, -1)], axis=1).astype(BF16)

    row = lambda v: v.reshape(1, -1).astype(F32)
    slot_gain = lambda g: jnp.tile(jnp.pad(g, (0, LANES - B_QK)), B_HEADS)
    return {
        "gmix": row(norm_mix_g[l]), "w_in": w_proj, "w_uq": w_uq_x, "w_ukv": w_ukv_x,
        "e64": _block_diag(SEG_TILE, 64), "e128": _block_diag(SEG_TILE, LANES),
        "gqa": row(jnp.tile(diff_qk_g[l, 0], 8) * (A_DIM ** -0.5 * LOG2E)), "gka": row(jnp.tile(diff_qk_g[l, 1], 8)),
        "gcq": row(mla_cq_g[l]), "gckv": row(mla_ckv_g[l]),
        "gqb": row(slot_gain(mla_qk_g[l, 0]) * (B_QK ** -0.5 * LOG2E)), "gkb": row(slot_gain(mla_qk_g[l, 1])),
        "gqc": row(jnp.tile(swa_qk_g[l, 0], 8) * (C_DIM ** -0.5)), "gkc": row(jnp.tile(swa_qk_g[l, 1], 4)),
        "w_gate": piece(9).astype(BF16),
        "w_a": w_branch_a[l].astype(BF16), "w_b": w_branch_b[l].astype(BF16), "w_c": w_branch_c[l].astype(BF16),
        "w_o": w_o[l].astype(BF16), "gffn": row(norm_ffn_g[l]), "w_rt": w_router[l].T.astype(F32),
    }


def _rope_slot_tables(positions):
    inv = 1.0 / (ROPE_THETA ** (jnp.arange(0, B_ROPE, 2, dtype=F32) / B_ROPE))
    ang = positions.astype(F32)[..., None] * inv
    cos, sin = jnp.cos(ang), jnp.sin(ang)
    ones = jnp.ones(ang.shape[:-1] + (B_NOPE,), F32)
    pad = jnp.zeros(ang.shape[:-1] + (LANES - B_QK,), F32)
    return (jnp.concatenate([ones, cos, cos, pad], axis=-1),
            jnp.concatenate([jnp.zeros_like(ones), sin, sin, pad], axis=-1))


def _alibi_slopes(n):
    return 2.0 ** (-8.0 * jnp.arange(1, n + 1, dtype=F32) / n)


def kernel(x, positions, norm_mix_g, w_in, diff_qk_g, diff_lambda, diff_out_g, mla_cq_g, w_uq, mla_ckv_g, w_ukv,
           mla_qk_g, swa_qk_g, swa_sink, w_branch_a, w_branch_b, w_branch_c, w_o, norm_ffn_g, w_router,
           w_exp_gate, w_exp_up, w_exp_down):
    b, s, d = x.shape
    depth = w_in.shape[0]
    cap = max(1, EC_CAPACITY * s // N_EXPERTS)
    tm_proj = min(512, s)
    tq = min(256, s)
    tq_a = min(512, s)
    tm_merge = min(512, s)
    tt = min(2048, s)

    cos_t, sin_t = _rope_slot_tables(positions)
    pos_f = positions.astype(F32)
    pos_c, pos_r = pos_f[:, :, None], pos_f[:, None, :]
    lane_bcast = lambda v: jnp.broadcast_to(v[..., None], v.shape + (LANES,)).astype(F32)
    slopes_a = lane_bcast(_alibi_slopes(A_HEADS)[:, None] * LOG2E)
    slopes_c = lane_bcast(_alibi_slopes(C_HEADS).reshape(C_KV_HEADS, C_REP))

    for l in range(depth):
        p = _layer_params(l, w_in, norm_mix_g, diff_qk_g, mla_cq_g, w_uq, mla_ckv_g, w_ukv, mla_qk_g, swa_qk_g,
                          w_branch_a, w_branch_b, w_branch_c, w_o, norm_ffn_g, w_router)
        qa, ka, va, qb, kb, vb, qc, kc, vc = _proj_call(x, cos_t, sin_t, p, tm_proj)
        lam_init = 0.8 - 0.6 * math.exp(-0.3 * l)
        oa = _attn_a_call(qa, ka, va, pos_c, pos_r, slopes_a, diff_lambda[l].astype(F32),
                          diff_out_g[l].reshape(1, -1).astype(F32), lam_init, tq_a)
        ob = _attn_b_call(qb, kb, vb, min(512, s))
        oc = _attn_c_call(qc, kc, vc, slopes_c, lane_bcast(swa_sink[l].reshape(C_KV_HEADS, C_REP)))
        xn, h2, aff_t = _merge_call(x, oa, ob, oc, p, tm_merge)
        pos4 = _topk_call(aff_t, cap).reshape(b, N_EXPERTS, 1, s)
        xin, asel = _gather_call(pos4, aff_t.reshape(b, N_EXPERTS, 1, s), h2, cap)
        y = _expert_call(xin, asel, w_exp_gate[l].astype(BF16), w_exp_up[l].astype(BF16),
                         w_exp_down[l].astype(BF16))
        x = _combine_call(xn, pos4, y, tt)
    return x
```

```python
import functools
import math

import numpy as np
import jax
import jax.numpy as jnp
from jax import lax
from jax.experimental import pallas as pl
from jax.experimental.pallas import tpu as pltpu

F32 = jnp.float32
BF16 = jnp.bfloat16

D_MODEL = 1024
EPS = 1e-6
A_HEADS = 4
A_DIM = 64
A_VDIM = 128
B_HEADS = 8
B_NOPE = 64
B_ROPE = 32
B_VDIM = 64
B_QK = B_NOPE + B_ROPE
B_Q_LORA = 384
B_KV_LORA = 256
ROPE_THETA = 10000.0
C_HEADS = 8
C_KV_HEADS = 2
C_REP = C_HEADS // C_KV_HEADS
C_DIM = 64
WINDOW = 128
N_EXPERTS = 16
EC_CAPACITY = 2
D_FF = 1024
LOG2E = math.log2(math.e)
SEG_TILE = 256
LANES = 128

_QA = 0
_KA = 512
_VA = 1024
_CQ = 1536
_CKV = 1920
_QC = 2176
_KC = 2688
_VC = 2944
_KR = 3200
_KRR = 3328
_PROJ_COLS = 3456

VMEM_LIMIT = 56 * 1024 * 1024

_NT = (((1,), (1,)), ((), ()))
_TN = (((0,), (0,)), ((), ()))


def _cparams(sem):
    return pltpu.CompilerParams(dimension_semantics=sem, vmem_limit_bytes=VMEM_LIMIT)


def _dot(a, b):
    return jnp.dot(a, b, preferred_element_type=F32)


def _seg_sum(x2, e):
    hi = x2.astype(BF16)
    lo = (x2 - hi.astype(F32)).astype(BF16)
    slabs = [_dot(hi[:, c:c + SEG_TILE], e) + _dot(lo[:, c:c + SEG_TILE], e)
             for c in range(0, x2.shape[1], SEG_TILE)]
    return slabs[0] if len(slabs) == 1 else jnp.concatenate(slabs, axis=1)


def _row_rms(x, g):
    return x * lax.rsqrt(jnp.mean(x * x, axis=-1, keepdims=True) + EPS) * g


def _proj_kernel(x_ref, gmix_ref, w_ref, wuq_ref, wukv_ref, e64_ref, e128_ref,
                 gqa_ref, gka_ref, gcq_ref, gckv_ref, gqb_ref, gkb_ref, gqc_ref, gkc_ref,
                 cos_ref, sin_ref,
                 qa_o, ka_o, va_o, qb_o, kb_o, vb_o, qc_o, kc_o, vc_o):
    hb = _row_rms(x_ref[0], gmix_ref[...]).astype(BF16)

    def proj(a, n):
        return _dot(hb, w_ref[:, a:a + n])

    e64 = e64_ref[...]
    e128 = e128_ref[...]

    def seg_norm(v, e, width, g):
        return v * lax.rsqrt(_seg_sum(v * v, e) * (1.0 / width) + EPS) * g

    def store_slots(o_ref, v, n):
        for j in range(n):
            o_ref[0, j] = v[:, LANES * j:LANES * (j + 1)].astype(o_ref.dtype)

    ones_slot = jnp.where(lax.broadcasted_iota(jnp.int32, (hb.shape[0], LANES), 1) == 0, 1.0, 0.0)

    def store_value_slots(o_ref, v, n):
        for j in range(n):
            o_ref[0, j] = jnp.concatenate([v[:, LANES * j:LANES * (j + 1)], ones_slot], axis=1).astype(o_ref.dtype)

    store_slots(qa_o, seg_norm(proj(_QA, 512), e64, A_DIM, gqa_ref[...]), A_HEADS)
    store_slots(ka_o, seg_norm(proj(_KA, 512), e64, A_DIM, gka_ref[...]), A_HEADS)
    store_value_slots(va_o, proj(_VA, 512), A_HEADS)

    cos_t = cos_ref[0]
    sin_t = sin_ref[0]
    cos8 = jnp.concatenate([cos_t] * B_HEADS, axis=1)
    sin8 = jnp.concatenate([sin_t] * B_HEADS, axis=1)
    cq = _row_rms(proj(_CQ, B_Q_LORA), gcq_ref[...]).astype(BF16)
    q2 = _dot(cq, wuq_ref[...])
    qb = q2[:, :1024] * cos8 + q2[:, 1024:] * sin8
    store_slots(qb_o, seg_norm(qb, e128, B_QK, gqb_ref[...]), B_HEADS)
    ckv = _row_rms(proj(_CKV, B_KV_LORA), gckv_ref[...]).astype(BF16)
    kv = _dot(ckv, wukv_ref[...])
    kr = proj(_KR, LANES) * cos_t + proj(_KRR, LANES) * sin_t
    kb = kv[:, :1024] + jnp.concatenate([kr] * B_HEADS, axis=1)
    store_slots(kb_o, seg_norm(kb, e128, B_QK, gkb_ref[...]), B_HEADS)
    store_value_slots(vb_o, kv[:, 1024:], B_HEADS // 2)

    store_slots(qc_o, seg_norm(proj(_QC, 512), e64, C_DIM, gqc_ref[...]), C_HEADS // 2)
    store_slots(kc_o, seg_norm(proj(_KC, 256), e64, C_DIM, gkc_ref[...]), C_KV_HEADS)
    store_value_slots(vc_o, proj(_VC, 256), C_KV_HEADS)


def _proj_call(x, cos_t, sin_t, p, tm):
    b, s, d = x.shape
    full = lambda a: pl.BlockSpec(a.shape, lambda i, j: (0,) * a.ndim, pipeline_mode=pl.Buffered(1))
    slot = lambda nw: pl.BlockSpec((1, nw[0], tm, nw[1]), lambda i, j: (i, 0, j, 0))
    tok = lambda w: pl.BlockSpec((1, tm, w), lambda i, j: (i, j, 0))
    consts = [p["gmix"], p["w_in"], p["w_uq"], p["w_ukv"], p["e64"], p["e128"],
              p["gqa"], p["gka"], p["gcq"], p["gckv"], p["gqb"], p["gkb"], p["gqc"], p["gkc"]]
    slots = [(A_HEADS, LANES), (A_HEADS, LANES), (A_HEADS, 2 * LANES), (B_HEADS, LANES), (B_HEADS, LANES),
             (B_HEADS // 2, 2 * LANES), (C_HEADS // 2, LANES), (C_KV_HEADS, LANES), (C_KV_HEADS, 2 * LANES)]
    return pl.pallas_call(
        _proj_kernel,
        grid=(b, s // tm),
        in_specs=[tok(d)] + [full(a) for a in consts] + [tok(LANES), tok(LANES)],
        out_specs=[slot(nw) for nw in slots],
        out_shape=[jax.ShapeDtypeStruct((b, nw[0], s, nw[1]), BF16) for nw in slots],
        compiler_params=_cparams(("parallel", "parallel")),
        name="proj",
    )(x, *consts, cos_t, sin_t)


def _attn_a_kernel(q_ref, k_ref, v_ref, pc_ref, pr_ref, slope_ref, lam_ref, og_ref, o_ref, s_scr, *, lam_init, mono):
    t = pl.program_id(2)
    q = q_ref[0, 0]
    tq = q.shape[0]
    ck = tq
    nck = k_ref.shape[2] // ck
    lane = lax.broadcasted_iota(jnp.int32, q.shape, 1)
    zero = jnp.zeros_like(q)
    qm = [jnp.where(lane < A_DIM, q, zero), jnp.where(lane >= A_DIM, q, zero)]
    slope = slope_ref[0][:, :1]
    a = slope * pc_ref[0]
    a_lanes = jnp.broadcast_to(a, (tq, LANES))

    def lane_fold_max(x):
        out = x[:, :LANES]
        for j in range(1, ck // LANES):
            out = jnp.maximum(out, x[:, j * LANES:(j + 1) * LANES])
        return out

    mx = [jnp.full((tq, LANES), -jnp.inf, F32)] * 2
    for d in range(nck):
        if mono:
            c = t if d == 0 else lax.rem(t + d, nck)
            k_c = k_ref[0, 0, pl.ds(pl.multiple_of(c * ck, ck), ck), :]
        else:
            c = d
            k_c = k_ref[0, 0, d * ck:(d + 1) * ck, :]
        b = slope * pr_ref[0, c]
        if mono and d > 0:
            sign = jnp.where(t + d < nck, -1.0, 1.0)
            row_part, col_part = sign * b, sign * a_lanes
            for m in range(2):
                sc = lax.dot_general(qm[m], k_c, _NT, preferred_element_type=F32) + row_part
                s_scr[m, c] = sc
                mx[m] = jnp.maximum(mx[m], lane_fold_max(sc) - col_part)
        else:
            bias = jnp.abs(a - b)
            for m in range(2):
                sc = lax.dot_general(qm[m], k_c, _NT, preferred_element_type=F32) - bias
                s_scr[m, c] = sc
                mx[m] = jnp.maximum(mx[m], lane_fold_max(sc))
    v = v_ref[0, 0]
    acc = []
    for m in range(2):
        row_max = jnp.broadcast_to(jnp.max(mx[m], axis=-1, keepdims=True), (tq, LANES))
        es = []
        for c in range(nck):
            if mono:
                sign = jnp.where(c < t, 1.0, jnp.where(c > t, -1.0, 0.0))
                stab = row_max + sign * a_lanes
            else:
                stab = row_max
            es.append(jnp.exp2(s_scr[m, c] - jnp.concatenate([stab] * (ck // LANES), axis=1)).astype(BF16))
        e = es[0] if nck == 1 else jnp.concatenate(es, axis=1)
        acc.append(_dot(e, v))
    lp = lam_ref[...]
    lam = (jnp.exp(jnp.sum(lp[0:1] * lp[1:2], axis=-1, keepdims=True))
           - jnp.exp(jnp.sum(lp[2:3] * lp[3:4], axis=-1, keepdims=True)) + lam_init)
    o = (acc[0][:, :LANES] * (1.0 / acc[0][:, LANES:LANES + 1])
         - acc[1][:, :LANES] * (lam / acc[1][:, LANES:LANES + 1]))
    o_ref[0] = (_row_rms(o, og_ref[...]) * (1.0 - lam_init)).astype(o_ref.dtype)


def _attn_a_call(qa, ka, va, pos_c, pos_r, slopes, lam_p, out_g, lam_init, tq, mono):
    b, h, s, _ = qa.shape
    kv_spec = lambda w: pl.BlockSpec((1, 1, s, w), lambda i, j, t: (i, j, 0, 0))
    return pl.pallas_call(
        functools.partial(_attn_a_kernel, lam_init=lam_init, mono=mono),
        grid=(b, h, s // tq),
        scratch_shapes=[pltpu.VMEM((2, s // tq, tq, tq), F32)],
        in_specs=[pl.BlockSpec((1, 1, tq, LANES), lambda i, j, t: (i, j, t, 0)), kv_spec(LANES), kv_spec(2 * LANES),
                  pl.BlockSpec((1, tq, 1), lambda i, j, t: (i, t, 0)),
                  pl.BlockSpec((1, s // tq, 1, tq), lambda i, j, t: (i, 0, 0, 0)),
                  pl.BlockSpec((1, 1, LANES), lambda i, j, t: (j, 0, 0)),
                  pl.BlockSpec(lam_p.shape, lambda i, j, t: (0, 0)),
                  pl.BlockSpec(out_g.shape, lambda i, j, t: (0, 0))],
        out_specs=pl.BlockSpec((1, tq, LANES), lambda i, j, t: (i, t, j)),
        out_shape=jax.ShapeDtypeStruct((b, s, h * LANES), BF16),
        compiler_params=_cparams(("parallel", "parallel", "arbitrary")),
        name="attn_a",
    )(qa, ka, va, pos_c, pos_r, slopes, lam_p, out_g)


def _attn_b_kernel(q_ref, k_ref, v_ref, o_ref):
    tq = q_ref.shape[2]
    es = []
    for j in range(2):
        s = lax.dot_general(q_ref[0, j], k_ref[0, j], _NT, preferred_element_type=F32)
        es.append(jnp.exp2(s - jnp.max(s, axis=-1, keepdims=True)).astype(BF16))
    acc = _dot(jnp.concatenate(es, axis=0), v_ref[0, 0])
    lane = lax.broadcasted_iota(jnp.int32, (tq, LANES), 1)
    o_ref[0] = jnp.where(lane < B_VDIM, acc[:tq, :LANES] * (1.0 / acc[:tq, LANES:LANES + 1]),
                         acc[tq:, :LANES] * (1.0 / acc[tq:, LANES:LANES + 1])).astype(o_ref.dtype)


def _attn_b_call(qb, kb, vb, tq):
    b, h, s, _ = qb.shape
    return pl.pallas_call(
        _attn_b_kernel,
        grid=(b, h // 2, s // tq),
        in_specs=[pl.BlockSpec((1, 2, tq, LANES), lambda i, j, t: (i, j, t, 0)),
                  pl.BlockSpec((1, 2, s, LANES), lambda i, j, t: (i, j, 0, 0)),
                  pl.BlockSpec((1, 1, s, 2 * LANES), lambda i, j, t: (i, j, 0, 0))],
        out_specs=pl.BlockSpec((1, tq, LANES), lambda i, j, t: (i, t, j)),
        out_shape=jax.ShapeDtypeStruct((b, s, (h // 2) * LANES), BF16),
        compiler_params=_cparams(("parallel", "parallel", "arbitrary")),
        name="attn_b",
    )(qb, kb, vb)


def _attn_c_kernel(q_ref, kp_ref, ko_ref, kn_ref, vp_ref, vo_ref, vn_ref, slope_ref, sink_ref, o_ref, *, seq):
    w = WINDOW
    nsub = ko_ref.shape[2] // w
    n0 = pl.program_id(2) * nsub
    kcat = jnp.concatenate([kp_ref[0, 0], ko_ref[0, 0], kn_ref[0, 0]], axis=0)
    vcat = jnp.concatenate([vp_ref[0, 0], vo_ref[0, 0], vn_ref[0, 0]], axis=0)
    lane = lax.broadcasted_iota(jnp.int32, (w, LANES), 1)
    r_idx = lax.broadcasted_iota(jnp.int32, (w, 3 * w), 0)
    c_idx = lax.broadcasted_iota(jnp.int32, (w, 3 * w), 1)
    arel = jnp.abs(c_idx - w - r_idx)
    dist = arel.astype(F32)
    slopes = slope_ref[0]
    sinks = sink_ref[0]
    bias4 = jnp.concatenate([jnp.where(arel <= w, -slopes[r:r + 1, :1] * dist, -1e30) for r in range(C_REP)], axis=0)
    sink4 = jnp.concatenate([jnp.broadcast_to(sinks[r:r + 1, :1], (w, 1)) for r in range(C_REP)], axis=0)
    c_row = lax.broadcasted_iota(jnp.int32, (1, 3 * w), 1)
    scs = []
    for i in range(nsub):
        parts = []
        for p in range(2):
            q = q_ref[0, p, i * w:(i + 1) * w, :]
            zero = jnp.zeros_like(q)
            parts += [jnp.where(lane < C_DIM, q, zero), jnp.where(lane >= C_DIM, q, zero)]
        qz = jnp.concatenate(parts, axis=0)
        s = lax.dot_general(qz, kcat[i * w:(i + 3) * w], _NT, preferred_element_type=F32)
        kidx = (n0 + i - 1) * w + c_row
        edge = jnp.where(kidx >= 0, jnp.where(kidx < seq, 0.0, -1e30), -1e30)
        scs.append(s + bias4 + edge)
    sc = jnp.concatenate(scs, axis=0)
    sk = jnp.concatenate([sink4] * nsub, axis=0)
    m = jnp.maximum(jnp.max(sc, axis=-1, keepdims=True), sk)
    e = jnp.exp(sc - m).astype(BF16)
    tail = jnp.exp(sk - m)
    for i in range(nsub):
        rows = slice(i * 4 * w, (i + 1) * 4 * w)
        acc = _dot(e[rows], vcat[i * w:(i + 3) * w])
        o = acc[:, :LANES] * (1.0 / (acc[:, LANES:LANES + 1] + tail[rows]))
        pair0 = jnp.where(lane < C_DIM, o[0:w], o[w:2 * w])
        pair1 = jnp.where(lane < C_DIM, o[2 * w:3 * w], o[3 * w:4 * w])
        o_ref[0, i * w:(i + 1) * w, :] = jnp.concatenate([pair0, pair1], axis=1).astype(o_ref.dtype)


def _attn_c_call(qc, kc, vc, slopes, sinks):
    b, _, s, _ = qc.shape
    nb = s // WINDOW
    nsub = min(4, nb)
    tq = nsub * WINDOW
    prev = lambda wd: pl.BlockSpec((1, 1, WINDOW, wd), lambda i, g, n: (i, g, jnp.maximum(n * nsub - 1, 0), 0))
    own = lambda wd: pl.BlockSpec((1, 1, tq, wd), lambda i, g, n: (i, g, n, 0))
    nxt = lambda wd: pl.BlockSpec((1, 1, WINDOW, wd), lambda i, g, n: (i, g, jnp.minimum((n + 1) * nsub, nb - 1), 0))
    kw, vw = kc.shape[-1], vc.shape[-1]
    per_group = pl.BlockSpec((1, C_REP, LANES), lambda i, g, n: (g, 0, 0))
    return pl.pallas_call(
        functools.partial(_attn_c_kernel, seq=s),
        grid=(b, C_KV_HEADS, s // tq),
        in_specs=[pl.BlockSpec((1, 2, tq, LANES), lambda i, g, n: (i, g, n, 0)),
                  prev(kw), own(kw), nxt(kw), prev(vw), own(vw), nxt(vw), per_group, per_group],
        out_specs=pl.BlockSpec((1, tq, 2 * LANES), lambda i, g, n: (i, n, g)),
        out_shape=jax.ShapeDtypeStruct((b, s, C_HEADS * C_DIM), BF16),
        compiler_params=_cparams(("parallel", "parallel", "arbitrary")),
        name="attn_c",
    )(qc, kc, kc, kc, vc, vc, vc, slopes, sinks)


def _merge_kernel(x_ref, gmix_ref, wg_ref, oa_ref, ob_ref, oc_ref, wa_ref, wb_ref, wc_ref, wo_ref,
                  gffn_ref, wr_ref, xn_o, h2_o, aff_o):
    d = D_MODEL
    x = x_ref[0]
    hb = _row_rms(x, gmix_ref[...]).astype(BF16)
    g = 1.0 / (1.0 + jnp.exp(-_dot(hb, wg_ref[...])))
    merged = (g[:, :d] * _dot(oa_ref[0], wa_ref[...]) + g[:, d:2 * d] * _dot(ob_ref[0], wb_ref[...])
              + g[:, 2 * d:] * _dot(oc_ref[0], wc_ref[...]))
    xn = x + _dot(merged.astype(BF16), wo_ref[...])
    xn_o[0] = xn
    h2 = _row_rms(xn, gffn_ref[...])
    h2_o[0] = h2.astype(BF16)
    logits = lax.dot_general(wr_ref[...], h2, _NT, preferred_element_type=F32,
                             precision=lax.Precision.HIGHEST)
    ex = jnp.exp(logits - jnp.max(logits, axis=0, keepdims=True))
    aff_o[0] = ex / jnp.sum(ex, axis=0, keepdims=True)


def _merge_call(x, oa, ob, oc, p, tm):
    b, s, d = x.shape
    full = lambda a: pl.BlockSpec(a.shape, lambda i, j: (0,) * a.ndim, pipeline_mode=pl.Buffered(1))
    tok = lambda w: pl.BlockSpec((1, tm, w), lambda i, j: (i, j, 0))
    return pl.pallas_call(
        _merge_kernel,
        grid=(b, s // tm),
        in_specs=[tok(d), full(p["gmix"]), full(p["w_gate"]), tok(512), tok(512), tok(512),
                  full(p["w_a"]), full(p["w_b"]), full(p["w_c"]), full(p["w_o"]), full(p["gffn"]), full(p["w_rt"])],
        out_specs=[tok(d), tok(d), pl.BlockSpec((1, N_EXPERTS, tm), lambda i, j: (i, 0, j))],
        out_shape=[jax.ShapeDtypeStruct((b, s, d), F32), jax.ShapeDtypeStruct((b, s, d), BF16),
                   jax.ShapeDtypeStruct((b, N_EXPERTS, s), F32)],
        compiler_params=_cparams(("parallel", "parallel")),
        name="merge",
    )(x, p["gmix"], p["w_gate"], oa, ob, oc, p["w_a"], p["w_b"], p["w_c"], p["w_o"], p["gffn"], p["w_rt"])


def _cumsum_lanes(mask01, chunk):
    rows, s = mask01.shape
    tri = jnp.where(lax.broadcasted_iota(jnp.int32, (chunk, chunk), 0)
                    <= lax.broadcasted_iota(jnp.int32, (chunk, chunk), 1), 1.0, 0.0).astype(BF16)
    carry = jnp.zeros((rows, 1), F32)
    outs = []
    for c in range(s // chunk):
        cs = _dot(mask01[:, c * chunk:(c + 1) * chunk], tri) + carry
        outs.append(cs)
        carry = cs[:, chunk - 1:chunk]
    return jnp.concatenate(outs, axis=1)


def _topk_kernel(aff_ref, pos_o, *, cap, chunk):
    bits = pltpu.bitcast(aff_ref[0], jnp.int32)
    rows = bits.shape[0]
    capf = float(cap)

    def count(mask):
        return jnp.sum(jnp.where(mask, 1.0, 0.0), axis=1, keepdims=True)

    def body(_, c):
        lo, hi = c
        mid = lo + ((hi - lo + 1) >> 1)
        ok = count(bits >= mid) >= capf
        return jnp.where(ok, mid, lo), jnp.where(ok, hi, mid - 1)

    lo0 = jnp.zeros((rows, 1), jnp.int32)
    hi0 = jnp.full((rows, 1), 0x7F800000, jnp.int32)
    thr, _ = lax.fori_loop(0, 31, body, (lo0, hi0))
    gt = bits > thr
    eq = bits == thr
    need = capf - count(gt)
    eq_rank = _cumsum_lanes(jnp.where(eq, 1.0, 0.0).astype(BF16), chunk)
    sel = jnp.where(gt, 1.0, jnp.where(eq, jnp.where(eq_rank <= need, 1.0, 0.0), 0.0))
    slot = _cumsum_lanes(sel.astype(BF16), chunk) - 1.0
    pos_o[0] = jnp.where(sel > 0.0, slot, -1.0).astype(jnp.int32)


def _topk_call(aff_t, cap):
    b, e, s = aff_t.shape
    spec = pl.BlockSpec((1, e, s), lambda i: (i, 0, 0))
    return pl.pallas_call(
        functools.partial(_topk_kernel, cap=cap, chunk=min(512, s)),
        grid=(b,),
        in_specs=[spec],
        out_specs=spec,
        out_shape=jax.ShapeDtypeStruct((b, e, s), jnp.int32),
        compiler_params=_cparams(("parallel",)),
        name="topk",
    )(aff_t)


def _gather_kernel(pos_ref, aff_ref, h_ref, xin_o, asel_o, *, cap):
    pos = pos_ref[0, 0]
    hit = lax.broadcasted_iota(jnp.int32, (cap, pos.shape[1]), 0) == pos
    onehot = jnp.where(hit, 1.0, 0.0).astype(BF16)
    xin_o[0, 0] = _dot(onehot, h_ref[0]).astype(BF16)
    asel = jnp.sum(jnp.where(hit, aff_ref[0, 0], 0.0), axis=1, keepdims=True)
    asel_o[0, 0] = jnp.broadcast_to(asel, (cap, LANES))


def _gather_call(pos4, aff4, h2, cap):
    b, e, _, s = pos4.shape
    d = h2.shape[-1]
    row = pl.BlockSpec((1, 1, 1, s), lambda i, j: (i, j, 0, 0))
    return pl.pallas_call(
        functools.partial(_gather_kernel, cap=cap),
        grid=(b, e),
        in_specs=[row, row, pl.BlockSpec((1, s, d), lambda i, j: (i, 0, 0))],
        out_specs=[pl.BlockSpec((1, 1, cap, d), lambda i, j: (i, j, 0, 0)),
                   pl.BlockSpec((1, 1, cap, LANES), lambda i, j: (i, j, 0, 0))],
        out_shape=[jax.ShapeDtypeStruct((b, e, cap, d), BF16), jax.ShapeDtypeStruct((b, e, cap, LANES), F32)],
        compiler_params=_cparams(("parallel", "arbitrary")),
        name="gather",
    )(pos4, aff4, h2)


def _expert_kernel(xin_ref, asel_ref, wg_ref, wu_ref, wd_ref, y_o):
    xin = xin_ref[0, 0]
    g = _dot(xin, wg_ref[0])
    u = _dot(xin, wu_ref[0])
    hid = (g * (1.0 / (1.0 + jnp.exp(-g))) * u).astype(BF16)
    y_o[0, 0] = (_dot(hid, wd_ref[0]) * asel_ref[0, 0][:, :1]).astype(y_o.dtype)


def _expert_call(xin, asel, wg, wu, wd):
    b, e, cap, d = xin.shape
    f = wg.shape[-1]
    tokens = lambda w: pl.BlockSpec((1, 1, cap, w), lambda j, i: (i, j, 0, 0))
    return pl.pallas_call(
        _expert_kernel,
        grid=(e, b),
        in_specs=[tokens(d), tokens(LANES),
                  pl.BlockSpec((1, d, f), lambda j, i: (j, 0, 0)),
                  pl.BlockSpec((1, d, f), lambda j, i: (j, 0, 0)),
                  pl.BlockSpec((1, f, d), lambda j, i: (j, 0, 0))],
        out_specs=tokens(d),
        out_shape=jax.ShapeDtypeStruct((b, e, cap, d), BF16),
        compiler_params=_cparams(("parallel", "arbitrary")),
        name="expert",
    )(xin, asel, wg, wu, wd)


def _combine_kernel(x_ref, pos_ref, y_ref, o_ref, *, cap):
    @pl.when(pl.program_id(2) == 0)
    def _():
        o_ref[...] = x_ref[...]

    pos = pos_ref[0, 0]
    hit = lax.broadcasted_iota(jnp.int32, (cap, pos.shape[1]), 0) == pos
    onehot = jnp.where(hit, 1.0, 0.0).astype(BF16)
    o_ref[0] += lax.dot_general(onehot, y_ref[0, 0], _TN, preferred_element_type=F32)


def _combine_call(xn, pos4, y, tt):
    b, s, d = xn.shape
    e, cap = y.shape[1], y.shape[2]
    tok = pl.BlockSpec((1, tt, d), lambda i, t, j: (i, t, 0))
    return pl.pallas_call(
        functools.partial(_combine_kernel, cap=cap),
        grid=(b, s // tt, e),
        in_specs=[tok, pl.BlockSpec((1, 1, 1, tt), lambda i, t, j: (i, j, 0, t)),
                  pl.BlockSpec((1, 1, cap, d), lambda i, t, j: (i, j, 0, 0))],
        out_specs=tok,
        out_shape=jax.ShapeDtypeStruct((b, s, d), F32),
        compiler_params=_cparams(("parallel", "parallel", "arbitrary")),
        name="combine",
    )(xn, pos4, y)


def _block_diag(n, blk):
    i = np.arange(n)
    return jnp.asarray((i[:, None] // blk) == (i[None, :] // blk), dtype=BF16)


def _head_slots(w, heads, width):
    r = w.shape[0]
    return jnp.pad(w.reshape(r, heads, width), ((0, 0), (0, 0), (0, LANES - width))).reshape(r, heads * LANES)


def _rotate_half_cols(w):
    half = B_ROPE // 2
    return jnp.concatenate([-w[..., half:], w[..., :half]], axis=-1)


def _layer_params(l, w_in, norm_mix_g, diff_qk_g, mla_cq_g, w_uq, mla_ckv_g, w_ukv, mla_qk_g, swa_qk_g,
                  w_branch_a, w_branch_b, w_branch_c, w_o, norm_ffn_g, w_router):
    d = D_MODEL
    wi = w_in[l]
    off = np.cumsum([0, 512, 512, 512, B_Q_LORA, B_KV_LORA, B_ROPE, 512, 128, 128, 3 * d])
    piece = lambda k: wi[:, off[k]:off[k + 1]]
    maps_to_heads = lambda w: w.reshape(d, 2, A_HEADS, A_DIM).transpose(0, 2, 1, 3).reshape(d, 512)
    dup = lambda w: jnp.concatenate([w.reshape(d, C_KV_HEADS, 1, C_DIM)] * 2, axis=2).reshape(d, 256)
    kr = piece(5)
    rope_slot = lambda w: jnp.pad(w, ((0, 0), (B_NOPE, LANES - B_QK)))
    w_proj = jnp.concatenate(
        [maps_to_heads(piece(0)), maps_to_heads(piece(1)), piece(2), piece(3), piece(4), piece(6),
         dup(piece(7)), dup(piece(8)), rope_slot(kr), rope_slot(_rotate_half_cols(kr))], axis=1).astype(BF16)
    assert w_proj.shape == (d, _PROJ_COLS)

    wq = w_uq[l].reshape(B_Q_LORA, B_HEADS, B_QK)
    wq_rot = jnp.concatenate([jnp.zeros_like(wq[..., :B_NOPE]), _rotate_half_cols(wq[..., B_NOPE:])], axis=-1)
    w_uq_x = jnp.concatenate([_head_slots(wq.reshape(B_Q_LORA, -1), B_HEADS, B_QK),
                              _head_slots(wq_rot.reshape(B_Q_LORA, -1), B_HEADS, B_QK)], axis=1).astype(BF16)
    wkv = w_ukv[l].reshape(B_KV_LORA, B_HEADS, B_NOPE + B_VDIM)
    w_ukv_x = jnp.concatenate([_head_slots(wkv[..., :B_NOPE].reshape(B_KV_LORA, -1), B_HEADS, B_NOPE),
                               wkv[..., B_NOPE:].reshape(B_KV_LORA, -1)], axis=1).astype(BF16)

    row = lambda v: v.reshape(1, -1).astype(F32)
    slot_gain = lambda g: jnp.tile(jnp.pad(g, (0, LANES - B_QK)), B_HEADS)
    return {
        "gmix": row(norm_mix_g[l]), "w_in": w_proj, "w_uq": w_uq_x, "w_ukv": w_ukv_x,
        "e64": _block_diag(SEG_TILE, 64), "e128": _block_diag(SEG_TILE, LANES),
        "gqa": row(jnp.tile(diff_qk_g[l, 0], 8) * (A_DIM ** -0.5 * LOG2E)), "gka": row(jnp.tile(diff_qk_g[l, 1], 8)),
        "gcq": row(mla_cq_g[l]), "gckv": row(mla_ckv_g[l]),
        "gqb": row(slot_gain(mla_qk_g[l, 0]) * (B_QK ** -0.5 * LOG2E)), "gkb": row(slot_gain(mla_qk_g[l, 1])),
        "gqc": row(jnp.tile(swa_qk_g[l, 0], 8) * (C_DIM ** -0.5)), "gkc": row(jnp.tile(swa_qk_g[l, 1], 4)),
        "w_gate": piece(9).astype(BF16),
        "w_a": w_branch_a[l].astype(BF16), "w_b": w_branch_b[l].astype(BF16), "w_c": w_branch_c[l].astype(BF16),
        "w_o": w_o[l].astype(BF16), "gffn": row(norm_ffn_g[l]), "w_rt": w_router[l].T.astype(F32),
    }


def _rope_slot_tables(positions):
    inv = 1.0 / (ROPE_THETA ** (jnp.arange(0, B_ROPE, 2, dtype=F32) / B_ROPE))
    ang = positions.astype(F32)[..., None] * inv
    cos, sin = jnp.cos(ang), jnp.sin(ang)
    ones = jnp.ones(ang.shape[:-1] + (B_NOPE,), F32)
    pad = jnp.zeros(ang.shape[:-1] + (LANES - B_QK,), F32)
    return (jnp.concatenate([ones, cos, cos, pad], axis=-1),
            jnp.concatenate([jnp.zeros_like(ones), sin, sin, pad], axis=-1))


def _alibi_slopes(n):
    return 2.0 ** (-8.0 * jnp.arange(1, n + 1, dtype=F32) / n)


def kernel(x, positions, norm_mix_g, w_in, diff_qk_g, diff_lambda, diff_out_g, mla_cq_g, w_uq, mla_ckv_g, w_ukv,
           mla_qk_g, swa_qk_g, swa_sink, w_branch_a, w_branch_b, w_branch_c, w_o, norm_ffn_g, w_router,
           w_exp_gate, w_exp_up, w_exp_down):
    b, s, d = x.shape
    depth = w_in.shape[0]
    cap = max(1, EC_CAPACITY * s // N_EXPERTS)
    tm_proj = min(512, s)
    tq = min(256, s)
    tq_a = min(512, s)
    tm_merge = min(512, s)
    tt = min(2048, s)

    cos_t, sin_t = _rope_slot_tables(positions)
    pos_f = positions.astype(F32)
    pos_c, pos_r = pos_f[:, :, None], pos_f.reshape(b, s // tq_a, 1, tq_a)
    monotone = jnp.all(positions[:, 1:] >= positions[:, :-1])
    lane_bcast = lambda v: jnp.broadcast_to(v[..., None], v.shape + (LANES,)).astype(F32)
    slopes_a = lane_bcast(_alibi_slopes(A_HEADS)[:, None] * LOG2E)
    slopes_c = lane_bcast(_alibi_slopes(C_HEADS).reshape(C_KV_HEADS, C_REP))

    for l in range(depth):
        p = _layer_params(l, w_in, norm_mix_g, diff_qk_g, mla_cq_g, w_uq, mla_ckv_g, w_ukv, mla_qk_g, swa_qk_g,
                          w_branch_a, w_branch_b, w_branch_c, w_o, norm_ffn_g, w_router)
        qa, ka, va, qb, kb, vb, qc, kc, vc = _proj_call(x, cos_t, sin_t, p, tm_proj)
        lam_init = 0.8 - 0.6 * math.exp(-0.3 * l)
        attn_a = functools.partial(_attn_a_call, qa, ka, va, pos_c, pos_r, slopes_a, diff_lambda[l].astype(F32),
                                   diff_out_g[l].reshape(1, -1).astype(F32), lam_init, tq_a)
        oa = lax.cond(monotone, functools.partial(attn_a, True), functools.partial(attn_a, False))
        ob = _attn_b_call(qb, kb, vb, min(512, s))
        oc = _attn_c_call(qc, kc, vc, slopes_c, lane_bcast(swa_sink[l].reshape(C_KV_HEADS, C_REP)))
        xn, h2, aff_t = _merge_call(x, oa, ob, oc, p, tm_merge)
        pos4 = _topk_call(aff_t, cap).reshape(b, N_EXPERTS, 1, s)
        xin, asel = _gather_call(pos4, aff_t.reshape(b, N_EXPERTS, 1, s), h2, cap)
        y = _expert_call(xin, asel, w_exp_gate[l].astype(BF16), w_exp_up[l].astype(BF16),
                         w_exp_down[l].astype(BF16))
        x = _combine_call(xn, pos4, y, tt)
    return x
```

```python
import functools
import math

import numpy as np
import jax
import jax.numpy as jnp
from jax import lax
from jax.experimental import pallas as pl
from jax.experimental.pallas import tpu as pltpu

F32 = jnp.float32
BF16 = jnp.bfloat16

D_MODEL = 1024
EPS = 1e-6
A_HEADS = 4
A_DIM = 64
A_VDIM = 128
B_HEADS = 8
B_NOPE = 64
B_ROPE = 32
B_VDIM = 64
B_QK = B_NOPE + B_ROPE
B_Q_LORA = 384
B_KV_LORA = 256
ROPE_THETA = 10000.0
C_HEADS = 8
C_KV_HEADS = 2
C_REP = C_HEADS // C_KV_HEADS
C_DIM = 64
WINDOW = 128
N_EXPERTS = 16
EC_CAPACITY = 2
D_FF = 1024
LOG2E = math.log2(math.e)
SEG_TILE = 256
LANES = 128

_QA = 0
_KA = 512
_VA = 1024
_CQ = 1536
_CKV = 1920
_QC = 2176
_KC = 2688
_VC = 2944
_KR = 3200
_KRR = 3328
_PROJ_COLS = 3456

VMEM_LIMIT = 56 * 1024 * 1024

_NT = (((1,), (1,)), ((), ()))
_TN = (((0,), (0,)), ((), ()))


def _cparams(sem):
    return pltpu.CompilerParams(dimension_semantics=sem, vmem_limit_bytes=VMEM_LIMIT)


def _dot(a, b):
    return jnp.dot(a, b, preferred_element_type=F32)


def _seg_sum(x2, e):
    hi = x2.astype(BF16)
    lo = (x2 - hi.astype(F32)).astype(BF16)
    e2 = jnp.concatenate([e, e], axis=0)
    slabs = [_dot(jnp.concatenate([hi[:, c:c + SEG_TILE], lo[:, c:c + SEG_TILE]], axis=1), e2)
             for c in range(0, x2.shape[1], SEG_TILE)]
    return slabs[0] if len(slabs) == 1 else jnp.concatenate(slabs, axis=1)


def _sigmoid(x):
    return 0.5 * jnp.tanh(0.5 * x) + 0.5


def _row_rms(x, g):
    return x * lax.rsqrt(jnp.mean(x * x, axis=-1, keepdims=True) + EPS) * g


def _proj_kernel(x_ref, gmix_ref, w_ref, wuq_ref, wukv_ref, e64_ref, e128_ref,
                 gqa_ref, gka_ref, gcq_ref, gckv_ref, gqb_ref, gkb_ref, gqc_ref, gkc_ref,
                 cos_ref, sin_ref,
                 qa_o, ka_o, va_o, qb_o, kb_o, vb_o, qc_o, kc_o, vc_o):
    hb = _row_rms(x_ref[0], gmix_ref[...]).astype(BF16)

    projected = _dot(hb, w_ref[...])

    def proj(a, n):
        return projected[:, a:a + n]

    e64 = e64_ref[...]
    e128 = e128_ref[...]

    def seg_norm(v, e, width, g):
        return v * lax.rsqrt(_seg_sum(v * v, e) * (1.0 / width) + EPS) * g

    def store_slots(o_ref, v, n):
        for j in range(n):
            o_ref[0, j] = v[:, LANES * j:LANES * (j + 1)].astype(o_ref.dtype)

    ones_slot = jnp.where(lax.broadcasted_iota(jnp.int32, (hb.shape[0], LANES), 1) == 0, 1.0, 0.0)

    def store_value_slots(o_ref, v, n):
        for j in range(n):
            o_ref[0, j] = jnp.concatenate([v[:, LANES * j:LANES * (j + 1)], ones_slot], axis=1).astype(o_ref.dtype)

    store_slots(qa_o, seg_norm(proj(_QA, 512), e64, A_DIM, gqa_ref[...]), A_HEADS)
    store_slots(ka_o, seg_norm(proj(_KA, 512), e64, A_DIM, gka_ref[...]), A_HEADS)
    store_value_slots(va_o, proj(_VA, 512), A_HEADS)

    cos_t = cos_ref[0]
    sin_t = sin_ref[0]
    cos8 = jnp.concatenate([cos_t] * B_HEADS, axis=1)
    sin8 = jnp.concatenate([sin_t] * B_HEADS, axis=1)
    cq = _row_rms(proj(_CQ, B_Q_LORA), gcq_ref[...]).astype(BF16)
    q2 = _dot(cq, wuq_ref[...])
    qb = q2[:, :1024] * cos8 + q2[:, 1024:] * sin8
    store_slots(qb_o, seg_norm(qb, e128, B_QK, gqb_ref[...]), B_HEADS)
    ckv = _row_rms(proj(_CKV, B_KV_LORA), gckv_ref[...]).astype(BF16)
    kv = _dot(ckv, wukv_ref[...])
    kr = proj(_KR, LANES) * cos_t + proj(_KRR, LANES) * sin_t
    kb = kv[:, :1024] + jnp.concatenate([kr] * B_HEADS, axis=1)
    store_slots(kb_o, seg_norm(kb, e128, B_QK, gkb_ref[...]), B_HEADS)
    store_value_slots(vb_o, kv[:, 1024:], B_HEADS // 2)

    store_slots(qc_o, seg_norm(proj(_QC, 512), e64, C_DIM, gqc_ref[...]), C_HEADS // 2)
    store_slots(kc_o, seg_norm(proj(_KC, 256), e64, C_DIM, gkc_ref[...]), C_KV_HEADS)
    store_value_slots(vc_o, proj(_VC, 256), C_KV_HEADS)


def _proj_call(x, cos_t, sin_t, p, tm):
    b, s, d = x.shape
    full = lambda a: pl.BlockSpec(a.shape, lambda i, j: (0,) * a.ndim, pipeline_mode=pl.Buffered(1))
    slot = lambda nw: pl.BlockSpec((1, nw[0], tm, nw[1]), lambda i, j: (i, 0, j, 0))
    tok = lambda w: pl.BlockSpec((1, tm, w), lambda i, j: (i, j, 0))
    consts = [p["gmix"], p["w_in"], p["w_uq"], p["w_ukv"], p["e64"], p["e128"],
              p["gqa"], p["gka"], p["gcq"], p["gckv"], p["gqb"], p["gkb"], p["gqc"], p["gkc"]]
    slots = [(A_HEADS, LANES), (A_HEADS, LANES), (A_HEADS, 2 * LANES), (B_HEADS, LANES), (B_HEADS, LANES),
             (B_HEADS // 2, 2 * LANES), (C_HEADS // 2, LANES), (C_KV_HEADS, LANES), (C_KV_HEADS, 2 * LANES)]
    return pl.pallas_call(
        _proj_kernel,
        grid=(b, s // tm),
        in_specs=[tok(d)] + [full(a) for a in consts] + [tok(LANES), tok(LANES)],
        out_specs=[slot(nw) for nw in slots],
        out_shape=[jax.ShapeDtypeStruct((b, nw[0], s, nw[1]), BF16) for nw in slots],
        compiler_params=_cparams(("parallel", "parallel")),
        name="proj",
    )(x, *consts, cos_t, sin_t)


def _attn_a_kernel(q_ref, k_ref, v_ref, pc_ref, pr_ref, slope_ref, lam_ref, og_ref, o_ref, s_scr, *, lam_init, mono):
    t = pl.program_id(2)
    q = q_ref[0, 0]
    tq = q.shape[0]
    ck = tq
    nck = k_ref.shape[2] // ck
    lane = lax.broadcasted_iota(jnp.int32, q.shape, 1)
    zero = jnp.zeros_like(q)
    qm = [jnp.where(lane < A_DIM, q, zero), jnp.where(lane >= A_DIM, q, zero)]
    slope = slope_ref[0][:, :1]
    a = slope * pc_ref[0]
    a_lanes = jnp.broadcast_to(a, (tq, LANES))

    def lane_fold_max(x):
        out = x[:, :LANES]
        for j in range(1, ck // LANES):
            out = jnp.maximum(out, x[:, j * LANES:(j + 1) * LANES])
        return out

    mx = [jnp.full((tq, LANES), -jnp.inf, F32)] * 2
    for d in range(nck):
        if mono:
            c = t if d == 0 else lax.rem(t + d, nck)
            k_c = k_ref[0, 0, pl.ds(pl.multiple_of(c * ck, ck), ck), :]
        else:
            c = d
            k_c = k_ref[0, 0, d * ck:(d + 1) * ck, :]
        b = slope * pr_ref[0, c]
        if mono and d > 0:
            sign = jnp.where(t + d < nck, -1.0, 1.0)
            row_part, col_part = sign * b, sign * a_lanes
            for m in range(2):
                sc = lax.dot_general(qm[m], k_c, _NT, preferred_element_type=F32) + row_part
                s_scr[m, c] = sc
                mx[m] = jnp.maximum(mx[m], lane_fold_max(sc) - col_part)
        else:
            bias = jnp.abs(a - b)
            for m in range(2):
                sc = lax.dot_general(qm[m], k_c, _NT, preferred_element_type=F32) - bias
                s_scr[m, c] = sc
                mx[m] = jnp.maximum(mx[m], lane_fold_max(sc))
    v = v_ref[0, 0]
    acc = []
    for m in range(2):
        row_max = jnp.broadcast_to(jnp.max(mx[m], axis=-1, keepdims=True), (tq, LANES))
        es = []
        for c in range(nck):
            if mono:
                sign = jnp.where(c < t, 1.0, jnp.where(c > t, -1.0, 0.0))
                stab = row_max + sign * a_lanes
            else:
                stab = row_max
            es.append(jnp.exp2(s_scr[m, c] - jnp.concatenate([stab] * (ck // LANES), axis=1)).astype(BF16))
        e = es[0] if nck == 1 else jnp.concatenate(es, axis=1)
        acc.append(_dot(e, v))
    lp = lam_ref[...]
    lam = (jnp.exp(jnp.sum(lp[0:1] * lp[1:2], axis=-1, keepdims=True))
           - jnp.exp(jnp.sum(lp[2:3] * lp[3:4], axis=-1, keepdims=True)) + lam_init)
    o = (acc[0][:, :LANES] * (1.0 / acc[0][:, LANES:LANES + 1])
         - acc[1][:, :LANES] * (lam / acc[1][:, LANES:LANES + 1]))
    o_ref[0] = (_row_rms(o, og_ref[...]) * (1.0 - lam_init)).astype(o_ref.dtype)


def _attn_a_call(qa, ka, va, pos_c, pos_r, slopes, lam_p, out_g, lam_init, tq, mono):
    b, h, s, _ = qa.shape
    kv_spec = lambda w: pl.BlockSpec((1, 1, s, w), lambda i, j, t: (i, j, 0, 0))
    return pl.pallas_call(
        functools.partial(_attn_a_kernel, lam_init=lam_init, mono=mono),
        grid=(b, h, s // tq),
        scratch_shapes=[pltpu.VMEM((2, s // tq, tq, tq), F32)],
        in_specs=[pl.BlockSpec((1, 1, tq, LANES), lambda i, j, t: (i, j, t, 0)), kv_spec(LANES), kv_spec(2 * LANES),
                  pl.BlockSpec((1, tq, 1), lambda i, j, t: (i, t, 0)),
                  pl.BlockSpec((1, s // tq, 1, tq), lambda i, j, t: (i, 0, 0, 0)),
                  pl.BlockSpec((1, 1, LANES), lambda i, j, t: (j, 0, 0)),
                  pl.BlockSpec(lam_p.shape, lambda i, j, t: (0, 0)),
                  pl.BlockSpec(out_g.shape, lambda i, j, t: (0, 0))],
        out_specs=pl.BlockSpec((1, tq, LANES), lambda i, j, t: (i, t, j)),
        out_shape=jax.ShapeDtypeStruct((b, s, h * LANES), BF16),
        compiler_params=_cparams(("parallel", "parallel", "arbitrary")),
        name="attn_a",
    )(qa, ka, va, pos_c, pos_r, slopes, lam_p, out_g)


def _attn_b_kernel(q_ref, k_ref, v_ref, o_ref):
    tq = q_ref.shape[2]
    es = []
    for j in range(2):
        s = lax.dot_general(q_ref[0, j], k_ref[0, j], _NT, preferred_element_type=F32)
        es.append(jnp.exp2(s - jnp.max(s, axis=-1, keepdims=True)).astype(BF16))
    acc = _dot(jnp.concatenate(es, axis=0), v_ref[0, 0])
    lane = lax.broadcasted_iota(jnp.int32, (tq, LANES), 1)
    o_ref[0] = jnp.where(lane < B_VDIM, acc[:tq, :LANES] * (1.0 / acc[:tq, LANES:LANES + 1]),
                         acc[tq:, :LANES] * (1.0 / acc[tq:, LANES:LANES + 1])).astype(o_ref.dtype)


def _attn_b_call(qb, kb, vb, tq):
    b, h, s, _ = qb.shape
    return pl.pallas_call(
        _attn_b_kernel,
        grid=(b, h // 2, s // tq),
        in_specs=[pl.BlockSpec((1, 2, tq, LANES), lambda i, j, t: (i, j, t, 0)),
                  pl.BlockSpec((1, 2, s, LANES), lambda i, j, t: (i, j, 0, 0)),
                  pl.BlockSpec((1, 1, s, 2 * LANES), lambda i, j, t: (i, j, 0, 0))],
        out_specs=pl.BlockSpec((1, tq, LANES), lambda i, j, t: (i, t, j)),
        out_shape=jax.ShapeDtypeStruct((b, s, (h // 2) * LANES), BF16),
        compiler_params=_cparams(("parallel", "parallel", "arbitrary")),
        name="attn_b",
    )(qb, kb, vb)


def _attn_c_kernel(q_ref, kp_ref, ko_ref, kn_ref, vp_ref, vo_ref, vn_ref, slope_ref, sink_ref, o_ref, *, seq):
    w = WINDOW
    nsub = ko_ref.shape[2] // w
    n0 = pl.program_id(2) * nsub
    kcat = jnp.concatenate([kp_ref[0, 0], ko_ref[0, 0], kn_ref[0, 0]], axis=0)
    vcat = jnp.concatenate([vp_ref[0, 0], vo_ref[0, 0], vn_ref[0, 0]], axis=0)
    lane = lax.broadcasted_iota(jnp.int32, (w, LANES), 1)
    r_idx = lax.broadcasted_iota(jnp.int32, (w, 3 * w), 0)
    c_idx = lax.broadcasted_iota(jnp.int32, (w, 3 * w), 1)
    arel = jnp.abs(c_idx - w - r_idx)
    dist = arel.astype(F32)
    slopes = slope_ref[0]
    sinks = sink_ref[0]
    bias4 = jnp.concatenate([jnp.where(arel <= w, -slopes[r:r + 1, :1] * dist, -1e30) for r in range(C_REP)], axis=0)
    sink4 = jnp.concatenate([jnp.broadcast_to(sinks[r:r + 1, :1], (w, 1)) for r in range(C_REP)], axis=0)
    c_row = lax.broadcasted_iota(jnp.int32, (1, 3 * w), 1)
    scs = []
    for i in range(nsub):
        parts = []
        for p in range(2):
            q = q_ref[0, p, i * w:(i + 1) * w, :]
            zero = jnp.zeros_like(q)
            parts += [jnp.where(lane < C_DIM, q, zero), jnp.where(lane >= C_DIM, q, zero)]
        qz = jnp.concatenate(parts, axis=0)
        s = lax.dot_general(qz, kcat[i * w:(i + 3) * w], _NT, preferred_element_type=F32)
        kidx = (n0 + i - 1) * w + c_row
        edge = jnp.where(kidx >= 0, jnp.where(kidx < seq, 0.0, -1e30), -1e30)
        scs.append(s + bias4 + edge)
    sc = jnp.concatenate(scs, axis=0)
    sk = jnp.concatenate([sink4] * nsub, axis=0)
    m = jnp.maximum(jnp.max(sc, axis=-1, keepdims=True), sk)
    e = jnp.exp(sc - m).astype(BF16)
    tail = jnp.exp(sk - m)
    for i in range(nsub):
        rows = slice(i * 4 * w, (i + 1) * 4 * w)
        acc = _dot(e[rows], vcat[i * w:(i + 3) * w])
        o = acc[:, :LANES] * (1.0 / (acc[:, LANES:LANES + 1] + tail[rows]))
        pair0 = jnp.where(lane < C_DIM, o[0:w], o[w:2 * w])
        pair1 = jnp.where(lane < C_DIM, o[2 * w:3 * w], o[3 * w:4 * w])
        o_ref[0, i * w:(i + 1) * w, :] = jnp.concatenate([pair0, pair1], axis=1).astype(o_ref.dtype)


def _attn_c_call(qc, kc, vc, slopes, sinks):
    b, _, s, _ = qc.shape
    nb = s // WINDOW
    nsub = min(4, nb)
    tq = nsub * WINDOW
    prev = lambda wd: pl.BlockSpec((1, 1, WINDOW, wd), lambda i, g, n: (i, g, jnp.maximum(n * nsub - 1, 0), 0))
    own = lambda wd: pl.BlockSpec((1, 1, tq, wd), lambda i, g, n: (i, g, n, 0))
    nxt = lambda wd: pl.BlockSpec((1, 1, WINDOW, wd), lambda i, g, n: (i, g, jnp.minimum((n + 1) * nsub, nb - 1), 0))
    kw, vw = kc.shape[-1], vc.shape[-1]
    per_group = pl.BlockSpec((1, C_REP, LANES), lambda i, g, n: (g, 0, 0))
    return pl.pallas_call(
        functools.partial(_attn_c_kernel, seq=s),
        grid=(b, C_KV_HEADS, s // tq),
        in_specs=[pl.BlockSpec((1, 2, tq, LANES), lambda i, g, n: (i, g, n, 0)),
                  prev(kw), own(kw), nxt(kw), prev(vw), own(vw), nxt(vw), per_group, per_group],
        out_specs=pl.BlockSpec((1, tq, 2 * LANES), lambda i, g, n: (i, n, g)),
        out_shape=jax.ShapeDtypeStruct((b, s, C_HEADS * C_DIM), BF16),
        compiler_params=_cparams(("parallel", "parallel", "arbitrary")),
        name="attn_c",
    )(qc, kc, kc, kc, vc, vc, vc, slopes, sinks)


def _merge_kernel(x_ref, gmix_ref, wg_ref, oa_ref, ob_ref, oc_ref, wa_ref, wb_ref, wc_ref, wo_ref,
                  gffn_ref, wr_ref, xn_o, h2_o, aff_o):
    d = D_MODEL
    x = x_ref[0]
    hb = _row_rms(x, gmix_ref[...]).astype(BF16)
    g = _sigmoid(_dot(hb, wg_ref[...]))
    merged = (g[:, :d] * _dot(oa_ref[0], wa_ref[...]) + g[:, d:2 * d] * _dot(ob_ref[0], wb_ref[...])
              + g[:, 2 * d:] * _dot(oc_ref[0], wc_ref[...]))
    xn = x + _dot(merged.astype(BF16), wo_ref[...])
    xn_o[0] = xn
    h2 = _row_rms(xn, gffn_ref[...])
    h2_o[0] = h2.astype(BF16)
    logits = lax.dot_general(wr_ref[...], h2, _NT, preferred_element_type=F32,
                             precision=lax.Precision.HIGHEST)
    ex = jnp.exp(logits - jnp.max(logits, axis=0, keepdims=True))
    aff_o[0] = ex / jnp.sum(ex, axis=0, keepdims=True)


def _merge_call(x, oa, ob, oc, p, tm):
    b, s, d = x.shape
    full = lambda a: pl.BlockSpec(a.shape, lambda i, j: (0,) * a.ndim, pipeline_mode=pl.Buffered(1))
    tok = lambda w: pl.BlockSpec((1, tm, w), lambda i, j: (i, j, 0))
    return pl.pallas_call(
        _merge_kernel,
        grid=(b, s // tm),
        in_specs=[tok(d), full(p["gmix"]), full(p["w_gate"]), tok(512), tok(512), tok(512),
                  full(p["w_a"]), full(p["w_b"]), full(p["w_c"]), full(p["w_o"]), full(p["gffn"]), full(p["w_rt"])],
        out_specs=[tok(d), tok(d), pl.BlockSpec((1, N_EXPERTS, tm), lambda i, j: (i, 0, j))],
        out_shape=[jax.ShapeDtypeStruct((b, s, d), F32), jax.ShapeDtypeStruct((b, s, d), BF16),
                   jax.ShapeDtypeStruct((b, N_EXPERTS, s), F32)],
        compiler_params=_cparams(("parallel", "parallel")),
        name="merge",
    )(x, p["gmix"], p["w_gate"], oa, ob, oc, p["w_a"], p["w_b"], p["w_c"], p["w_o"], p["gffn"], p["w_rt"])


def _cumsum_lanes(mask01, chunk):
    rows, s = mask01.shape
    tri = jnp.where(lax.broadcasted_iota(jnp.int32, (chunk, chunk), 0)
                    <= lax.broadcasted_iota(jnp.int32, (chunk, chunk), 1), 1.0, 0.0).astype(BF16)
    carry = jnp.zeros((rows, 1), F32)
    outs = []
    for c in range(s // chunk):
        cs = _dot(mask01[:, c * chunk:(c + 1) * chunk], tri) + carry
        outs.append(cs)
        carry = cs[:, chunk - 1:chunk]
    return jnp.concatenate(outs, axis=1)


def _topk_kernel(aff_ref, pos_o, *, cap, chunk):
    bits = pltpu.bitcast(aff_ref[0], jnp.int32)
    rows = bits.shape[0]
    capf = float(cap)

    def count(mask):
        return jnp.sum(jnp.where(mask, 1.0, 0.0), axis=1, keepdims=True)

    def body(_, c):
        lo, hi = c
        mid = lo + ((hi - lo + 1) >> 1)
        ok = count(bits >= mid) >= capf
        return jnp.where(ok, mid, lo), jnp.where(ok, hi, mid - 1)

    lo0 = jnp.zeros((rows, 1), jnp.int32)
    hi0 = jnp.full((rows, 1), 0x7F800000, jnp.int32)
    thr, _ = lax.fori_loop(0, 31, body, (lo0, hi0))
    gt = bits > thr
    eq = bits == thr
    need = capf - count(gt)
    eq_rank = _cumsum_lanes(jnp.where(eq, 1.0, 0.0).astype(BF16), chunk)
    sel = jnp.where(gt, 1.0, jnp.where(eq, jnp.where(eq_rank <= need, 1.0, 0.0), 0.0))
    slot = _cumsum_lanes(sel.astype(BF16), chunk) - 1.0
    pos_o[0] = jnp.where(sel > 0.0, slot, -1.0).astype(jnp.int32)


def _topk_call(aff_t, cap):
    b, e, s = aff_t.shape
    spec = pl.BlockSpec((1, e, s), lambda i: (i, 0, 0))
    return pl.pallas_call(
        functools.partial(_topk_kernel, cap=cap, chunk=min(512, s)),
        grid=(b,),
        in_specs=[spec],
        out_specs=spec,
        out_shape=jax.ShapeDtypeStruct((b, e, s), jnp.int32),
        compiler_params=_cparams(("parallel",)),
        name="topk",
    )(aff_t)


def _gather_kernel(pos_ref, aff_ref, h_ref, xin_o, asel_o, *, cap):
    pos = pos_ref[0, 0]
    hit = lax.broadcasted_iota(jnp.int32, (cap, pos.shape[1]), 0) == pos
    onehot = jnp.where(hit, 1.0, 0.0).astype(BF16)
    xin_o[0, 0] = _dot(onehot, h_ref[0]).astype(BF16)
    asel = jnp.sum(jnp.where(hit, aff_ref[0, 0], 0.0), axis=1, keepdims=True)
    asel_o[0, 0] = jnp.broadcast_to(asel, (cap, LANES))


def _gather_call(pos4, aff4, h2, cap):
    b, e, _, s = pos4.shape
    d = h2.shape[-1]
    row = pl.BlockSpec((1, 1, 1, s), lambda i, j: (i, j, 0, 0))
    return pl.pallas_call(
        functools.partial(_gather_kernel, cap=cap),
        grid=(b, e),
        in_specs=[row, row, pl.BlockSpec((1, s, d), lambda i, j: (i, 0, 0))],
        out_specs=[pl.BlockSpec((1, 1, cap, d), lambda i, j: (i, j, 0, 0)),
                   pl.BlockSpec((1, 1, cap, LANES), lambda i, j: (i, j, 0, 0))],
        out_shape=[jax.ShapeDtypeStruct((b, e, cap, d), BF16), jax.ShapeDtypeStruct((b, e, cap, LANES), F32)],
        compiler_params=_cparams(("parallel", "arbitrary")),
        name="gather",
    )(pos4, aff4, h2)


def _expert_kernel(xin_ref, asel_ref, wg_ref, wu_ref, wd_ref, y_o):
    xin = xin_ref[0, 0]
    g = _dot(xin, wg_ref[0])
    u = _dot(xin, wu_ref[0])
    hid = (g * _sigmoid(g) * u).astype(BF16)
    y_o[0, 0] = (_dot(hid, wd_ref[0]) * asel_ref[0, 0][:, :1]).astype(y_o.dtype)


def _expert_call(xin, asel, wg, wu, wd):
    b, e, cap, d = xin.shape
    f = wg.shape[-1]
    tokens = lambda w: pl.BlockSpec((1, 1, cap, w), lambda j, i: (i, j, 0, 0))
    return pl.pallas_call(
        _expert_kernel,
        grid=(e, b),
        in_specs=[tokens(d), tokens(LANES),
                  pl.BlockSpec((1, d, f), lambda j, i: (j, 0, 0)),
                  pl.BlockSpec((1, d, f), lambda j, i: (j, 0, 0)),
                  pl.BlockSpec((1, f, d), lambda j, i: (j, 0, 0))],
        out_specs=tokens(d),
        out_shape=jax.ShapeDtypeStruct((b, e, cap, d), BF16),
        compiler_params=_cparams(("parallel", "arbitrary")),
        name="expert",
    )(xin, asel, wg, wu, wd)


def _combine_kernel(x_ref, pos_ref, y_ref, o_ref, *, cap):
    @pl.when(pl.program_id(2) == 0)
    def _():
        o_ref[...] = x_ref[...]

    pos = pos_ref[0, 0]
    hit = lax.broadcasted_iota(jnp.int32, (cap, pos.shape[1]), 0) == pos
    onehot = jnp.where(hit, 1.0, 0.0).astype(BF16)
    o_ref[0] += lax.dot_general(onehot, y_ref[0, 0], _TN, preferred_element_type=F32)


def _combine_call(xn, pos4, y, tt):
    b, s, d = xn.shape
    e, cap = y.shape[1], y.shape[2]
    tok = pl.BlockSpec((1, tt, d), lambda i, t, j: (i, t, 0))
    return pl.pallas_call(
        functools.partial(_combine_kernel, cap=cap),
        grid=(b, s // tt, e),
        in_specs=[tok, pl.BlockSpec((1, 1, 1, tt), lambda i, t, j: (i, j, 0, t)),
                  pl.BlockSpec((1, 1, cap, d), lambda i, t, j: (i, j, 0, 0))],
        out_specs=tok,
        out_shape=jax.ShapeDtypeStruct((b, s, d), F32),
        compiler_params=_cparams(("parallel", "parallel", "arbitrary")),
        name="combine",
    )(xn, pos4, y)


def _block_diag(n, blk):
    i = np.arange(n)
    return jnp.asarray((i[:, None] // blk) == (i[None, :] // blk), dtype=BF16)


def _head_slots(w, heads, width):
    r = w.shape[0]
    return jnp.pad(w.reshape(r, heads, width), ((0, 0), (0, 0), (0, LANES - width))).reshape(r, heads * LANES)


def _rotate_half_cols(w):
    half = B_ROPE // 2
    return jnp.concatenate([-w[..., half:], w[..., :half]], axis=-1)


def _layer_params(l, w_in, norm_mix_g, diff_qk_g, mla_cq_g, w_uq, mla_ckv_g, w_ukv, mla_qk_g, swa_qk_g,
                  w_branch_a, w_branch_b, w_branch_c, w_o, norm_ffn_g, w_router):
    d = D_MODEL
    wi = w_in[l]
    off = np.cumsum([0, 512, 512, 512, B_Q_LORA, B_KV_LORA, B_ROPE, 512, 128, 128, 3 * d])
    piece = lambda k: wi[:, off[k]:off[k + 1]]
    maps_to_heads = lambda w: w.reshape(d, 2, A_HEADS, A_DIM).transpose(0, 2, 1, 3).reshape(d, 512)
    dup = lambda w: jnp.concatenate([w.reshape(d, C_KV_HEADS, 1, C_DIM)] * 2, axis=2).reshape(d, 256)
    kr = piece(5)
    rope_slot = lambda w: jnp.pad(w, ((0, 0), (B_NOPE, LANES - B_QK)))
    w_proj = jnp.concatenate(
        [maps_to_heads(piece(0)), maps_to_heads(piece(1)), piece(2), piece(3), piece(4), piece(6),
         dup(piece(7)), dup(piece(8)), rope_slot(kr), rope_slot(_rotate_half_cols(kr))], axis=1).astype(BF16)
    assert w_proj.shape == (d, _PROJ_COLS)

    wq = w_uq[l].reshape(B_Q_LORA, B_HEADS, B_QK)
    wq_rot = jnp.concatenate([jnp.zeros_like(wq[..., :B_NOPE]), _rotate_half_cols(wq[..., B_NOPE:])], axis=-1)
    w_uq_x = jnp.concatenate([_head_slots(wq.reshape(B_Q_LORA, -1), B_HEADS, B_QK),
                              _head_slots(wq_rot.reshape(B_Q_LORA, -1), B_HEADS, B_QK)], axis=1).astype(BF16)
    wkv = w_ukv[l].reshape(B_KV_LORA, B_HEADS, B_NOPE + B_VDIM)
    w_ukv_x = jnp.concatenate([_head_slots(wkv[..., :B_NOPE].reshape(B_KV_LORA, -1), B_HEADS, B_NOPE),
                               wkv[..., B_NOPE:].reshape(B_KV_LORA, -1)], axis=1).astype(BF16)

    row = lambda v: v.reshape(1, -1).astype(F32)
    slot_gain = lambda g: jnp.tile(jnp.pad(g, (0, LANES - B_QK)), B_HEADS)
    return {
        "gmix": row(norm_mix_g[l]), "w_in": w_proj, "w_uq": w_uq_x, "w_ukv": w_ukv_x,
        "e64": _block_diag(SEG_TILE, 64), "e128": _block_diag(SEG_TILE, LANES),
        "gqa": row(jnp.tile(diff_qk_g[l, 0], 8) * (A_DIM ** -0.5 * LOG2E)), "gka": row(jnp.tile(diff_qk_g[l, 1], 8)),
        "gcq": row(mla_cq_g[l]), "gckv": row(mla_ckv_g[l]),
        "gqb": row(slot_gain(mla_qk_g[l, 0]) * (B_QK ** -0.5 * LOG2E)), "gkb": row(slot_gain(mla_qk_g[l, 1])),
        "gqc": row(jnp.tile(swa_qk_g[l, 0], 8) * (C_DIM ** -0.5)), "gkc": row(jnp.tile(swa_qk_g[l, 1], 4)),
        "w_gate": piece(9).astype(BF16),
        "w_a": w_branch_a[l].astype(BF16), "w_b": w_branch_b[l].astype(BF16), "w_c": w_branch_c[l].astype(BF16),
        "w_o": w_o[l].astype(BF16), "gffn": row(norm_ffn_g[l]), "w_rt": w_router[l].T.astype(F32),
    }


def _rope_slot_tables(positions):
    inv = 1.0 / (ROPE_THETA ** (jnp.arange(0, B_ROPE, 2, dtype=F32) / B_ROPE))
    ang = positions.astype(F32)[..., None] * inv
    cos, sin = jnp.cos(ang), jnp.sin(ang)
    ones = jnp.ones(ang.shape[:-1] + (B_NOPE,), F32)
    pad = jnp.zeros(ang.shape[:-1] + (LANES - B_QK,), F32)
    return (jnp.concatenate([ones, cos, cos, pad], axis=-1),
            jnp.concatenate([jnp.zeros_like(ones), sin, sin, pad], axis=-1))


def _alibi_slopes(n):
    return 2.0 ** (-8.0 * jnp.arange(1, n + 1, dtype=F32) / n)


def kernel(x, positions, norm_mix_g, w_in, diff_qk_g, diff_lambda, diff_out_g, mla_cq_g, w_uq, mla_ckv_g, w_ukv,
           mla_qk_g, swa_qk_g, swa_sink, w_branch_a, w_branch_b, w_branch_c, w_o, norm_ffn_g, w_router,
           w_exp_gate, w_exp_up, w_exp_down):
    b, s, d = x.shape
    depth = w_in.shape[0]
    cap = max(1, EC_CAPACITY * s // N_EXPERTS)
    tm_proj = min(512, s)
    tq = min(256, s)
    tq_a = min(512, s)
    tm_merge = min(512, s)
    tt = min(2048, s)

    cos_t, sin_t = _rope_slot_tables(positions)
    pos_f = positions.astype(F32)
    pos_c, pos_r = pos_f[:, :, None], pos_f.reshape(b, s // tq_a, 1, tq_a)
    monotone = jnp.all(positions[:, 1:] >= positions[:, :-1])
    lane_bcast = lambda v: jnp.broadcast_to(v[..., None], v.shape + (LANES,)).astype(F32)
    slopes_a = lane_bcast(_alibi_slopes(A_HEADS)[:, None] * LOG2E)
    slopes_c = lane_bcast(_alibi_slopes(C_HEADS).reshape(C_KV_HEADS, C_REP))

    for l in range(depth):
        p = _layer_params(l, w_in, norm_mix_g, diff_qk_g, mla_cq_g, w_uq, mla_ckv_g, w_ukv, mla_qk_g, swa_qk_g,
                          w_branch_a, w_branch_b, w_branch_c, w_o, norm_ffn_g, w_router)
        qa, ka, va, qb, kb, vb, qc, kc, vc = _proj_call(x, cos_t, sin_t, p, tm_proj)
        lam_init = 0.8 - 0.6 * math.exp(-0.3 * l)
        attn_a = functools.partial(_attn_a_call, qa, ka, va, pos_c, pos_r, slopes_a, diff_lambda[l].astype(F32),
                                   diff_out_g[l].reshape(1, -1).astype(F32), lam_init, tq_a)
        oa = lax.cond(monotone, functools.partial(attn_a, True), functools.partial(attn_a, False))
        ob = _attn_b_call(qb, kb, vb, min(512, s))
        oc = _attn_c_call(qc, kc, vc, slopes_c, lane_bcast(swa_sink[l].reshape(C_KV_HEADS, C_REP)))
        xn, h2, aff_t = _merge_call(x, oa, ob, oc, p, tm_merge)
        pos4 = _topk_call(aff_t, cap).reshape(b, N_EXPERTS, 1, s)
        xin, asel = _gather_call(pos4, aff_t.reshape(b, N_EXPERTS, 1, s), h2, cap)
        y = _expert_call(xin, asel, w_exp_gate[l].astype(BF16), w_exp_up[l].astype(BF16),
                         w_exp_down[l].astype(BF16))
        x = _combine_call(xn, pos4, y, tt)
    return x
```

```python
import functools
import math

import numpy as np
import jax
import jax.numpy as jnp
from jax import lax
from jax.experimental import pallas as pl
from jax.experimental.pallas import tpu as pltpu
from jax.experimental.pallas import tpu_sc as plsc

F32 = jnp.float32
BF16 = jnp.bfloat16

D_MODEL = 1024
EPS = 1e-6
A_HEADS = 4
A_DIM = 64
A_VDIM = 128
B_HEADS = 8
B_NOPE = 64
B_ROPE = 32
B_VDIM = 64
B_QK = B_NOPE + B_ROPE
B_Q_LORA = 384
B_KV_LORA = 256
ROPE_THETA = 10000.0
C_HEADS = 8
C_KV_HEADS = 2
C_REP = C_HEADS // C_KV_HEADS
C_DIM = 64
WINDOW = 128
N_EXPERTS = 16
EC_CAPACITY = 2
D_FF = 1024
LOG2E = math.log2(math.e)
SC_CORES, SC_SUBCORES, SC_LANES = 2, 16, 16
SC_GATHER_ROWS = 128
SEG_TILE = 256
LANES = 128

_QA = 0
_KA = 512
_VA = 1024
_CQ = 1536
_CKV = 1920
_QC = 2176
_KC = 2688
_VC = 2944
_KR = 3200
_KRR = 3328
_PROJ_COLS = 3456

VMEM_LIMIT = 56 * 1024 * 1024

_NT = (((1,), (1,)), ((), ()))
_TN = (((0,), (0,)), ((), ()))


def _cparams(sem):
    return pltpu.CompilerParams(dimension_semantics=sem, vmem_limit_bytes=VMEM_LIMIT)


def _dot(a, b):
    return jnp.dot(a, b, preferred_element_type=F32)


def _seg_sum(x2, e):
    hi = x2.astype(BF16)
    lo = (x2 - hi.astype(F32)).astype(BF16)
    e2 = jnp.concatenate([e, e], axis=0)
    slabs = [_dot(jnp.concatenate([hi[:, c:c + SEG_TILE], lo[:, c:c + SEG_TILE]], axis=1), e2)
             for c in range(0, x2.shape[1], SEG_TILE)]
    return slabs[0] if len(slabs) == 1 else jnp.concatenate(slabs, axis=1)


def _sigmoid(x):
    return 0.5 * jnp.tanh(0.5 * x) + 0.5


def _row_rms(x, g):
    return x * lax.rsqrt(jnp.mean(x * x, axis=-1, keepdims=True) + EPS) * g


def _proj_kernel(x_ref, gmix_ref, w_ref, wuq_ref, wukv_ref, e64_ref, e128_ref,
                 gqa_ref, gka_ref, gcq_ref, gckv_ref, gqb_ref, gkb_ref, gqc_ref, gkc_ref,
                 cos_ref, sin_ref,
                 qa_o, ka_o, va_o, qb_o, kb_o, vb_o, qc_o, kc_o, vc_o):
    hb = _row_rms(x_ref[0], gmix_ref[...]).astype(BF16)

    projected = _dot(hb, w_ref[...])

    def proj(a, n):
        return projected[:, a:a + n]

    e64 = e64_ref[...]
    e128 = e128_ref[...]

    def seg_norm(v, e, width, g):
        return v * lax.rsqrt(_seg_sum(v * v, e) * (1.0 / width) + EPS) * g

    def store_slots(o_ref, v, n):
        for j in range(n):
            o_ref[0, j] = v[:, LANES * j:LANES * (j + 1)].astype(o_ref.dtype)

    ones_slot = jnp.where(lax.broadcasted_iota(jnp.int32, (hb.shape[0], LANES), 1) == 0, 1.0, 0.0)

    def store_value_slots(o_ref, v, n):
        for j in range(n):
            o_ref[0, j] = jnp.concatenate([v[:, LANES * j:LANES * (j + 1)], ones_slot], axis=1).astype(o_ref.dtype)

    store_slots(qa_o, seg_norm(proj(_QA, 512), e64, A_DIM, gqa_ref[...]), A_HEADS)
    store_slots(ka_o, seg_norm(proj(_KA, 512), e64, A_DIM, gka_ref[...]), A_HEADS)
    store_value_slots(va_o, proj(_VA, 512), A_HEADS)

    cos_t = cos_ref[0]
    sin_t = sin_ref[0]
    cos8 = jnp.concatenate([cos_t] * B_HEADS, axis=1)
    sin8 = jnp.concatenate([sin_t] * B_HEADS, axis=1)
    cq = _row_rms(proj(_CQ, B_Q_LORA), gcq_ref[...]).astype(BF16)
    q2 = _dot(cq, wuq_ref[...])
    qb = q2[:, :1024] * cos8 + q2[:, 1024:] * sin8
    store_slots(qb_o, seg_norm(qb, e128, B_QK, gqb_ref[...]), B_HEADS)
    ckv = _row_rms(proj(_CKV, B_KV_LORA), gckv_ref[...]).astype(BF16)
    kv = _dot(ckv, wukv_ref[...])
    kr = proj(_KR, LANES) * cos_t + proj(_KRR, LANES) * sin_t
    kb = kv[:, :1024] + jnp.concatenate([kr] * B_HEADS, axis=1)
    store_slots(kb_o, seg_norm(kb, e128, B_QK, gkb_ref[...]), B_HEADS)
    store_value_slots(vb_o, kv[:, 1024:], B_HEADS // 2)

    store_slots(qc_o, seg_norm(proj(_QC, 512), e64, C_DIM, gqc_ref[...]), C_HEADS // 2)
    store_slots(kc_o, seg_norm(proj(_KC, 256), e64, C_DIM, gkc_ref[...]), C_KV_HEADS)
    store_value_slots(vc_o, proj(_VC, 256), C_KV_HEADS)


def _proj_call(x, cos_t, sin_t, p, tm):
    b, s, d = x.shape
    full = lambda a: pl.BlockSpec(a.shape, lambda i, j: (0,) * a.ndim, pipeline_mode=pl.Buffered(1))
    slot = lambda nw: pl.BlockSpec((1, nw[0], tm, nw[1]), lambda i, j: (i, 0, j, 0))
    tok = lambda w: pl.BlockSpec((1, tm, w), lambda i, j: (i, j, 0))
    consts = [p["gmix"], p["w_in"], p["w_uq"], p["w_ukv"], p["e64"], p["e128"],
              p["gqa"], p["gka"], p["gcq"], p["gckv"], p["gqb"], p["gkb"], p["gqc"], p["gkc"]]
    slots = [(A_HEADS, LANES), (A_HEADS, LANES), (A_HEADS, 2 * LANES), (B_HEADS, LANES), (B_HEADS, LANES),
             (B_HEADS // 2, 2 * LANES), (C_HEADS // 2, LANES), (C_KV_HEADS, LANES), (C_KV_HEADS, 2 * LANES)]
    return pl.pallas_call(
        _proj_kernel,
        grid=(b, s // tm),
        in_specs=[tok(d)] + [full(a) for a in consts] + [tok(LANES), tok(LANES)],
        out_specs=[slot(nw) for nw in slots],
        out_shape=[jax.ShapeDtypeStruct((b, nw[0], s, nw[1]), BF16) for nw in slots],
        compiler_params=_cparams(("parallel", "parallel")),
        name="proj",
    )(x, *consts, cos_t, sin_t)


def _attn_a_kernel(q_ref, k_ref, v_ref, pc_ref, pr_ref, slope_ref, lam_ref, og_ref, o_ref, s_scr, *, lam_init, mono):
    t = pl.program_id(2)
    q = q_ref[0, 0]
    tq = q.shape[0]
    ck = tq
    nck = k_ref.shape[2] // ck
    lane = lax.broadcasted_iota(jnp.int32, q.shape, 1)
    zero = jnp.zeros_like(q)
    qm = [jnp.where(lane < A_DIM, q, zero), jnp.where(lane >= A_DIM, q, zero)]
    slope = slope_ref[0][:, :1]
    a = slope * pc_ref[0]
    a_lanes = jnp.broadcast_to(a, (tq, LANES))

    def lane_fold_max(x):
        out = x[:, :LANES]
        for j in range(1, ck // LANES):
            out = jnp.maximum(out, x[:, j * LANES:(j + 1) * LANES])
        return out

    mx = [jnp.full((tq, LANES), -jnp.inf, F32)] * 2
    for d in range(nck):
        if mono:
            c = t if d == 0 else lax.rem(t + d, nck)
            k_c = k_ref[0, 0, pl.ds(pl.multiple_of(c * ck, ck), ck), :]
        else:
            c = d
            k_c = k_ref[0, 0, d * ck:(d + 1) * ck, :]
        b = slope * pr_ref[0, c]
        if mono and d > 0:
            sign = jnp.where(t + d < nck, -1.0, 1.0)
            row_part, col_part = sign * b, sign * a_lanes
            for m in range(2):
                sc = lax.dot_general(qm[m], k_c, _NT, preferred_element_type=F32) + row_part
                s_scr[m, c] = sc
                mx[m] = jnp.maximum(mx[m], lane_fold_max(sc) - col_part)
        else:
            bias = jnp.abs(a - b)
            for m in range(2):
                sc = lax.dot_general(qm[m], k_c, _NT, preferred_element_type=F32) - bias
                s_scr[m, c] = sc
                mx[m] = jnp.maximum(mx[m], lane_fold_max(sc))
    v = v_ref[0, 0]
    acc = []
    for m in range(2):
        row_max = jnp.broadcast_to(jnp.max(mx[m], axis=-1, keepdims=True), (tq, LANES))
        es = []
        for c in range(nck):
            if mono:
                sign = jnp.where(c < t, 1.0, jnp.where(c > t, -1.0, 0.0))
                stab = row_max + sign * a_lanes
            else:
                stab = row_max
            es.append(jnp.exp2(s_scr[m, c] - jnp.concatenate([stab] * (ck // LANES), axis=1)).astype(BF16))
        e = es[0] if nck == 1 else jnp.concatenate(es, axis=1)
        acc.append(_dot(e, v))
    lp = lam_ref[...]
    lam = (jnp.exp(jnp.sum(lp[0:1] * lp[1:2], axis=-1, keepdims=True))
           - jnp.exp(jnp.sum(lp[2:3] * lp[3:4], axis=-1, keepdims=True)) + lam_init)
    o = (acc[0][:, :LANES] * (1.0 / acc[0][:, LANES:LANES + 1])
         - acc[1][:, :LANES] * (lam / acc[1][:, LANES:LANES + 1]))
    o_ref[0] = (_row_rms(o, og_ref[...]) * (1.0 - lam_init)).astype(o_ref.dtype)


def _attn_a_call(qa, ka, va, pos_c, pos_r, slopes, lam_p, out_g, lam_init, tq, mono):
    b, h, s, _ = qa.shape
    kv_spec = lambda w: pl.BlockSpec((1, 1, s, w), lambda i, j, t: (i, j, 0, 0))
    return pl.pallas_call(
        functools.partial(_attn_a_kernel, lam_init=lam_init, mono=mono),
        grid=(b, h, s // tq),
        scratch_shapes=[pltpu.VMEM((2, s // tq, tq, tq), F32)],
        in_specs=[pl.BlockSpec((1, 1, tq, LANES), lambda i, j, t: (i, j, t, 0)), kv_spec(LANES), kv_spec(2 * LANES),
                  pl.BlockSpec((1, tq, 1), lambda i, j, t: (i, t, 0)),
                  pl.BlockSpec((1, s // tq, 1, tq), lambda i, j, t: (i, 0, 0, 0)),
                  pl.BlockSpec((1, 1, LANES), lambda i, j, t: (j, 0, 0)),
                  pl.BlockSpec(lam_p.shape, lambda i, j, t: (0, 0)),
                  pl.BlockSpec(out_g.shape, lambda i, j, t: (0, 0))],
        out_specs=pl.BlockSpec((1, tq, LANES), lambda i, j, t: (i, t, j)),
        out_shape=jax.ShapeDtypeStruct((b, s, h * LANES), BF16),
        compiler_params=_cparams(("parallel", "parallel", "arbitrary")),
        name="attn_a",
    )(qa, ka, va, pos_c, pos_r, slopes, lam_p, out_g)


def _attn_b_kernel(q_ref, k_ref, v_ref, o_ref):
    tq = q_ref.shape[2]
    es = []
    for j in range(2):
        s = lax.dot_general(q_ref[0, j], k_ref[0, j], _NT, preferred_element_type=F32)
        es.append(jnp.exp2(s - jnp.max(s, axis=-1, keepdims=True)).astype(BF16))
    acc = _dot(jnp.concatenate(es, axis=0), v_ref[0, 0])
    lane = lax.broadcasted_iota(jnp.int32, (tq, LANES), 1)
    o_ref[0] = jnp.where(lane < B_VDIM, acc[:tq, :LANES] * (1.0 / acc[:tq, LANES:LANES + 1]),
                         acc[tq:, :LANES] * (1.0 / acc[tq:, LANES:LANES + 1])).astype(o_ref.dtype)


def _attn_b_call(qb, kb, vb, tq):
    b, h, s, _ = qb.shape
    return pl.pallas_call(
        _attn_b_kernel,
        grid=(b, h // 2, s // tq),
        in_specs=[pl.BlockSpec((1, 2, tq, LANES), lambda i, j, t: (i, j, t, 0)),
                  pl.BlockSpec((1, 2, s, LANES), lambda i, j, t: (i, j, 0, 0)),
                  pl.BlockSpec((1, 1, s, 2 * LANES), lambda i, j, t: (i, j, 0, 0))],
        out_specs=pl.BlockSpec((1, tq, LANES), lambda i, j, t: (i, t, j)),
        out_shape=jax.ShapeDtypeStruct((b, s, (h // 2) * LANES), BF16),
        compiler_params=_cparams(("parallel", "parallel", "arbitrary")),
        name="attn_b",
    )(qb, kb, vb)


def _attn_c_kernel(q_ref, kp_ref, ko_ref, kn_ref, vp_ref, vo_ref, vn_ref, slope_ref, sink_ref, o_ref, *, seq):
    w = WINDOW
    nsub = ko_ref.shape[2] // w
    n0 = pl.program_id(2) * nsub
    kcat = jnp.concatenate([kp_ref[0, 0], ko_ref[0, 0], kn_ref[0, 0]], axis=0)
    vcat = jnp.concatenate([vp_ref[0, 0], vo_ref[0, 0], vn_ref[0, 0]], axis=0)
    lane = lax.broadcasted_iota(jnp.int32, (w, LANES), 1)
    r_idx = lax.broadcasted_iota(jnp.int32, (w, 3 * w), 0)
    c_idx = lax.broadcasted_iota(jnp.int32, (w, 3 * w), 1)
    arel = jnp.abs(c_idx - w - r_idx)
    dist = arel.astype(F32)
    slopes = slope_ref[0]
    sinks = sink_ref[0]
    bias4 = jnp.concatenate([jnp.where(arel <= w, -slopes[r:r + 1, :1] * dist, -1e30) for r in range(C_REP)], axis=0)
    sink4 = jnp.concatenate([jnp.broadcast_to(sinks[r:r + 1, :1], (w, 1)) for r in range(C_REP)], axis=0)
    c_row = lax.broadcasted_iota(jnp.int32, (1, 3 * w), 1)
    scs = []
    for i in range(nsub):
        parts = []
        for p in range(2):
            q = q_ref[0, p, i * w:(i + 1) * w, :]
            zero = jnp.zeros_like(q)
            parts += [jnp.where(lane < C_DIM, q, zero), jnp.where(lane >= C_DIM, q, zero)]
        qz = jnp.concatenate(parts, axis=0)
        s = lax.dot_general(qz, kcat[i * w:(i + 3) * w], _NT, preferred_element_type=F32)
        kidx = (n0 + i - 1) * w + c_row
        edge = jnp.where(kidx >= 0, jnp.where(kidx < seq, 0.0, -1e30), -1e30)
        scs.append(s + bias4 + edge)
    sc = jnp.concatenate(scs, axis=0)
    sk = jnp.concatenate([sink4] * nsub, axis=0)
    m = jnp.maximum(jnp.max(sc, axis=-1, keepdims=True), sk)
    e = jnp.exp(sc - m).astype(BF16)
    tail = jnp.exp(sk - m)
    for i in range(nsub):
        rows = slice(i * 4 * w, (i + 1) * 4 * w)
        acc = _dot(e[rows], vcat[i * w:(i + 3) * w])
        o = acc[:, :LANES] * (1.0 / (acc[:, LANES:LANES + 1] + tail[rows]))
        pair0 = jnp.where(lane < C_DIM, o[0:w], o[w:2 * w])
        pair1 = jnp.where(lane < C_DIM, o[2 * w:3 * w], o[3 * w:4 * w])
        o_ref[0, i * w:(i + 1) * w, :] = jnp.concatenate([pair0, pair1], axis=1).astype(o_ref.dtype)


def _attn_c_call(qc, kc, vc, slopes, sinks):
    b, _, s, _ = qc.shape
    nb = s // WINDOW
    nsub = min(4, nb)
    tq = nsub * WINDOW
    prev = lambda wd: pl.BlockSpec((1, 1, WINDOW, wd), lambda i, g, n: (i, g, jnp.maximum(n * nsub - 1, 0), 0))
    own = lambda wd: pl.BlockSpec((1, 1, tq, wd), lambda i, g, n: (i, g, n, 0))
    nxt = lambda wd: pl.BlockSpec((1, 1, WINDOW, wd), lambda i, g, n: (i, g, jnp.minimum((n + 1) * nsub, nb - 1), 0))
    kw, vw = kc.shape[-1], vc.shape[-1]
    per_group = pl.BlockSpec((1, C_REP, LANES), lambda i, g, n: (g, 0, 0))
    return pl.pallas_call(
        functools.partial(_attn_c_kernel, seq=s),
        grid=(b, C_KV_HEADS, s // tq),
        in_specs=[pl.BlockSpec((1, 2, tq, LANES), lambda i, g, n: (i, g, n, 0)),
                  prev(kw), own(kw), nxt(kw), prev(vw), own(vw), nxt(vw), per_group, per_group],
        out_specs=pl.BlockSpec((1, tq, 2 * LANES), lambda i, g, n: (i, n, g)),
        out_shape=jax.ShapeDtypeStruct((b, s, C_HEADS * C_DIM), BF16),
        compiler_params=_cparams(("parallel", "parallel", "arbitrary")),
        name="attn_c",
    )(qc, kc, kc, kc, vc, vc, vc, slopes, sinks)


def _merge_kernel(x_ref, gmix_ref, wg_ref, oa_ref, ob_ref, oc_ref, wa_ref, wb_ref, wc_ref, wo_ref,
                  gffn_ref, wr_ref, xn_o, h2_o, aff_o):
    d = D_MODEL
    x = x_ref[0]
    hb = _row_rms(x, gmix_ref[...]).astype(BF16)
    g = _sigmoid(_dot(hb, wg_ref[...]))
    merged = (g[:, :d] * _dot(oa_ref[0], wa_ref[...]) + g[:, d:2 * d] * _dot(ob_ref[0], wb_ref[...])
              + g[:, 2 * d:] * _dot(oc_ref[0], wc_ref[...]))
    xn = x + _dot(merged.astype(BF16), wo_ref[...])
    xn_o[0] = xn
    h2 = _row_rms(xn, gffn_ref[...])
    h2b = h2.astype(BF16).astype(F32)
    half = d // 2
    lo = pltpu.bitcast(h2b[:, :half], jnp.uint32) >> 16
    hi = pltpu.bitcast(h2b[:, half:], jnp.uint32) & jnp.uint32(0xFFFF0000)
    h2_o[0] = pltpu.bitcast(lo | hi, F32)
    logits = lax.dot_general(wr_ref[...], h2, _NT, preferred_element_type=F32,
                             precision=lax.Precision.HIGHEST)
    ex = jnp.exp(logits - jnp.max(logits, axis=0, keepdims=True))
    aff_o[0] = ex / jnp.sum(ex, axis=0, keepdims=True)


def _merge_call(x, oa, ob, oc, p, tm):
    b, s, d = x.shape
    full = lambda a: pl.BlockSpec(a.shape, lambda i, j: (0,) * a.ndim, pipeline_mode=pl.Buffered(1))
    tok = lambda w: pl.BlockSpec((1, tm, w), lambda i, j: (i, j, 0))
    return pl.pallas_call(
        _merge_kernel,
        grid=(b, s // tm),
        in_specs=[tok(d), full(p["gmix"]), full(p["w_gate"]), tok(512), tok(512), tok(512),
                  full(p["w_a"]), full(p["w_b"]), full(p["w_c"]), full(p["w_o"]), full(p["gffn"]), full(p["w_rt"])],
        out_specs=[tok(d), tok(d // 2), pl.BlockSpec((1, N_EXPERTS, tm), lambda i, j: (i, 0, j))],
        out_shape=[jax.ShapeDtypeStruct((b, s, d), F32), jax.ShapeDtypeStruct((b, s, d // 2), F32),
                   jax.ShapeDtypeStruct((b, N_EXPERTS, s), F32)],
        compiler_params=_cparams(("parallel", "parallel")),
        name="merge",
    )(x, p["gmix"], p["w_gate"], oa, ob, oc, p["w_a"], p["w_b"], p["w_c"], p["w_o"], p["gffn"], p["w_rt"])


def _cumsum_lanes(mask01, chunk):
    rows, s = mask01.shape
    tri = jnp.where(lax.broadcasted_iota(jnp.int32, (chunk, chunk), 0)
                    <= lax.broadcasted_iota(jnp.int32, (chunk, chunk), 1), 1.0, 0.0).astype(BF16)
    carry = jnp.zeros((rows, 1), F32)
    outs = []
    for c in range(s // chunk):
        cs = _dot(mask01[:, c * chunk:(c + 1) * chunk], tri) + carry
        outs.append(cs)
        carry = cs[:, chunk - 1:chunk]
    return jnp.concatenate(outs, axis=1)


def _topk_kernel(aff_ref, pos_o, *, cap, chunk):
    bits = pltpu.bitcast(aff_ref[0], jnp.int32)
    rows = bits.shape[0]
    capf = float(cap)

    def count(mask):
        return jnp.sum(jnp.where(mask, 1.0, 0.0), axis=1, keepdims=True)

    def body(_, c):
        lo, hi = c
        mid = lo + ((hi - lo + 1) >> 1)
        ok = count(bits >= mid) >= capf
        return jnp.where(ok, mid, lo), jnp.where(ok, hi, mid - 1)

    lo0 = jnp.zeros((rows, 1), jnp.int32)
    hi0 = jnp.full((rows, 1), 0x7F800000, jnp.int32)
    thr, _ = lax.fori_loop(0, 31, body, (lo0, hi0))
    gt = bits > thr
    eq = bits == thr
    need = capf - count(gt)
    eq_rank = _cumsum_lanes(jnp.where(eq, 1.0, 0.0).astype(BF16), chunk)
    sel = jnp.where(gt, 1.0, jnp.where(eq, jnp.where(eq_rank <= need, 1.0, 0.0), 0.0))
    slot = _cumsum_lanes(sel.astype(BF16), chunk) - 1.0
    pos_o[0] = jnp.where(sel > 0.0, slot, -1.0).astype(jnp.int32)


def _topk_call(aff_t, cap):
    b, e, s = aff_t.shape
    spec = pl.BlockSpec((1, e, s), lambda i: (i, 0, 0))
    return pl.pallas_call(
        functools.partial(_topk_kernel, cap=cap, chunk=min(512, s)),
        grid=(b,),
        in_specs=[spec],
        out_specs=spec,
        out_shape=jax.ShapeDtypeStruct((b, e, s), jnp.int32),
        compiler_params=_cparams(("parallel",)),
        name="topk",
    )(aff_t)


def _sc_gather_call(table, pos2, aff2, cap):
    npairs, s = pos2.shape
    width = table.shape[1]
    workers = SC_CORES * SC_SUBCORES
    per_worker = npairs // workers
    nchunk = cap // SC_GATHER_ROWS
    assert npairs % workers == 0 and cap % SC_GATHER_ROWS == 0 and s % SC_LANES == 0
    mesh = plsc.VectorSubcoreMesh(core_axis_name="c", subcore_axis_name="s")

    @functools.partial(
        pl.kernel, mesh=mesh,
        out_type=[jax.ShapeDtypeStruct((npairs * cap, width), F32),
                  jax.ShapeDtypeStruct((npairs * cap * SC_LANES,), F32)],
        scratch_types=[pltpu.VMEM((s,), jnp.int32), pltpu.VMEM((s,), F32),
                       pltpu.VMEM((nchunk, SC_GATHER_ROWS), jnp.int32), pltpu.VMEM((cap,), F32),
                       pltpu.VMEM((cap * SC_LANES,), F32), pltpu.VMEM((SC_GATHER_ROWS, width), F32),
                       pltpu.SemaphoreType.DMA],
        compiler_params=pltpu.CompilerParams(needs_layout_passes=False),
        name="sc_gather",
    )
    def gather(table_hbm, pos_hbm, aff_hbm, rows_hbm, asel_hbm, pos_v, aff_v, idx_v, aslot_v, asplat_v, rows_v, sem):
        wid = lax.axis_index("s") * SC_CORES + lax.axis_index("c")
        lanes = lax.iota(jnp.int32, SC_LANES)

        @pl.loop(0, per_worker)
        def _(j):
            pair = wid * per_worker + j
            row0 = (pair // N_EXPERTS) * s
            pltpu.sync_copy(pos_hbm.at[pair], pos_v)
            pltpu.sync_copy(aff_hbm.at[pair], aff_v)

            @pl.loop(0, s // SC_LANES)
            def _(i):
                sl = pl.ds(i * SC_LANES, SC_LANES)
                slot = pos_v[sl]
                chosen = slot >= 0
                token_row = lanes + (i * SC_LANES + row0)
                plsc.store_scatter(idx_v, [slot >> (SC_GATHER_ROWS.bit_length() - 1), slot & (SC_GATHER_ROWS - 1)], token_row, mask=chosen)
                plsc.store_scatter(aslot_v, [slot], aff_v[sl], mask=chosen)

            @pl.loop(0, cap)
            def _(r):
                asplat_v[pl.ds(r * SC_LANES, SC_LANES)] = plsc.load_gather(
                    aslot_v, [jnp.full((SC_LANES,), r, jnp.int32)])

            pltpu.sync_copy(asplat_v, asel_hbm.at[pl.ds(pair * (cap * SC_LANES), cap * SC_LANES)])
            for c in range(nchunk):
                pltpu.async_copy(table_hbm.at[idx_v.at[c]], rows_v, sem).wait()
                pltpu.sync_copy(rows_v, rows_hbm.at[pl.ds(pair * cap + c * SC_GATHER_ROWS, SC_GATHER_ROWS)])

    return gather(table, pos2, aff2)


def _expert_kernel(xin_ref, asel_ref, wg_ref, wu_ref, wd_ref, y_o):
    bits = pltpu.bitcast(xin_ref[0, 0], jnp.uint32)
    xin = jnp.concatenate([pltpu.bitcast(bits << 16, F32), pltpu.bitcast(bits & jnp.uint32(0xFFFF0000), F32)],
                          axis=1).astype(BF16)
    g = _dot(xin, wg_ref[0])
    u = _dot(xin, wu_ref[0])
    hid = (g * _sigmoid(g) * u).astype(BF16)
    y_o[0, 0] = (_dot(hid, wd_ref[0]) * asel_ref[0, 0][:, :1]).astype(y_o.dtype)


def _expert_call(xin, asel, wg, wu, wd):
    b, e, cap, _ = xin.shape
    d, f = wg.shape[-2:]
    tokens = lambda w: pl.BlockSpec((1, 1, cap, w), lambda j, i: (i, j, 0, 0))
    return pl.pallas_call(
        _expert_kernel,
        grid=(e, b),
        in_specs=[tokens(xin.shape[-1]), tokens(asel.shape[-1]),
                  pl.BlockSpec((1, d, f), lambda j, i: (j, 0, 0)),
                  pl.BlockSpec((1, d, f), lambda j, i: (j, 0, 0)),
                  pl.BlockSpec((1, f, d), lambda j, i: (j, 0, 0))],
        out_specs=tokens(d),
        out_shape=jax.ShapeDtypeStruct((b, e, cap, d), BF16),
        compiler_params=_cparams(("parallel", "arbitrary")),
        name="expert",
    )(xin, asel, wg, wu, wd)


def _combine_kernel(x_ref, pos_ref, y_ref, o_ref, *, cap):
    @pl.when(pl.program_id(2) == 0)
    def _():
        o_ref[...] = x_ref[...]

    pos = pos_ref[0, 0]
    hit = lax.broadcasted_iota(jnp.int32, (cap, pos.shape[1]), 0) == pos
    onehot = jnp.where(hit, 1.0, 0.0).astype(BF16)
    o_ref[0] += lax.dot_general(onehot, y_ref[0, 0], _TN, preferred_element_type=F32)


def _combine_call(xn, pos4, y, tt):
    b, s, d = xn.shape
    e, cap = y.shape[1], y.shape[2]
    tok = pl.BlockSpec((1, tt, d), lambda i, t, j: (i, t, 0))
    return pl.pallas_call(
        functools.partial(_combine_kernel, cap=cap),
        grid=(b, s // tt, e),
        in_specs=[tok, pl.BlockSpec((1, 1, 1, tt), lambda i, t, j: (i, j, 0, t)),
                  pl.BlockSpec((1, 1, cap, d), lambda i, t, j: (i, j, 0, 0))],
        out_specs=tok,
        out_shape=jax.ShapeDtypeStruct((b, s, d), F32),
        compiler_params=_cparams(("parallel", "parallel", "arbitrary")),
        name="combine",
    )(xn, pos4, y)


def _block_diag(n, blk):
    i = np.arange(n)
    return jnp.asarray((i[:, None] // blk) == (i[None, :] // blk), dtype=BF16)


def _head_slots(w, heads, width):
    r = w.shape[0]
    return jnp.pad(w.reshape(r, heads, width), ((0, 0), (0, 0), (0, LANES - width))).reshape(r, heads * LANES)


def _rotate_half_cols(w):
    half = B_ROPE // 2
    return jnp.concatenate([-w[..., half:], w[..., :half]], axis=-1)


def _layer_params(l, w_in, norm_mix_g, diff_qk_g, mla_cq_g, w_uq, mla_ckv_g, w_ukv, mla_qk_g, swa_qk_g,
                  w_branch_a, w_branch_b, w_branch_c, w_o, norm_ffn_g, w_router):
    d = D_MODEL
    wi = w_in[l]
    off = np.cumsum([0, 512, 512, 512, B_Q_LORA, B_KV_LORA, B_ROPE, 512, 128, 128, 3 * d])
    piece = lambda k: wi[:, off[k]:off[k + 1]]
    maps_to_heads = lambda w: w.reshape(d, 2, A_HEADS, A_DIM).transpose(0, 2, 1, 3).reshape(d, 512)
    dup = lambda w: jnp.concatenate([w.reshape(d, C_KV_HEADS, 1, C_DIM)] * 2, axis=2).reshape(d, 256)
    kr = piece(5)
    rope_slot = lambda w: jnp.pad(w, ((0, 0), (B_NOPE, LANES - B_QK)))
    w_proj = jnp.concatenate(
        [maps_to_heads(piece(0)), maps_to_heads(piece(1)), piece(2), piece(3), piece(4), piece(6),
         dup(piece(7)), dup(piece(8)), rope_slot(kr), rope_slot(_rotate_half_cols(kr))], axis=1).astype(BF16)
    assert w_proj.shape == (d, _PROJ_COLS)

    wq = w_uq[l].reshape(B_Q_LORA, B_HEADS, B_QK)
    wq_rot = jnp.concatenate([jnp.zeros_like(wq[..., :B_NOPE]), _rotate_half_cols(wq[..., B_NOPE:])], axis=-1)
    w_uq_x = jnp.concatenate([_head_slots(wq.reshape(B_Q_LORA, -1), B_HEADS, B_QK),
                              _head_slots(wq_rot.reshape(B_Q_LORA, -1), B_HEADS, B_QK)], axis=1).astype(BF16)
    wkv = w_ukv[l].reshape(B_KV_LORA, B_HEADS, B_NOPE + B_VDIM)
    w_ukv_x = jnp.concatenate([_head_slots(wkv[..., :B_NOPE].reshape(B_KV_LORA, -1), B_HEADS, B_NOPE),
                               wkv[..., B_NOPE:].reshape(B_KV_LORA, -1)], axis=1).astype(BF16)

    row = lambda v: v.reshape(1, -1).astype(F32)
    slot_gain = lambda g: jnp.tile(jnp.pad(g, (0, LANES - B_QK)), B_HEADS)
    return {
        "gmix": row(norm_mix_g[l]), "w_in": w_proj, "w_uq": w_uq_x, "w_ukv": w_ukv_x,
        "e64": _block_diag(SEG_TILE, 64), "e128": _block_diag(SEG_TILE, LANES),
        "gqa": row(jnp.tile(diff_qk_g[l, 0], 8) * (A_DIM ** -0.5 * LOG2E)), "gka": row(jnp.tile(diff_qk_g[l, 1], 8)),
        "gcq": row(mla_cq_g[l]), "gckv": row(mla_ckv_g[l]),
        "gqb": row(slot_gain(mla_qk_g[l, 0]) * (B_QK ** -0.5 * LOG2E)), "gkb": row(slot_gain(mla_qk_g[l, 1])),
        "gqc": row(jnp.tile(swa_qk_g[l, 0], 8) * (C_DIM ** -0.5)), "gkc": row(jnp.tile(swa_qk_g[l, 1], 4)),
        "w_gate": piece(9).astype(BF16),
        "w_a": w_branch_a[l].astype(BF16), "w_b": w_branch_b[l].astype(BF16), "w_c": w_branch_c[l].astype(BF16),
        "w_o": w_o[l].astype(BF16), "gffn": row(norm_ffn_g[l]), "w_rt": w_router[l].T.astype(F32),
    }


def _rope_slot_tables(positions):
    inv = 1.0 / (ROPE_THETA ** (jnp.arange(0, B_ROPE, 2, dtype=F32) / B_ROPE))
    ang = positions.astype(F32)[..., None] * inv
    cos, sin = jnp.cos(ang), jnp.sin(ang)
    ones = jnp.ones(ang.shape[:-1] + (B_NOPE,), F32)
    pad = jnp.zeros(ang.shape[:-1] + (LANES - B_QK,), F32)
    return (jnp.concatenate([ones, cos, cos, pad], axis=-1),
            jnp.concatenate([jnp.zeros_like(ones), sin, sin, pad], axis=-1))


def _alibi_slopes(n):
    return 2.0 ** (-8.0 * jnp.arange(1, n + 1, dtype=F32) / n)


def kernel(x, positions, norm_mix_g, w_in, diff_qk_g, diff_lambda, diff_out_g, mla_cq_g, w_uq, mla_ckv_g, w_ukv,
           mla_qk_g, swa_qk_g, swa_sink, w_branch_a, w_branch_b, w_branch_c, w_o, norm_ffn_g, w_router,
           w_exp_gate, w_exp_up, w_exp_down):
    b, s, d = x.shape
    depth = w_in.shape[0]
    cap = max(1, EC_CAPACITY * s // N_EXPERTS)
    tm_proj = min(512, s)
    tq = min(256, s)
    tq_a = min(512, s)
    tm_merge = min(512, s)
    tt = min(2048, s)

    cos_t, sin_t = _rope_slot_tables(positions)
    pos_f = positions.astype(F32)
    pos_c, pos_r = pos_f[:, :, None], pos_f.reshape(b, s // tq_a, 1, tq_a)
    monotone = jnp.all(positions[:, 1:] >= positions[:, :-1])
    lane_bcast = lambda v: jnp.broadcast_to(v[..., None], v.shape + (LANES,)).astype(F32)
    slopes_a = lane_bcast(_alibi_slopes(A_HEADS)[:, None] * LOG2E)
    slopes_c = lane_bcast(_alibi_slopes(C_HEADS).reshape(C_KV_HEADS, C_REP))

    for l in range(depth):
        p = _layer_params(l, w_in, norm_mix_g, diff_qk_g, mla_cq_g, w_uq, mla_ckv_g, w_ukv, mla_qk_g, swa_qk_g,
                          w_branch_a, w_branch_b, w_branch_c, w_o, norm_ffn_g, w_router)
        qa, ka, va, qb, kb, vb, qc, kc, vc = _proj_call(x, cos_t, sin_t, p, tm_proj)
        lam_init = 0.8 - 0.6 * math.exp(-0.3 * l)
        attn_a = functools.partial(_attn_a_call, qa, ka, va, pos_c, pos_r, slopes_a, diff_lambda[l].astype(F32),
                                   diff_out_g[l].reshape(1, -1).astype(F32), lam_init, tq_a)
        oa = lax.cond(monotone, functools.partial(attn_a, True), functools.partial(attn_a, False))
        ob = _attn_b_call(qb, kb, vb, min(512, s))
        oc = _attn_c_call(qc, kc, vc, slopes_c, lane_bcast(swa_sink[l].reshape(C_KV_HEADS, C_REP)))
        xn, h2, aff_t = _merge_call(x, oa, ob, oc, p, tm_merge)
        pos = _topk_call(aff_t, cap)
        pos4 = pos.reshape(b, N_EXPERTS, 1, s)
        rows, asel = _sc_gather_call(h2.reshape(b * s, d // 2), pos.reshape(b * N_EXPERTS, s),
                                     aff_t.reshape(b * N_EXPERTS, s), cap)
        xin = rows.reshape(b, N_EXPERTS, cap, d // 2)
        asel = asel.reshape(b, N_EXPERTS, cap, SC_LANES)
        y = _expert_call(xin, asel, w_exp_gate[l].astype(BF16), w_exp_up[l].astype(BF16),
                         w_exp_down[l].astype(BF16))
        x = _combine_call(xn, pos4, y, tt)
    return x
```

```python
import functools
import math

import numpy as np
import jax
import jax.numpy as jnp
from jax import lax
from jax.experimental import pallas as pl
from jax.experimental.pallas import tpu as pltpu
from jax.experimental.pallas import tpu_sc as plsc

F32 = jnp.float32
BF16 = jnp.bfloat16

D_MODEL = 1024
EPS = 1e-6
A_HEADS = 4
A_DIM = 64
A_VDIM = 128
B_HEADS = 8
B_NOPE = 64
B_ROPE = 32
B_VDIM = 64
B_QK = B_NOPE + B_ROPE
B_Q_LORA = 384
B_KV_LORA = 256
ROPE_THETA = 10000.0
C_HEADS = 8
C_KV_HEADS = 2
C_REP = C_HEADS // C_KV_HEADS
C_DIM = 64
WINDOW = 128
N_EXPERTS = 16
EC_CAPACITY = 2
D_FF = 1024
LOG2E = math.log2(math.e)
SC_CORES, SC_SUBCORES, SC_LANES = 2, 16, 16
SC_GATHER_ROWS = 128
SEG_TILE = 256
LANES = 128

_QA = 0
_KA = 512
_VA = 1024
_CQ = 1536
_CKV = 1920
_QC = 2176
_KC = 2688
_VC = 2944
_KR = 3200
_KRR = 3328
_PROJ_COLS = 3456

VMEM_LIMIT = 56 * 1024 * 1024

_NT = (((1,), (1,)), ((), ()))
_TN = (((0,), (0,)), ((), ()))


def _cparams(sem):
    return pltpu.CompilerParams(dimension_semantics=sem, vmem_limit_bytes=VMEM_LIMIT)


def _dot(a, b):
    return jnp.dot(a, b, preferred_element_type=F32)


def _seg_sum(x2, e):
    hi = x2.astype(BF16)
    lo = (x2 - hi.astype(F32)).astype(BF16)
    e2 = jnp.concatenate([e, e], axis=0)
    slabs = [_dot(jnp.concatenate([hi[:, c:c + SEG_TILE], lo[:, c:c + SEG_TILE]], axis=1), e2)
             for c in range(0, x2.shape[1], SEG_TILE)]
    return slabs[0] if len(slabs) == 1 else jnp.concatenate(slabs, axis=1)


def _pack_rows(x):
    half = x.shape[1] // 2
    xb = x.astype(BF16).astype(F32)
    lo = pltpu.bitcast(xb[:, :half], jnp.uint32) >> 16
    hi = pltpu.bitcast(xb[:, half:], jnp.uint32) & jnp.uint32(0xFFFF0000)
    return pltpu.bitcast(lo | hi, F32)


def _unpack_rows(words):
    bits = pltpu.bitcast(words, jnp.uint32)
    return jnp.concatenate([pltpu.bitcast(bits << 16, F32), pltpu.bitcast(bits & jnp.uint32(0xFFFF0000), F32)],
                           axis=1).astype(BF16)


def _sigmoid(x):
    return 0.5 * jnp.tanh(0.5 * x) + 0.5


def _row_rms(x, g):
    return x * lax.rsqrt(jnp.mean(x * x, axis=-1, keepdims=True) + EPS) * g


def _proj_kernel(x_ref, gmix_ref, w_ref, wuq_ref, wukv_ref, e64_ref, e128_ref,
                 gqa_ref, gka_ref, gcq_ref, gckv_ref, gqb_ref, gkb_ref, gqc_ref, gkc_ref,
                 cos_ref, sin_ref,
                 qa_o, ka_o, va_o, qb_o, kb_o, vb_o, qc_o, kc_o, vc_o):
    hb = _row_rms(x_ref[0], gmix_ref[...]).astype(BF16)

    projected = _dot(hb, w_ref[...])

    def proj(a, n):
        return projected[:, a:a + n]

    e64 = e64_ref[...]
    e128 = e128_ref[...]

    def seg_norm(v, e, width, g):
        return v * lax.rsqrt(_seg_sum(v * v, e) * (1.0 / width) + EPS) * g

    def store_slots(o_ref, v, n):
        for j in range(n):
            o_ref[0, j] = v[:, LANES * j:LANES * (j + 1)].astype(o_ref.dtype)

    ones_slot = jnp.where(lax.broadcasted_iota(jnp.int32, (hb.shape[0], LANES), 1) == 0, 1.0, 0.0)

    def store_value_slots(o_ref, v, n):
        for j in range(n):
            o_ref[0, j] = jnp.concatenate([v[:, LANES * j:LANES * (j + 1)], ones_slot], axis=1).astype(o_ref.dtype)

    store_slots(qa_o, seg_norm(proj(_QA, 512), e64, A_DIM, gqa_ref[...]), A_HEADS)
    store_slots(ka_o, seg_norm(proj(_KA, 512), e64, A_DIM, gka_ref[...]), A_HEADS)
    store_value_slots(va_o, proj(_VA, 512), A_HEADS)

    cos_t = cos_ref[0]
    sin_t = sin_ref[0]
    cos8 = jnp.concatenate([cos_t] * B_HEADS, axis=1)
    sin8 = jnp.concatenate([sin_t] * B_HEADS, axis=1)
    cq = _row_rms(proj(_CQ, B_Q_LORA), gcq_ref[...]).astype(BF16)
    q2 = _dot(cq, wuq_ref[...])
    qb = q2[:, :1024] * cos8 + q2[:, 1024:] * sin8
    store_slots(qb_o, seg_norm(qb, e128, B_QK, gqb_ref[...]), B_HEADS)
    ckv = _row_rms(proj(_CKV, B_KV_LORA), gckv_ref[...]).astype(BF16)
    kv = _dot(ckv, wukv_ref[...])
    kr = proj(_KR, LANES) * cos_t + proj(_KRR, LANES) * sin_t
    kb = kv[:, :1024] + jnp.concatenate([kr] * B_HEADS, axis=1)
    store_slots(kb_o, seg_norm(kb, e128, B_QK, gkb_ref[...]), B_HEADS)
    store_value_slots(vb_o, kv[:, 1024:], B_HEADS // 2)

    store_slots(qc_o, seg_norm(proj(_QC, 512), e64, C_DIM, gqc_ref[...]), C_HEADS // 2)
    store_slots(kc_o, seg_norm(proj(_KC, 256), e64, C_DIM, gkc_ref[...]), C_KV_HEADS)
    store_value_slots(vc_o, proj(_VC, 256), C_KV_HEADS)


def _proj_call(x, cos_t, sin_t, p, tm):
    b, s, d = x.shape
    full = lambda a: pl.BlockSpec(a.shape, lambda i, j: (0,) * a.ndim, pipeline_mode=pl.Buffered(1))
    slot = lambda nw: pl.BlockSpec((1, nw[0], tm, nw[1]), lambda i, j: (i, 0, j, 0))
    tok = lambda w: pl.BlockSpec((1, tm, w), lambda i, j: (i, j, 0))
    consts = [p["gmix"], p["w_in"], p["w_uq"], p["w_ukv"], p["e64"], p["e128"],
              p["gqa"], p["gka"], p["gcq"], p["gckv"], p["gqb"], p["gkb"], p["gqc"], p["gkc"]]
    slots = [(A_HEADS, LANES), (A_HEADS, LANES), (A_HEADS, 2 * LANES), (B_HEADS, LANES), (B_HEADS, LANES),
             (B_HEADS // 2, 2 * LANES), (C_HEADS // 2, LANES), (C_KV_HEADS, LANES), (C_KV_HEADS, 2 * LANES)]
    return pl.pallas_call(
        _proj_kernel,
        grid=(b, s // tm),
        in_specs=[tok(d)] + [full(a) for a in consts] + [tok(LANES), tok(LANES)],
        out_specs=[slot(nw) for nw in slots],
        out_shape=[jax.ShapeDtypeStruct((b, nw[0], s, nw[1]), BF16) for nw in slots],
        compiler_params=_cparams(("parallel", "parallel")),
        name="proj",
    )(x, *consts, cos_t, sin_t)


def _attn_a_kernel(q_ref, k_ref, v_ref, pc_ref, pr_ref, slope_ref, lam_ref, og_ref, o_ref, s_scr, *, lam_init, mono):
    t = pl.program_id(2)
    q = q_ref[0, 0]
    tq = q.shape[0]
    ck = tq
    nck = k_ref.shape[2] // ck
    lane = lax.broadcasted_iota(jnp.int32, q.shape, 1)
    zero = jnp.zeros_like(q)
    qm = [jnp.where(lane < A_DIM, q, zero), jnp.where(lane >= A_DIM, q, zero)]
    slope = slope_ref[0][:, :1]
    a = slope * pc_ref[0]
    a_lanes = jnp.broadcast_to(a, (tq, LANES))

    def lane_fold_max(x):
        out = x[:, :LANES]
        for j in range(1, ck // LANES):
            out = jnp.maximum(out, x[:, j * LANES:(j + 1) * LANES])
        return out

    mx = [jnp.full((tq, LANES), -jnp.inf, F32)] * 2
    for d in range(nck):
        if mono:
            c = t if d == 0 else lax.rem(t + d, nck)
            k_c = k_ref[0, 0, pl.ds(pl.multiple_of(c * ck, ck), ck), :]
        else:
            c = d
            k_c = k_ref[0, 0, d * ck:(d + 1) * ck, :]
        b = slope * pr_ref[0, c]
        if mono and d > 0:
            sign = jnp.where(t + d < nck, -1.0, 1.0)
            row_part, col_part = sign * b, sign * a_lanes
            for m in range(2):
                sc = lax.dot_general(qm[m], k_c, _NT, preferred_element_type=F32) + row_part
                s_scr[m, c] = sc
                mx[m] = jnp.maximum(mx[m], lane_fold_max(sc) - col_part)
        else:
            bias = jnp.abs(a - b)
            for m in range(2):
                sc = lax.dot_general(qm[m], k_c, _NT, preferred_element_type=F32) - bias
                s_scr[m, c] = sc
                mx[m] = jnp.maximum(mx[m], lane_fold_max(sc))
    v = v_ref[0, 0]
    acc = []
    for m in range(2):
        row_max = jnp.broadcast_to(jnp.max(mx[m], axis=-1, keepdims=True), (tq, LANES))
        es = []
        for c in range(nck):
            if mono:
                sign = jnp.where(c < t, 1.0, jnp.where(c > t, -1.0, 0.0))
                stab = row_max + sign * a_lanes
            else:
                stab = row_max
            es.append(jnp.exp2(s_scr[m, c] - jnp.concatenate([stab] * (ck // LANES), axis=1)).astype(BF16))
        e = es[0] if nck == 1 else jnp.concatenate(es, axis=1)
        acc.append(_dot(e, v))
    lp = lam_ref[...]
    lam = (jnp.exp(jnp.sum(lp[0:1] * lp[1:2], axis=-1, keepdims=True))
           - jnp.exp(jnp.sum(lp[2:3] * lp[3:4], axis=-1, keepdims=True)) + lam_init)
    o = (acc[0][:, :LANES] * (1.0 / acc[0][:, LANES:LANES + 1])
         - acc[1][:, :LANES] * (lam / acc[1][:, LANES:LANES + 1]))
    o_ref[0] = (_row_rms(o, og_ref[...]) * (1.0 - lam_init)).astype(o_ref.dtype)


def _attn_a_call(qa, ka, va, pos_c, pos_r, slopes, lam_p, out_g, lam_init, tq, mono):
    b, h, s, _ = qa.shape
    kv_spec = lambda w: pl.BlockSpec((1, 1, s, w), lambda i, j, t: (i, j, 0, 0))
    return pl.pallas_call(
        functools.partial(_attn_a_kernel, lam_init=lam_init, mono=mono),
        grid=(b, h, s // tq),
        scratch_shapes=[pltpu.VMEM((2, s // tq, tq, tq), F32)],
        in_specs=[pl.BlockSpec((1, 1, tq, LANES), lambda i, j, t: (i, j, t, 0)), kv_spec(LANES), kv_spec(2 * LANES),
                  pl.BlockSpec((1, tq, 1), lambda i, j, t: (i, t, 0)),
                  pl.BlockSpec((1, s // tq, 1, tq), lambda i, j, t: (i, 0, 0, 0)),
                  pl.BlockSpec((1, 1, LANES), lambda i, j, t: (j, 0, 0)),
                  pl.BlockSpec(lam_p.shape, lambda i, j, t: (0, 0)),
                  pl.BlockSpec(out_g.shape, lambda i, j, t: (0, 0))],
        out_specs=pl.BlockSpec((1, tq, LANES), lambda i, j, t: (i, t, j)),
        out_shape=jax.ShapeDtypeStruct((b, s, h * LANES), BF16),
        compiler_params=_cparams(("parallel", "parallel", "arbitrary")),
        name="attn_a",
    )(qa, ka, va, pos_c, pos_r, slopes, lam_p, out_g)


def _attn_b_kernel(q_ref, k_ref, v_ref, o_ref):
    tq = q_ref.shape[2]
    es = []
    for j in range(2):
        s = lax.dot_general(q_ref[0, j], k_ref[0, j], _NT, preferred_element_type=F32)
        es.append(jnp.exp2(s - jnp.max(s, axis=-1, keepdims=True)).astype(BF16))
    acc = _dot(jnp.concatenate(es, axis=0), v_ref[0, 0])
    lane = lax.broadcasted_iota(jnp.int32, (tq, LANES), 1)
    o_ref[0] = jnp.where(lane < B_VDIM, acc[:tq, :LANES] * (1.0 / acc[:tq, LANES:LANES + 1]),
                         acc[tq:, :LANES] * (1.0 / acc[tq:, LANES:LANES + 1])).astype(o_ref.dtype)


def _attn_b_call(qb, kb, vb, tq):
    b, h, s, _ = qb.shape
    return pl.pallas_call(
        _attn_b_kernel,
        grid=(b, h // 2, s // tq),
        in_specs=[pl.BlockSpec((1, 2, tq, LANES), lambda i, j, t: (i, j, t, 0)),
                  pl.BlockSpec((1, 2, s, LANES), lambda i, j, t: (i, j, 0, 0)),
                  pl.BlockSpec((1, 1, s, 2 * LANES), lambda i, j, t: (i, j, 0, 0))],
        out_specs=pl.BlockSpec((1, tq, LANES), lambda i, j, t: (i, t, j)),
        out_shape=jax.ShapeDtypeStruct((b, s, (h // 2) * LANES), BF16),
        compiler_params=_cparams(("parallel", "parallel", "arbitrary")),
        name="attn_b",
    )(qb, kb, vb)


def _attn_c_kernel(q_ref, kp_ref, ko_ref, kn_ref, vp_ref, vo_ref, vn_ref, slope_ref, sink_ref, o_ref, *, seq):
    w = WINDOW
    nsub = ko_ref.shape[2] // w
    n0 = pl.program_id(2) * nsub
    kcat = jnp.concatenate([kp_ref[0, 0], ko_ref[0, 0], kn_ref[0, 0]], axis=0)
    vcat = jnp.concatenate([vp_ref[0, 0], vo_ref[0, 0], vn_ref[0, 0]], axis=0)
    lane = lax.broadcasted_iota(jnp.int32, (w, LANES), 1)
    r_idx = lax.broadcasted_iota(jnp.int32, (w, 3 * w), 0)
    c_idx = lax.broadcasted_iota(jnp.int32, (w, 3 * w), 1)
    arel = jnp.abs(c_idx - w - r_idx)
    dist = arel.astype(F32)
    slopes = slope_ref[0]
    sinks = sink_ref[0]
    bias4 = jnp.concatenate([jnp.where(arel <= w, -slopes[r:r + 1, :1] * dist, -1e30) for r in range(C_REP)], axis=0)
    sink4 = jnp.concatenate([jnp.broadcast_to(sinks[r:r + 1, :1], (w, 1)) for r in range(C_REP)], axis=0)
    c_row = lax.broadcasted_iota(jnp.int32, (1, 3 * w), 1)
    scs = []
    for i in range(nsub):
        parts = []
        for p in range(2):
            q = q_ref[0, p, i * w:(i + 1) * w, :]
            zero = jnp.zeros_like(q)
            parts += [jnp.where(lane < C_DIM, q, zero), jnp.where(lane >= C_DIM, q, zero)]
        qz = jnp.concatenate(parts, axis=0)
        s = lax.dot_general(qz, kcat[i * w:(i + 3) * w], _NT, preferred_element_type=F32)
        kidx = (n0 + i - 1) * w + c_row
        edge = jnp.where(kidx >= 0, jnp.where(kidx < seq, 0.0, -1e30), -1e30)
        scs.append(s + bias4 + edge)
    sc = jnp.concatenate(scs, axis=0)
    sk = jnp.concatenate([sink4] * nsub, axis=0)
    m = jnp.maximum(jnp.max(sc, axis=-1, keepdims=True), sk)
    e = jnp.exp(sc - m).astype(BF16)
    tail = jnp.exp(sk - m)
    for i in range(nsub):
        rows = slice(i * 4 * w, (i + 1) * 4 * w)
        acc = _dot(e[rows], vcat[i * w:(i + 3) * w])
        o = acc[:, :LANES] * (1.0 / (acc[:, LANES:LANES + 1] + tail[rows]))
        pair0 = jnp.where(lane < C_DIM, o[0:w], o[w:2 * w])
        pair1 = jnp.where(lane < C_DIM, o[2 * w:3 * w], o[3 * w:4 * w])
        o_ref[0, i * w:(i + 1) * w, :] = jnp.concatenate([pair0, pair1], axis=1).astype(o_ref.dtype)


def _attn_c_call(qc, kc, vc, slopes, sinks):
    b, _, s, _ = qc.shape
    nb = s // WINDOW
    nsub = min(4, nb)
    tq = nsub * WINDOW
    prev = lambda wd: pl.BlockSpec((1, 1, WINDOW, wd), lambda i, g, n: (i, g, jnp.maximum(n * nsub - 1, 0), 0))
    own = lambda wd: pl.BlockSpec((1, 1, tq, wd), lambda i, g, n: (i, g, n, 0))
    nxt = lambda wd: pl.BlockSpec((1, 1, WINDOW, wd), lambda i, g, n: (i, g, jnp.minimum((n + 1) * nsub, nb - 1), 0))
    kw, vw = kc.shape[-1], vc.shape[-1]
    per_group = pl.BlockSpec((1, C_REP, LANES), lambda i, g, n: (g, 0, 0))
    return pl.pallas_call(
        functools.partial(_attn_c_kernel, seq=s),
        grid=(b, C_KV_HEADS, s // tq),
        in_specs=[pl.BlockSpec((1, 2, tq, LANES), lambda i, g, n: (i, g, n, 0)),
                  prev(kw), own(kw), nxt(kw), prev(vw), own(vw), nxt(vw), per_group, per_group],
        out_specs=pl.BlockSpec((1, tq, 2 * LANES), lambda i, g, n: (i, n, g)),
        out_shape=jax.ShapeDtypeStruct((b, s, C_HEADS * C_DIM), BF16),
        compiler_params=_cparams(("parallel", "parallel", "arbitrary")),
        name="attn_c",
    )(qc, kc, kc, kc, vc, vc, vc, slopes, sinks)


def _merge_kernel(x_ref, gmix_ref, wg_ref, oa_ref, ob_ref, oc_ref, wa_ref, wb_ref, wc_ref, wo_ref,
                  gffn_ref, wr_ref, xn_o, h2_o, aff_o):
    d = D_MODEL
    x = x_ref[0]
    hb = _row_rms(x, gmix_ref[...]).astype(BF16)
    g = _sigmoid(_dot(hb, wg_ref[...]))
    merged = (g[:, :d] * _dot(oa_ref[0], wa_ref[...]) + g[:, d:2 * d] * _dot(ob_ref[0], wb_ref[...])
              + g[:, 2 * d:] * _dot(oc_ref[0], wc_ref[...]))
    xn = x + _dot(merged.astype(BF16), wo_ref[...])
    xn_o[0] = xn
    h2 = _row_rms(xn, gffn_ref[...])
    h2_o[0] = _pack_rows(h2)
    logits = lax.dot_general(wr_ref[...], h2, _NT, preferred_element_type=F32,
                             precision=lax.Precision.HIGHEST)
    ex = jnp.exp(logits - jnp.max(logits, axis=0, keepdims=True))
    aff_o[0] = ex / jnp.sum(ex, axis=0, keepdims=True)


def _merge_call(x, oa, ob, oc, p, tm):
    b, s, d = x.shape
    full = lambda a: pl.BlockSpec(a.shape, lambda i, j: (0,) * a.ndim, pipeline_mode=pl.Buffered(1))
    tok = lambda w: pl.BlockSpec((1, tm, w), lambda i, j: (i, j, 0))
    return pl.pallas_call(
        _merge_kernel,
        grid=(b, s // tm),
        in_specs=[tok(d), full(p["gmix"]), full(p["w_gate"]), tok(512), tok(512), tok(512),
                  full(p["w_a"]), full(p["w_b"]), full(p["w_c"]), full(p["w_o"]), full(p["gffn"]), full(p["w_rt"])],
        out_specs=[tok(d), tok(d // 2), pl.BlockSpec((1, N_EXPERTS, tm), lambda i, j: (i, 0, j))],
        out_shape=[jax.ShapeDtypeStruct((b, s, d), F32), jax.ShapeDtypeStruct((b, s, d // 2), F32),
                   jax.ShapeDtypeStruct((b, N_EXPERTS, s), F32)],
        compiler_params=_cparams(("parallel", "parallel")),
        name="merge",
    )(x, p["gmix"], p["w_gate"], oa, ob, oc, p["w_a"], p["w_b"], p["w_c"], p["w_o"], p["gffn"], p["w_rt"])


def _cumsum_lanes(mask01, chunk):
    rows, s = mask01.shape
    tri = jnp.where(lax.broadcasted_iota(jnp.int32, (chunk, chunk), 0)
                    <= lax.broadcasted_iota(jnp.int32, (chunk, chunk), 1), 1.0, 0.0).astype(BF16)
    carry = jnp.zeros((rows, 1), F32)
    outs = []
    for c in range(s // chunk):
        cs = _dot(mask01[:, c * chunk:(c + 1) * chunk], tri) + carry
        outs.append(cs)
        carry = cs[:, chunk - 1:chunk]
    return jnp.concatenate(outs, axis=1)


def _topk_kernel(aff_ref, pos_o, dest_o, start_o, *, cap, chunk):
    bits = pltpu.bitcast(aff_ref[0], jnp.int32)
    rows = bits.shape[0]
    capf = float(cap)

    def count(mask):
        return jnp.sum(jnp.where(mask, 1.0, 0.0), axis=1, keepdims=True)

    def body(_, c):
        lo, hi = c
        mid = lo + ((hi - lo + 1) >> 1)
        ok = count(bits >= mid) >= capf
        return jnp.where(ok, mid, lo), jnp.where(ok, hi, mid - 1)

    lo0 = jnp.zeros((rows, 1), jnp.int32)
    hi0 = jnp.full((rows, 1), 0x7F800000, jnp.int32)
    thr, _ = lax.fori_loop(0, 31, body, (lo0, hi0))
    gt = bits > thr
    eq = bits == thr
    need = capf - count(gt)
    eq_rank = _cumsum_lanes(jnp.where(eq, 1.0, 0.0).astype(BF16), chunk)
    sel = jnp.where(gt, 1.0, jnp.where(eq, jnp.where(eq_rank <= need, 1.0, 0.0), 0.0))
    sel_b = sel.astype(BF16)
    slot = _cumsum_lanes(sel_b, chunk) - 1.0
    pos_o[0] = jnp.where(sel > 0.0, slot, -1.0).astype(jnp.int32)
    per_token = jnp.broadcast_to(jnp.sum(sel, axis=0, keepdims=True), (8, sel.shape[1]))
    before_token = (_cumsum_lanes(per_token.astype(BF16), chunk) - per_token)[0:1]
    lower = jnp.where(lax.broadcasted_iota(jnp.int32, (rows, rows), 1)
                      < lax.broadcasted_iota(jnp.int32, (rows, rows), 0), 1.0, 0.0).astype(BF16)
    before_expert = _dot(lower, sel_b)
    dest_o[0] = jnp.where(sel > 0.0, before_token + before_expert, -1.0).astype(jnp.int32)
    start_o[0] = before_token.astype(jnp.int32)


def _topk_call(aff_t, cap):
    b, e, s = aff_t.shape
    spec = pl.BlockSpec((1, e, s), lambda i: (i, 0, 0))
    return pl.pallas_call(
        functools.partial(_topk_kernel, cap=cap, chunk=min(512, s)),
        grid=(b,),
        in_specs=[spec],
        out_specs=[spec, spec, pl.BlockSpec((1, 1, s), lambda i: (i, 0, 0))],
        out_shape=[jax.ShapeDtypeStruct((b, e, s), jnp.int32), jax.ShapeDtypeStruct((b, e, s), jnp.int32),
                   jax.ShapeDtypeStruct((b, 1, s), jnp.int32)],
        compiler_params=_cparams(("parallel",)),
        name="topk",
    )(aff_t)


def _sc_gather_call(table, pos2, aff2, cap):
    npairs, s = pos2.shape
    width = table.shape[1]
    workers = SC_CORES * SC_SUBCORES
    per_worker = npairs // workers
    nchunk = cap // SC_GATHER_ROWS
    assert npairs % workers == 0 and cap % SC_GATHER_ROWS == 0 and s % SC_LANES == 0
    mesh = plsc.VectorSubcoreMesh(core_axis_name="c", subcore_axis_name="s")

    @functools.partial(
        pl.kernel, mesh=mesh,
        out_type=[jax.ShapeDtypeStruct((npairs * cap, width), F32),
                  jax.ShapeDtypeStruct((npairs * cap * SC_LANES,), F32)],
        scratch_types=[pltpu.VMEM((s,), jnp.int32), pltpu.VMEM((s,), F32),
                       pltpu.VMEM((nchunk, SC_GATHER_ROWS), jnp.int32), pltpu.VMEM((cap,), F32),
                       pltpu.VMEM((cap * SC_LANES,), F32), pltpu.VMEM((SC_GATHER_ROWS, width), F32),
                       pltpu.SemaphoreType.DMA],
        compiler_params=pltpu.CompilerParams(needs_layout_passes=False),
        name="sc_gather",
    )
    def gather(table_hbm, pos_hbm, aff_hbm, rows_hbm, asel_hbm, pos_v, aff_v, idx_v, aslot_v, asplat_v, rows_v, sem):
        wid = lax.axis_index("s") * SC_CORES + lax.axis_index("c")
        lanes = lax.iota(jnp.int32, SC_LANES)

        @pl.loop(0, per_worker)
        def _(j):
            pair = wid * per_worker + j
            row0 = (pair // N_EXPERTS) * s
            pltpu.sync_copy(pos_hbm.at[pair], pos_v)
            pltpu.sync_copy(aff_hbm.at[pair], aff_v)

            @pl.loop(0, s // SC_LANES)
            def _(i):
                sl = pl.ds(i * SC_LANES, SC_LANES)
                slot = pos_v[sl]
                chosen = slot >= 0
                token_row = lanes + (i * SC_LANES + row0)
                plsc.store_scatter(idx_v, [slot >> (SC_GATHER_ROWS.bit_length() - 1), slot & (SC_GATHER_ROWS - 1)], token_row, mask=chosen)
                plsc.store_scatter(aslot_v, [slot], aff_v[sl], mask=chosen)

            @pl.loop(0, cap)
            def _(r):
                asplat_v[pl.ds(r * SC_LANES, SC_LANES)] = plsc.load_gather(
                    aslot_v, [jnp.full((SC_LANES,), r, jnp.int32)])

            pltpu.sync_copy(asplat_v, asel_hbm.at[pl.ds(pair * (cap * SC_LANES), cap * SC_LANES)])
            for c in range(nchunk):
                pltpu.async_copy(table_hbm.at[idx_v.at[c]], rows_v, sem).wait()
                pltpu.sync_copy(rows_v, rows_hbm.at[pl.ds(pair * cap + c * SC_GATHER_ROWS, SC_GATHER_ROWS)])

    return gather(table, pos2, aff2)


def _expert_kernel(xin_ref, asel_ref, wg_ref, wu_ref, wd_ref, y_o):
    xin = _unpack_rows(xin_ref[0, 0])
    g = _dot(xin, wg_ref[0])
    u = _dot(xin, wu_ref[0])
    hid = (g * _sigmoid(g) * u).astype(BF16)
    y_o[0, 0] = _pack_rows(_dot(hid, wd_ref[0]) * asel_ref[0, 0][:, :1])


def _expert_call(xin, asel, wg, wu, wd):
    b, e, cap, _ = xin.shape
    d, f = wg.shape[-2:]
    tokens = lambda w: pl.BlockSpec((1, 1, cap, w), lambda j, i: (i, j, 0, 0))
    return pl.pallas_call(
        _expert_kernel,
        grid=(e, b),
        in_specs=[tokens(xin.shape[-1]), tokens(asel.shape[-1]),
                  pl.BlockSpec((1, d, f), lambda j, i: (j, 0, 0)),
                  pl.BlockSpec((1, d, f), lambda j, i: (j, 0, 0)),
                  pl.BlockSpec((1, f, d), lambda j, i: (j, 0, 0))],
        out_specs=tokens(d // 2),
        out_shape=jax.ShapeDtypeStruct((b, e, cap, d // 2), F32),
        compiler_params=_cparams(("parallel", "arbitrary")),
        name="expert",
    )(xin, asel, wg, wu, wd)


def _sc_regroup_call(y_rows, dest2, pos2, cap):
    npairs, s = dest2.shape
    width = y_rows.shape[1]
    workers = SC_CORES * SC_SUBCORES
    batches = npairs // N_EXPERTS
    per_batch = N_EXPERTS * cap
    split = workers // batches
    span = per_batch // split
    nchunk = span // SC_GATHER_ROWS
    assert workers % batches == 0 and per_batch % split == 0 and span % SC_GATHER_ROWS == 0 and s % SC_LANES == 0
    mesh = plsc.VectorSubcoreMesh(core_axis_name="c", subcore_axis_name="s")

    @functools.partial(
        pl.kernel, mesh=mesh,
        out_type=[jax.ShapeDtypeStruct((npairs * cap, width), F32),
                  jax.ShapeDtypeStruct((npairs * cap,), jnp.int32)],
        scratch_types=[pltpu.VMEM((s,), jnp.int32), pltpu.VMEM((s,), jnp.int32),
                       pltpu.VMEM((nchunk, SC_GATHER_ROWS), jnp.int32), pltpu.VMEM((span,), jnp.int32),
                       pltpu.VMEM((SC_GATHER_ROWS, width), F32), pltpu.SemaphoreType.DMA],
        compiler_params=pltpu.CompilerParams(needs_layout_passes=False),
        name="sc_regroup",
    )
    def regroup(y_hbm, dest_hbm, pos_hbm, rows_hbm, tok_hbm, dest_v, pos_v, src_v, tok_v, rows_v, sem):
        wid = lax.axis_index("s") * SC_CORES + lax.axis_index("c")
        batch = wid // split
        first = (wid % split) * span
        lanes = lax.iota(jnp.int32, SC_LANES)

        @pl.loop(0, N_EXPERTS)
        def _(e):
            pair = batch * N_EXPERTS + e
            pltpu.sync_copy(dest_hbm.at[pair], dest_v)
            pltpu.sync_copy(pos_hbm.at[pair], pos_v)

            @pl.loop(0, s // SC_LANES)
            def _(i):
                sl = pl.ds(i * SC_LANES, SC_LANES)
                local = dest_v[sl] - first
                mine = (local >= 0) & (local < span)
                plsc.store_scatter(src_v, [local >> (SC_GATHER_ROWS.bit_length() - 1), local & (SC_GATHER_ROWS - 1)],
                                   pos_v[sl] + pair * cap, mask=mine)
                plsc.store_scatter(tok_v, [local], lanes + i * SC_LANES, mask=mine)

        out0 = batch * per_batch + first
        pltpu.sync_copy(tok_v, tok_hbm.at[pl.ds(out0, span)])
        for c in range(nchunk):
            pltpu.async_copy(y_hbm.at[src_v.at[c]], rows_v, sem).wait()
            pltpu.sync_copy(rows_v, rows_hbm.at[pl.ds(out0 + c * SC_GATHER_ROWS, SC_GATHER_ROWS)])

    return regroup(y_rows, dest2, pos2)


def _combine_kernel(start_ref, x_ref, tok_ref, rows_ref, o_ref, *, ntile, ck):
    b, j = pl.program_id(0), pl.program_id(1)
    tt = x_ref.shape[1]
    lo = start_ref[b * (ntile + 1) + j]
    hi = start_ref[b * (ntile + 1) + j + 1]
    c_lo = lo // ck
    c_hi = jnp.where(hi > lo, (hi - 1) // ck + 1, c_lo)
    tokens = j * tt + lax.broadcasted_iota(jnp.int32, (tt, ck), 0)
    o_ref[0] = x_ref[0]

    def body(c, carry):
        onehot = jnp.where(tok_ref[0, c] == tokens, 1.0, 0.0).astype(BF16)
        o_ref[0] += _dot(onehot, _unpack_rows(rows_ref[0, c]))
        return carry

    lax.fori_loop(c_lo, c_hi, body, 0)


def _combine_call(xn, starts, tok, rows, tt, ck):
    b, s, d = xn.shape
    nchunk = tok.shape[1] // ck
    ntile = s // tt
    return pl.pallas_call(
        functools.partial(_combine_kernel, ntile=ntile, ck=ck),
        grid_spec=pltpu.PrefetchScalarGridSpec(
            num_scalar_prefetch=1,
            grid=(b, ntile),
            in_specs=[pl.BlockSpec((1, tt, d), lambda i, t, st: (i, t, 0)),
                      pl.BlockSpec((1, nchunk, 1, ck), lambda i, t, st: (i, 0, 0, 0)),
                      pl.BlockSpec((1, nchunk, ck, rows.shape[-1]), lambda i, t, st: (i, 0, 0, 0))],
            out_specs=pl.BlockSpec((1, tt, d), lambda i, t, st: (i, t, 0)),
        ),
        out_shape=jax.ShapeDtypeStruct((b, s, d), F32),
        compiler_params=_cparams(("parallel", "arbitrary")),
        name="combine",
    )(starts, xn, tok.reshape(b, nchunk, 1, ck), rows.reshape(b, nchunk, ck, rows.shape[-1]))


def _block_diag(n, blk):
    i = np.arange(n)
    return jnp.asarray((i[:, None] // blk) == (i[None, :] // blk), dtype=BF16)


def _head_slots(w, heads, width):
    r = w.shape[0]
    return jnp.pad(w.reshape(r, heads, width), ((0, 0), (0, 0), (0, LANES - width))).reshape(r, heads * LANES)


def _rotate_half_cols(w):
    half = B_ROPE // 2
    return jnp.concatenate([-w[..., half:], w[..., :half]], axis=-1)


def _layer_params(l, w_in, norm_mix_g, diff_qk_g, mla_cq_g, w_uq, mla_ckv_g, w_ukv, mla_qk_g, swa_qk_g,
                  w_branch_a, w_branch_b, w_branch_c, w_o, norm_ffn_g, w_router):
    d = D_MODEL
    wi = w_in[l]
    off = np.cumsum([0, 512, 512, 512, B_Q_LORA, B_KV_LORA, B_ROPE, 512, 128, 128, 3 * d])
    piece = lambda k: wi[:, off[k]:off[k + 1]]
    maps_to_heads = lambda w: w.reshape(d, 2, A_HEADS, A_DIM).transpose(0, 2, 1, 3).reshape(d, 512)
    dup = lambda w: jnp.concatenate([w.reshape(d, C_KV_HEADS, 1, C_DIM)] * 2, axis=2).reshape(d, 256)
    kr = piece(5)
    rope_slot = lambda w: jnp.pad(w, ((0, 0), (B_NOPE, LANES - B_QK)))
    w_proj = jnp.concatenate(
        [maps_to_heads(piece(0)), maps_to_heads(piece(1)), piece(2), piece(3), piece(4), piece(6),
         dup(piece(7)), dup(piece(8)), rope_slot(kr), rope_slot(_rotate_half_cols(kr))], axis=1).astype(BF16)
    assert w_proj.shape == (d, _PROJ_COLS)

    wq = w_uq[l].reshape(B_Q_LORA, B_HEADS, B_QK)
    wq_rot = jnp.concatenate([jnp.zeros_like(wq[..., :B_NOPE]), _rotate_half_cols(wq[..., B_NOPE:])], axis=-1)
    w_uq_x = jnp.concatenate([_head_slots(wq.reshape(B_Q_LORA, -1), B_HEADS, B_QK),
                              _head_slots(wq_rot.reshape(B_Q_LORA, -1), B_HEADS, B_QK)], axis=1).astype(BF16)
    wkv = w_ukv[l].reshape(B_KV_LORA, B_HEADS, B_NOPE + B_VDIM)
    w_ukv_x = jnp.concatenate([_head_slots(wkv[..., :B_NOPE].reshape(B_KV_LORA, -1), B_HEADS, B_NOPE),
                               wkv[..., B_NOPE:].reshape(B_KV_LORA, -1)], axis=1).astype(BF16)

    row = lambda v: v.reshape(1, -1).astype(F32)
    slot_gain = lambda g: jnp.tile(jnp.pad(g, (0, LANES - B_QK)), B_HEADS)
    return {
        "gmix": row(norm_mix_g[l]), "w_in": w_proj, "w_uq": w_uq_x, "w_ukv": w_ukv_x,
        "e64": _block_diag(SEG_TILE, 64), "e128": _block_diag(SEG_TILE, LANES),
        "gqa": row(jnp.tile(diff_qk_g[l, 0], 8) * (A_DIM ** -0.5 * LOG2E)), "gka": row(jnp.tile(diff_qk_g[l, 1], 8)),
        "gcq": row(mla_cq_g[l]), "gckv": row(mla_ckv_g[l]),
        "gqb": row(slot_gain(mla_qk_g[l, 0]) * (B_QK ** -0.5 * LOG2E)), "gkb": row(slot_gain(mla_qk_g[l, 1])),
        "gqc": row(jnp.tile(swa_qk_g[l, 0], 8) * (C_DIM ** -0.5)), "gkc": row(jnp.tile(swa_qk_g[l, 1], 4)),
        "w_gate": piece(9).astype(BF16),
        "w_a": w_branch_a[l].astype(BF16), "w_b": w_branch_b[l].astype(BF16), "w_c": w_branch_c[l].astype(BF16),
        "w_o": w_o[l].astype(BF16), "gffn": row(norm_ffn_g[l]), "w_rt": w_router[l].T.astype(F32),
    }


def _rope_slot_tables(positions):
    inv = 1.0 / (ROPE_THETA ** (jnp.arange(0, B_ROPE, 2, dtype=F32) / B_ROPE))
    ang = positions.astype(F32)[..., None] * inv
    cos, sin = jnp.cos(ang), jnp.sin(ang)
    ones = jnp.ones(ang.shape[:-1] + (B_NOPE,), F32)
    pad = jnp.zeros(ang.shape[:-1] + (LANES - B_QK,), F32)
    return (jnp.concatenate([ones, cos, cos, pad], axis=-1),
            jnp.concatenate([jnp.zeros_like(ones), sin, sin, pad], axis=-1))


def _alibi_slopes(n):
    return 2.0 ** (-8.0 * jnp.arange(1, n + 1, dtype=F32) / n)


def kernel(x, positions, norm_mix_g, w_in, diff_qk_g, diff_lambda, diff_out_g, mla_cq_g, w_uq, mla_ckv_g, w_ukv,
           mla_qk_g, swa_qk_g, swa_sink, w_branch_a, w_branch_b, w_branch_c, w_o, norm_ffn_g, w_router,
           w_exp_gate, w_exp_up, w_exp_down):
    b, s, d = x.shape
    depth = w_in.shape[0]
    cap = max(1, EC_CAPACITY * s // N_EXPERTS)
    tm_proj = min(512, s)
    tq = min(256, s)
    tq_a = min(512, s)
    tm_merge = min(512, s)
    tt = min(256, s)

    cos_t, sin_t = _rope_slot_tables(positions)
    pos_f = positions.astype(F32)
    pos_c, pos_r = pos_f[:, :, None], pos_f.reshape(b, s // tq_a, 1, tq_a)
    monotone = jnp.all(positions[:, 1:] >= positions[:, :-1])
    lane_bcast = lambda v: jnp.broadcast_to(v[..., None], v.shape + (LANES,)).astype(F32)
    slopes_a = lane_bcast(_alibi_slopes(A_HEADS)[:, None] * LOG2E)
    slopes_c = lane_bcast(_alibi_slopes(C_HEADS).reshape(C_KV_HEADS, C_REP))

    for l in range(depth):
        p = _layer_params(l, w_in, norm_mix_g, diff_qk_g, mla_cq_g, w_uq, mla_ckv_g, w_ukv, mla_qk_g, swa_qk_g,
                          w_branch_a, w_branch_b, w_branch_c, w_o, norm_ffn_g, w_router)
        qa, ka, va, qb, kb, vb, qc, kc, vc = _proj_call(x, cos_t, sin_t, p, tm_proj)
        lam_init = 0.8 - 0.6 * math.exp(-0.3 * l)
        attn_a = functools.partial(_attn_a_call, qa, ka, va, pos_c, pos_r, slopes_a, diff_lambda[l].astype(F32),
                                   diff_out_g[l].reshape(1, -1).astype(F32), lam_init, tq_a)
        oa = lax.cond(monotone, functools.partial(attn_a, True), functools.partial(attn_a, False))
        ob = _attn_b_call(qb, kb, vb, min(512, s))
        oc = _attn_c_call(qc, kc, vc, slopes_c, lane_bcast(swa_sink[l].reshape(C_KV_HEADS, C_REP)))
        xn, h2, aff_t = _merge_call(x, oa, ob, oc, p, tm_merge)
        pos, dest, start = _topk_call(aff_t, cap)
        pos2, dest2 = pos.reshape(b * N_EXPERTS, s), dest.reshape(b * N_EXPERTS, s)
        rows, asel = _sc_gather_call(h2.reshape(b * s, d // 2), pos2, aff_t.reshape(b * N_EXPERTS, s), cap)
        xin = rows.reshape(b, N_EXPERTS, cap, d // 2)
        asel = asel.reshape(b, N_EXPERTS, cap, SC_LANES)
        y = _expert_call(xin, asel, w_exp_gate[l].astype(BF16), w_exp_up[l].astype(BF16),
                         w_exp_down[l].astype(BF16))
        y_rows, y_tok = _sc_regroup_call(y.reshape(b * N_EXPERTS * cap, d // 2), dest2, pos2, cap)
        starts = jnp.concatenate([start[:, 0, ::tt], jnp.full((b, 1), N_EXPERTS * cap, jnp.int32)], axis=1)
        x = _combine_call(xn, starts.reshape(-1), y_tok.reshape(b, N_EXPERTS * cap),
                          y_rows.reshape(b, N_EXPERTS * cap, d // 2), tt, min(256, N_EXPERTS * cap))
    return x
```

```python
import functools
import math

import numpy as np
import jax
import jax.numpy as jnp
from jax import lax
from jax.experimental import pallas as pl
from jax.experimental.pallas import tpu as pltpu
from jax.experimental.pallas import tpu_sc as plsc

F32 = jnp.float32
BF16 = jnp.bfloat16

D_MODEL = 1024
EPS = 1e-6
A_HEADS = 4
A_DIM = 64
A_VDIM = 128
B_HEADS = 8
B_NOPE = 64
B_ROPE = 32
B_VDIM = 64
B_QK = B_NOPE + B_ROPE
B_Q_LORA = 384
B_KV_LORA = 256
ROPE_THETA = 10000.0
C_HEADS = 8
C_KV_HEADS = 2
C_REP = C_HEADS // C_KV_HEADS
C_DIM = 64
WINDOW = 128
N_EXPERTS = 16
EC_CAPACITY = 2
D_FF = 1024
LOG2E = math.log2(math.e)
SC_CORES, SC_SUBCORES, SC_LANES = 2, 16, 16
SC_GATHER_ROWS = 128
SEG_TILE = 256
LANES = 128

_QA = 0
_KA = 512
_VA = 1024
_CQ = 1536
_CKV = 1920
_QC = 2176
_KC = 2688
_VC = 2944
_KR = 3200
_KRR = 3328
_PROJ_COLS = 3456

VMEM_LIMIT = 56 * 1024 * 1024

_NT = (((1,), (1,)), ((), ()))
_TN = (((0,), (0,)), ((), ()))


def _cparams(sem):
    return pltpu.CompilerParams(dimension_semantics=sem, vmem_limit_bytes=VMEM_LIMIT)


def _dot(a, b):
    return jnp.dot(a, b, preferred_element_type=F32)


def _seg_sum(x2, e):
    hi = x2.astype(BF16)
    lo = (x2 - hi.astype(F32)).astype(BF16)
    e2 = jnp.concatenate([e, e], axis=0)
    slabs = [_dot(jnp.concatenate([hi[:, c:c + SEG_TILE], lo[:, c:c + SEG_TILE]], axis=1), e2)
             for c in range(0, x2.shape[1], SEG_TILE)]
    return slabs[0] if len(slabs) == 1 else jnp.concatenate(slabs, axis=1)


def _pack_rows(x):
    half = x.shape[1] // 2
    xb = x.astype(BF16).astype(F32)
    lo = pltpu.bitcast(xb[:, :half], jnp.uint32) >> 16
    hi = pltpu.bitcast(xb[:, half:], jnp.uint32) & jnp.uint32(0xFFFF0000)
    return pltpu.bitcast(lo | hi, F32)


def _unpack_rows(words):
    bits = pltpu.bitcast(words, jnp.uint32)
    return jnp.concatenate([pltpu.bitcast(bits << 16, F32), pltpu.bitcast(bits & jnp.uint32(0xFFFF0000), F32)],
                           axis=1).astype(BF16)


def _sigmoid(x):
    return 0.5 * jnp.tanh(0.5 * x) + 0.5


def _row_rms(x, g):
    return x * lax.rsqrt(jnp.mean(x * x, axis=-1, keepdims=True) + EPS) * g


def _proj_kernel(x_ref, gmix_ref, w_ref, wuq_ref, wukv_ref, e64_ref, e128_ref,
                 gqa_ref, gka_ref, gcq_ref, gckv_ref, gqb_ref, gkb_ref, gqc_ref, gkc_ref,
                 cos_ref, sin_ref,
                 qa_o, ka_o, va_o, qb_o, kb_o, vb_o, qc_o, kc_o, vc_o):
    hb = _row_rms(x_ref[0], gmix_ref[...]).astype(BF16)

    projected = _dot(hb, w_ref[...])

    def proj(a, n):
        return projected[:, a:a + n]

    e64 = e64_ref[...]
    e128 = e128_ref[...]

    def seg_norm(v, e, width, g):
        return v * lax.rsqrt(_seg_sum(v * v, e) * (1.0 / width) + EPS) * g

    def store_slots(o_ref, v, n):
        for j in range(n):
            o_ref[0, j] = v[:, LANES * j:LANES * (j + 1)].astype(o_ref.dtype)

    ones_slot = jnp.where(lax.broadcasted_iota(jnp.int32, (hb.shape[0], LANES), 1) == 0, 1.0, 0.0)

    def store_value_slots(o_ref, v, n):
        for j in range(n):
            o_ref[0, j] = jnp.concatenate([v[:, LANES * j:LANES * (j + 1)], ones_slot], axis=1).astype(o_ref.dtype)

    store_slots(qa_o, seg_norm(proj(_QA, 512), e64, A_DIM, gqa_ref[...]), A_HEADS)
    store_slots(ka_o, seg_norm(proj(_KA, 512), e64, A_DIM, gka_ref[...]), A_HEADS)
    store_value_slots(va_o, proj(_VA, 512), A_HEADS)

    cos_t = cos_ref[0]
    sin_t = sin_ref[0]
    cos8 = jnp.concatenate([cos_t] * B_HEADS, axis=1)
    sin8 = jnp.concatenate([sin_t] * B_HEADS, axis=1)
    cq = _row_rms(proj(_CQ, B_Q_LORA), gcq_ref[...]).astype(BF16)
    q2 = _dot(cq, wuq_ref[...])
    qb = q2[:, :1024] * cos8 + q2[:, 1024:] * sin8
    store_slots(qb_o, seg_norm(qb, e128, B_QK, gqb_ref[...]), B_HEADS)
    ckv = _row_rms(proj(_CKV, B_KV_LORA), gckv_ref[...]).astype(BF16)
    kv = _dot(ckv, wukv_ref[...])
    kr = proj(_KR, LANES) * cos_t + proj(_KRR, LANES) * sin_t
    kb = kv[:, :1024] + jnp.concatenate([kr] * B_HEADS, axis=1)
    store_slots(kb_o, seg_norm(kb, e128, B_QK, gkb_ref[...]), B_HEADS)
    store_value_slots(vb_o, kv[:, 1024:], B_HEADS // 2)

    store_slots(qc_o, seg_norm(proj(_QC, 512), e64, C_DIM, gqc_ref[...]), C_HEADS // 2)
    store_slots(kc_o, seg_norm(proj(_KC, 256), e64, C_DIM, gkc_ref[...]), C_KV_HEADS)
    store_value_slots(vc_o, proj(_VC, 256), C_KV_HEADS)


def _proj_call(x, cos_t, sin_t, p, tm):
    b, s, d = x.shape
    full = lambda a: pl.BlockSpec(a.shape, lambda i, j: (0,) * a.ndim, pipeline_mode=pl.Buffered(1))
    slot = lambda nw: pl.BlockSpec((1, nw[0], tm, nw[1]), lambda i, j: (i, 0, j, 0))
    tok = lambda w: pl.BlockSpec((1, tm, w), lambda i, j: (i, j, 0))
    consts = [p["gmix"], p["w_in"], p["w_uq"], p["w_ukv"], p["e64"], p["e128"],
              p["gqa"], p["gka"], p["gcq"], p["gckv"], p["gqb"], p["gkb"], p["gqc"], p["gkc"]]
    slots = [(A_HEADS, LANES), (A_HEADS, LANES), (A_HEADS, 2 * LANES), (B_HEADS, LANES), (B_HEADS, LANES),
             (B_HEADS // 2, 2 * LANES), (C_HEADS // 2, LANES), (C_KV_HEADS, LANES), (C_KV_HEADS, 2 * LANES)]
    return pl.pallas_call(
        _proj_kernel,
        grid=(b, s // tm),
        in_specs=[tok(d)] + [full(a) for a in consts] + [tok(LANES), tok(LANES)],
        out_specs=[slot(nw) for nw in slots],
        out_shape=[jax.ShapeDtypeStruct((b, nw[0], s, nw[1]), BF16) for nw in slots],
        compiler_params=_cparams(("parallel", "parallel")),
        name="proj",
    )(x, *consts, cos_t, sin_t)


def _attn_a_kernel(q_ref, k_ref, v_ref, pc_ref, pr_ref, slope_ref, lam_ref, og_ref, o_ref, s_scr, *, lam_init, mono):
    t = pl.program_id(2)
    q = q_ref[0, 0]
    tq = q.shape[0]
    ck = tq
    nck = k_ref.shape[2] // ck
    lane = lax.broadcasted_iota(jnp.int32, q.shape, 1)
    zero = jnp.zeros_like(q)
    qm = [jnp.where(lane < A_DIM, q, zero), jnp.where(lane >= A_DIM, q, zero)]
    slope = slope_ref[0][:, :1]
    a = slope * pc_ref[0]
    a_lanes = jnp.broadcast_to(a, (tq, LANES))

    def lane_fold_max(x):
        out = x[:, :LANES]
        for j in range(1, ck // LANES):
            out = jnp.maximum(out, x[:, j * LANES:(j + 1) * LANES])
        return out

    mx = [jnp.full((tq, LANES), -jnp.inf, F32)] * 2
    for d in range(nck):
        if mono:
            c = t if d == 0 else lax.rem(t + d, nck)
            k_c = k_ref[0, 0, pl.ds(pl.multiple_of(c * ck, ck), ck), :]
        else:
            c = d
            k_c = k_ref[0, 0, d * ck:(d + 1) * ck, :]
        b = slope * pr_ref[0, c]
        if mono and d > 0:
            sign = jnp.where(t + d < nck, -1.0, 1.0)
            row_part, col_part = sign * b, sign * a_lanes
            for m in range(2):
                sc = lax.dot_general(qm[m], k_c, _NT, preferred_element_type=F32) + row_part
                s_scr[m, c] = sc
                mx[m] = jnp.maximum(mx[m], lane_fold_max(sc) - col_part)
        else:
            bias = jnp.abs(a - b)
            for m in range(2):
                sc = lax.dot_general(qm[m], k_c, _NT, preferred_element_type=F32) - bias
                s_scr[m, c] = sc
                mx[m] = jnp.maximum(mx[m], lane_fold_max(sc))
    v = v_ref[0, 0]
    acc = []
    for m in range(2):
        row_max = jnp.broadcast_to(jnp.max(mx[m], axis=-1, keepdims=True), (tq, LANES))
        es = []
        for c in range(nck):
            if mono:
                sign = jnp.where(c < t, 1.0, jnp.where(c > t, -1.0, 0.0))
                stab = row_max + sign * a_lanes
            else:
                stab = row_max
            es.append(jnp.exp2(s_scr[m, c] - jnp.concatenate([stab] * (ck // LANES), axis=1)).astype(BF16))
        e = es[0] if nck == 1 else jnp.concatenate(es, axis=1)
        acc.append(_dot(e, v))
    lp = lam_ref[...]
    lam = (jnp.exp(jnp.sum(lp[0:1] * lp[1:2], axis=-1, keepdims=True))
           - jnp.exp(jnp.sum(lp[2:3] * lp[3:4], axis=-1, keepdims=True)) + lam_init)
    o = (acc[0][:, :LANES] * (1.0 / acc[0][:, LANES:LANES + 1])
         - acc[1][:, :LANES] * (lam / acc[1][:, LANES:LANES + 1]))
    o_ref[0] = (_row_rms(o, og_ref[...]) * (1.0 - lam_init)).astype(o_ref.dtype)


def _attn_a_call(qa, ka, va, pos_c, pos_r, slopes, lam_p, out_g, lam_init, tq, mono):
    b, h, s, _ = qa.shape
    kv_spec = lambda w: pl.BlockSpec((1, 1, s, w), lambda i, j, t: (i, j, 0, 0))
    return pl.pallas_call(
        functools.partial(_attn_a_kernel, lam_init=lam_init, mono=mono),
        grid=(b, h, s // tq),
        scratch_shapes=[pltpu.VMEM((2, s // tq, tq, tq), F32)],
        in_specs=[pl.BlockSpec((1, 1, tq, LANES), lambda i, j, t: (i, j, t, 0)), kv_spec(LANES), kv_spec(2 * LANES),
                  pl.BlockSpec((1, tq, 1), lambda i, j, t: (i, t, 0)),
                  pl.BlockSpec((1, s // tq, 1, tq), lambda i, j, t: (i, 0, 0, 0)),
                  pl.BlockSpec((1, 1, LANES), lambda i, j, t: (j, 0, 0)),
                  pl.BlockSpec(lam_p.shape, lambda i, j, t: (0, 0)),
                  pl.BlockSpec(out_g.shape, lambda i, j, t: (0, 0))],
        out_specs=pl.BlockSpec((1, tq, LANES), lambda i, j, t: (i, t, j)),
        out_shape=jax.ShapeDtypeStruct((b, s, h * LANES), BF16),
        compiler_params=_cparams(("parallel", "parallel", "arbitrary")),
        name="attn_a",
    )(qa, ka, va, pos_c, pos_r, slopes, lam_p, out_g)


def _attn_b_kernel(q_ref, k_ref, v_ref, o_ref):
    tq = q_ref.shape[2]
    es = []
    for j in range(2):
        s = lax.dot_general(q_ref[0, j], k_ref[0, j], _NT, preferred_element_type=F32)
        es.append(jnp.exp2(s - jnp.max(s, axis=-1, keepdims=True)).astype(BF16))
    acc = _dot(jnp.concatenate(es, axis=0), v_ref[0, 0])
    lane = lax.broadcasted_iota(jnp.int32, (tq, LANES), 1)
    o_ref[0] = jnp.where(lane < B_VDIM, acc[:tq, :LANES] * (1.0 / acc[:tq, LANES:LANES + 1]),
                         acc[tq:, :LANES] * (1.0 / acc[tq:, LANES:LANES + 1])).astype(o_ref.dtype)


def _attn_b_call(qb, kb, vb, tq):
    b, h, s, _ = qb.shape
    return pl.pallas_call(
        _attn_b_kernel,
        grid=(b, h // 2, s // tq),
        in_specs=[pl.BlockSpec((1, 2, tq, LANES), lambda i, j, t: (i, j, t, 0)),
                  pl.BlockSpec((1, 2, s, LANES), lambda i, j, t: (i, j, 0, 0)),
                  pl.BlockSpec((1, 1, s, 2 * LANES), lambda i, j, t: (i, j, 0, 0))],
        out_specs=pl.BlockSpec((1, tq, LANES), lambda i, j, t: (i, t, j)),
        out_shape=jax.ShapeDtypeStruct((b, s, (h // 2) * LANES), BF16),
        compiler_params=_cparams(("parallel", "parallel", "arbitrary")),
        name="attn_b",
    )(qb, kb, vb)


def _attn_c_kernel(q_ref, kp_ref, ko_ref, kn_ref, vp_ref, vo_ref, vn_ref, slope_ref, sink_ref, o_ref, *, seq):
    w = WINDOW
    nsub = ko_ref.shape[2] // w
    n0 = pl.program_id(2) * nsub
    kcat = jnp.concatenate([kp_ref[0, 0], ko_ref[0, 0], kn_ref[0, 0]], axis=0)
    vcat = jnp.concatenate([vp_ref[0, 0], vo_ref[0, 0], vn_ref[0, 0]], axis=0)
    lane = lax.broadcasted_iota(jnp.int32, (w, LANES), 1)
    r_idx = lax.broadcasted_iota(jnp.int32, (w, 3 * w), 0)
    c_idx = lax.broadcasted_iota(jnp.int32, (w, 3 * w), 1)
    arel = jnp.abs(c_idx - w - r_idx)
    dist = arel.astype(F32)
    slopes = slope_ref[0]
    sinks = sink_ref[0]
    bias4 = jnp.concatenate([jnp.where(arel <= w, -slopes[r:r + 1, :1] * dist, -1e30) for r in range(C_REP)], axis=0)
    sink4 = jnp.concatenate([jnp.broadcast_to(sinks[r:r + 1, :1], (w, 1)) for r in range(C_REP)], axis=0)
    c_row = lax.broadcasted_iota(jnp.int32, (1, 3 * w), 1)
    scs = []
    for i in range(nsub):
        parts = []
        for p in range(2):
            q = q_ref[0, p, i * w:(i + 1) * w, :]
            zero = jnp.zeros_like(q)
            parts += [jnp.where(lane < C_DIM, q, zero), jnp.where(lane >= C_DIM, q, zero)]
        qz = jnp.concatenate(parts, axis=0)
        s = lax.dot_general(qz, kcat[i * w:(i + 3) * w], _NT, preferred_element_type=F32)
        kidx = (n0 + i - 1) * w + c_row
        edge = jnp.where(kidx >= 0, jnp.where(kidx < seq, 0.0, -1e30), -1e30)
        scs.append(s + bias4 + edge)
    sc = jnp.concatenate(scs, axis=0)
    sk = jnp.concatenate([sink4] * nsub, axis=0)
    m = jnp.maximum(jnp.max(sc, axis=-1, keepdims=True), sk)
    e = jnp.exp(sc - m).astype(BF16)
    tail = jnp.exp(sk - m)
    for i in range(nsub):
        rows = slice(i * 4 * w, (i + 1) * 4 * w)
        acc = _dot(e[rows], vcat[i * w:(i + 3) * w])
        o = acc[:, :LANES] * (1.0 / (acc[:, LANES:LANES + 1] + tail[rows]))
        pair0 = jnp.where(lane < C_DIM, o[0:w], o[w:2 * w])
        pair1 = jnp.where(lane < C_DIM, o[2 * w:3 * w], o[3 * w:4 * w])
        o_ref[0, i * w:(i + 1) * w, :] = jnp.concatenate([pair0, pair1], axis=1).astype(o_ref.dtype)


def _attn_c_call(qc, kc, vc, slopes, sinks):
    b, _, s, _ = qc.shape
    nb = s // WINDOW
    nsub = min(4, nb)
    tq = nsub * WINDOW
    prev = lambda wd: pl.BlockSpec((1, 1, WINDOW, wd), lambda i, g, n: (i, g, jnp.maximum(n * nsub - 1, 0), 0))
    own = lambda wd: pl.BlockSpec((1, 1, tq, wd), lambda i, g, n: (i, g, n, 0))
    nxt = lambda wd: pl.BlockSpec((1, 1, WINDOW, wd), lambda i, g, n: (i, g, jnp.minimum((n + 1) * nsub, nb - 1), 0))
    kw, vw = kc.shape[-1], vc.shape[-1]
    per_group = pl.BlockSpec((1, C_REP, LANES), lambda i, g, n: (g, 0, 0))
    return pl.pallas_call(
        functools.partial(_attn_c_kernel, seq=s),
        grid=(b, C_KV_HEADS, s // tq),
        in_specs=[pl.BlockSpec((1, 2, tq, LANES), lambda i, g, n: (i, g, n, 0)),
                  prev(kw), own(kw), nxt(kw), prev(vw), own(vw), nxt(vw), per_group, per_group],
        out_specs=pl.BlockSpec((1, tq, 2 * LANES), lambda i, g, n: (i, n, g)),
        out_shape=jax.ShapeDtypeStruct((b, s, C_HEADS * C_DIM), BF16),
        compiler_params=_cparams(("parallel", "parallel", "arbitrary")),
        name="attn_c",
    )(qc, kc, kc, kc, vc, vc, vc, slopes, sinks)


def _merge_kernel(x_ref, gmix_ref, wg_ref, oa_ref, ob_ref, oc_ref, wa_ref, wb_ref, wc_ref, wo_ref,
                  gffn_ref, wr_ref, xn_o, h2_o, aff_o):
    d = D_MODEL
    x = x_ref[0]
    hb = _row_rms(x, gmix_ref[...]).astype(BF16)
    g = _sigmoid(_dot(hb, wg_ref[...]))
    merged = (g[:, :d] * _dot(oa_ref[0], wa_ref[...]) + g[:, d:2 * d] * _dot(ob_ref[0], wb_ref[...])
              + g[:, 2 * d:] * _dot(oc_ref[0], wc_ref[...]))
    xn = x + _dot(merged.astype(BF16), wo_ref[...])
    xn_o[0] = xn
    h2 = _row_rms(xn, gffn_ref[...])
    h2_o[0] = _pack_rows(h2)
    logits = lax.dot_general(wr_ref[...], h2, _NT, preferred_element_type=F32,
                             precision=lax.Precision.HIGHEST)
    ex = jnp.exp(logits - jnp.max(logits, axis=0, keepdims=True))
    aff_o[0] = ex / jnp.sum(ex, axis=0, keepdims=True)


def _merge_call(x, oa, ob, oc, p, tm):
    b, s, d = x.shape
    full = lambda a: pl.BlockSpec(a.shape, lambda i, j: (0,) * a.ndim, pipeline_mode=pl.Buffered(1))
    tok = lambda w: pl.BlockSpec((1, tm, w), lambda i, j: (i, j, 0))
    return pl.pallas_call(
        _merge_kernel,
        grid=(b, s // tm),
        in_specs=[tok(d), full(p["gmix"]), full(p["w_gate"]), tok(512), tok(512), tok(512),
                  full(p["w_a"]), full(p["w_b"]), full(p["w_c"]), full(p["w_o"]), full(p["gffn"]), full(p["w_rt"])],
        out_specs=[tok(d), tok(d // 2), pl.BlockSpec((1, N_EXPERTS, tm), lambda i, j: (i, 0, j))],
        out_shape=[jax.ShapeDtypeStruct((b, s, d), F32), jax.ShapeDtypeStruct((b, s, d // 2), F32),
                   jax.ShapeDtypeStruct((b, N_EXPERTS, s), F32)],
        compiler_params=_cparams(("parallel", "parallel")),
        name="merge",
    )(x, p["gmix"], p["w_gate"], oa, ob, oc, p["w_a"], p["w_b"], p["w_c"], p["w_o"], p["gffn"], p["w_rt"])


def _cumsum_lanes(mask01, chunk):
    rows, s = mask01.shape
    tri = jnp.where(lax.broadcasted_iota(jnp.int32, (chunk, chunk), 0)
                    <= lax.broadcasted_iota(jnp.int32, (chunk, chunk), 1), 1.0, 0.0).astype(BF16)
    carry = jnp.zeros((rows, 1), F32)
    outs = []
    for c in range(s // chunk):
        cs = _dot(mask01[:, c * chunk:(c + 1) * chunk], tri) + carry
        outs.append(cs)
        carry = cs[:, chunk - 1:chunk]
    return jnp.concatenate(outs, axis=1)


def _topk_kernel(aff_ref, pos_o, dest_o, start_o, *, cap, chunk):
    bits = pltpu.bitcast(aff_ref[0], jnp.int32)
    rows = bits.shape[0]
    capf = float(cap)

    def count(mask):
        return jnp.sum(jnp.where(mask, 1.0, 0.0), axis=1, keepdims=True)

    def body(_, c):
        lo, hi = c
        mid = lo + ((hi - lo + 1) >> 1)
        ok = count(bits >= mid) >= capf
        return jnp.where(ok, mid, lo), jnp.where(ok, hi, mid - 1)

    lo0 = jnp.zeros((rows, 1), jnp.int32)
    hi0 = jnp.full((rows, 1), 0x7F800000, jnp.int32)
    thr, _ = lax.fori_loop(0, 31, body, (lo0, hi0))
    gt = bits > thr
    eq = bits == thr
    need = capf - count(gt)
    eq_rank = _cumsum_lanes(jnp.where(eq, 1.0, 0.0).astype(BF16), chunk)
    sel = jnp.where(gt, 1.0, jnp.where(eq, jnp.where(eq_rank <= need, 1.0, 0.0), 0.0))
    sel_b = sel.astype(BF16)
    slot = _cumsum_lanes(sel_b, chunk) - 1.0
    pos_o[0] = jnp.where(sel > 0.0, slot, -1.0).astype(jnp.int32)
    per_token = jnp.broadcast_to(jnp.sum(sel, axis=0, keepdims=True), (8, sel.shape[1]))
    before_token = (_cumsum_lanes(per_token.astype(BF16), chunk) - per_token)[0:1]
    lower = jnp.where(lax.broadcasted_iota(jnp.int32, (rows, rows), 1)
                      < lax.broadcasted_iota(jnp.int32, (rows, rows), 0), 1.0, 0.0).astype(BF16)
    before_expert = _dot(lower, sel_b)
    dest_o[0] = jnp.where(sel > 0.0, before_token + before_expert, -1.0).astype(jnp.int32)
    start_o[0] = before_token.astype(jnp.int32)


def _topk_call(aff_t, cap):
    b, e, s = aff_t.shape
    spec = pl.BlockSpec((1, e, s), lambda i: (i, 0, 0))
    return pl.pallas_call(
        functools.partial(_topk_kernel, cap=cap, chunk=min(512, s)),
        grid=(b,),
        in_specs=[spec],
        out_specs=[spec, spec, pl.BlockSpec((1, 1, s), lambda i: (i, 0, 0))],
        out_shape=[jax.ShapeDtypeStruct((b, e, s), jnp.int32), jax.ShapeDtypeStruct((b, e, s), jnp.int32),
                   jax.ShapeDtypeStruct((b, 1, s), jnp.int32)],
        compiler_params=_cparams(("parallel",)),
        name="topk",
    )(aff_t)


def _sc_gather_call(table, pos2, aff2, cap, batch0):
    npairs, s = pos2.shape
    width = table.shape[1]
    workers = SC_CORES * SC_SUBCORES
    per_worker = npairs // workers
    nchunk = cap // SC_GATHER_ROWS
    assert npairs % workers == 0 and cap % SC_GATHER_ROWS == 0 and s % SC_LANES == 0
    mesh = plsc.VectorSubcoreMesh(core_axis_name="c", subcore_axis_name="s")

    @functools.partial(
        pl.kernel, mesh=mesh,
        out_type=[jax.ShapeDtypeStruct((npairs * cap, width), F32),
                  jax.ShapeDtypeStruct((npairs * cap * SC_LANES,), F32)],
        scratch_types=[pltpu.VMEM((s,), jnp.int32), pltpu.VMEM((s,), F32),
                       pltpu.VMEM((nchunk, SC_GATHER_ROWS), jnp.int32), pltpu.VMEM((cap,), F32),
                       pltpu.VMEM((cap * SC_LANES,), F32), pltpu.VMEM((SC_GATHER_ROWS, width), F32),
                       pltpu.SemaphoreType.DMA],
        compiler_params=pltpu.CompilerParams(needs_layout_passes=False),
        name="sc_gather",
    )
    def gather(table_hbm, pos_hbm, aff_hbm, rows_hbm, asel_hbm, pos_v, aff_v, idx_v, aslot_v, asplat_v, rows_v, sem):
        wid = lax.axis_index("s") * SC_CORES + lax.axis_index("c")
        lanes = lax.iota(jnp.int32, SC_LANES)

        @pl.loop(0, per_worker)
        def _(j):
            pair = wid * per_worker + j
            row0 = (pair // N_EXPERTS + batch0) * s
            pltpu.sync_copy(pos_hbm.at[pair], pos_v)
            pltpu.sync_copy(aff_hbm.at[pair], aff_v)

            @pl.loop(0, s // SC_LANES)
            def _(i):
                sl = pl.ds(i * SC_LANES, SC_LANES)
                slot = pos_v[sl]
                chosen = slot >= 0
                token_row = lanes + (i * SC_LANES + row0)
                plsc.store_scatter(idx_v, [slot >> (SC_GATHER_ROWS.bit_length() - 1), slot & (SC_GATHER_ROWS - 1)], token_row, mask=chosen)
                plsc.store_scatter(aslot_v, [slot], aff_v[sl], mask=chosen)

            @pl.loop(0, cap)
            def _(r):
                asplat_v[pl.ds(r * SC_LANES, SC_LANES)] = plsc.load_gather(
                    aslot_v, [jnp.full((SC_LANES,), r, jnp.int32)])

            pltpu.sync_copy(asplat_v, asel_hbm.at[pl.ds(pair * (cap * SC_LANES), cap * SC_LANES)])
            for c in range(nchunk):
                pltpu.async_copy(table_hbm.at[idx_v.at[c]], rows_v, sem).wait()
                pltpu.sync_copy(rows_v, rows_hbm.at[pl.ds(pair * cap + c * SC_GATHER_ROWS, SC_GATHER_ROWS)])

    return gather(table, pos2, aff2)


def _expert_kernel(xin_ref, asel_ref, wg_ref, wu_ref, wd_ref, y_o, wg_s, wu_s, wd_s):
    @pl.when(pl.program_id(1) == 0)
    def _():
        wg_s[...] = wg_ref[0, 0].astype(BF16)
        wu_s[...] = wu_ref[0, 0].astype(BF16)
        wd_s[...] = wd_ref[0, 0].astype(BF16)

    xin = _unpack_rows(xin_ref[0, 0])
    g = _dot(xin, wg_s[...])
    u = _dot(xin, wu_s[...])
    hid = (g * _sigmoid(g) * u).astype(BF16)
    y_o[0, 0] = _pack_rows(_dot(hid, wd_s[...]) * asel_ref[0, 0][:, :1])


def _expert_call(xin, asel, wg, wu, wd, layer):
    b, e, cap, _ = xin.shape
    d, f = wg.shape[-2:]
    tokens = lambda w: pl.BlockSpec((1, 1, cap, w), lambda j, i: (i, j, 0, 0))
    weight = lambda r, c: pl.BlockSpec((1, 1, r, c), lambda j, i: (layer, j, 0, 0))
    return pl.pallas_call(
        _expert_kernel,
        grid=(e, b),
        in_specs=[tokens(xin.shape[-1]), tokens(asel.shape[-1]), weight(d, f), weight(d, f), weight(f, d)],
        out_specs=tokens(d // 2),
        out_shape=jax.ShapeDtypeStruct((b, e, cap, d // 2), F32),
        scratch_shapes=[pltpu.VMEM((d, f), BF16), pltpu.VMEM((d, f), BF16), pltpu.VMEM((f, d), BF16)],
        compiler_params=_cparams(("parallel", "arbitrary")),
        name="expert",
    )(xin, asel, wg, wu, wd)


def _sc_regroup_call(y_rows, dest2, pos2, cap):
    npairs, s = dest2.shape
    width = y_rows.shape[1]
    workers = SC_CORES * SC_SUBCORES
    batches = npairs // N_EXPERTS
    per_batch = N_EXPERTS * cap
    split = workers // batches
    span = per_batch // split
    nchunk = span // SC_GATHER_ROWS
    assert workers % batches == 0 and per_batch % split == 0 and span % SC_GATHER_ROWS == 0 and s % SC_LANES == 0
    mesh = plsc.VectorSubcoreMesh(core_axis_name="c", subcore_axis_name="s")

    @functools.partial(
        pl.kernel, mesh=mesh,
        out_type=[jax.ShapeDtypeStruct((npairs * cap, width), F32),
                  jax.ShapeDtypeStruct((npairs * cap,), jnp.int32)],
        scratch_types=[pltpu.VMEM((s,), jnp.int32), pltpu.VMEM((s,), jnp.int32),
                       pltpu.VMEM((nchunk, SC_GATHER_ROWS), jnp.int32), pltpu.VMEM((span,), jnp.int32),
                       pltpu.VMEM((SC_GATHER_ROWS, width), F32), pltpu.SemaphoreType.DMA],
        compiler_params=pltpu.CompilerParams(needs_layout_passes=False),
        name="sc_regroup",
    )
    def regroup(y_hbm, dest_hbm, pos_hbm, rows_hbm, tok_hbm, dest_v, pos_v, src_v, tok_v, rows_v, sem):
        wid = lax.axis_index("s") * SC_CORES + lax.axis_index("c")
        batch = wid // split
        first = (wid % split) * span
        lanes = lax.iota(jnp.int32, SC_LANES)

        @pl.loop(0, N_EXPERTS)
        def _(e):
            pair = batch * N_EXPERTS + e
            pltpu.sync_copy(dest_hbm.at[pair], dest_v)
            pltpu.sync_copy(pos_hbm.at[pair], pos_v)

            @pl.loop(0, s // SC_LANES)
            def _(i):
                sl = pl.ds(i * SC_LANES, SC_LANES)
                local = dest_v[sl] - first
                mine = (local >= 0) & (local < span)
                plsc.store_scatter(src_v, [local >> (SC_GATHER_ROWS.bit_length() - 1), local & (SC_GATHER_ROWS - 1)],
                                   pos_v[sl] + pair * cap, mask=mine)
                plsc.store_scatter(tok_v, [local], lanes + i * SC_LANES, mask=mine)

        out0 = batch * per_batch + first
        pltpu.sync_copy(tok_v, tok_hbm.at[pl.ds(out0, span)])
        for c in range(nchunk):
            pltpu.async_copy(y_hbm.at[src_v.at[c]], rows_v, sem).wait()
            pltpu.sync_copy(rows_v, rows_hbm.at[pl.ds(out0 + c * SC_GATHER_ROWS, SC_GATHER_ROWS)])

    return regroup(y_rows, dest2, pos2)


def _combine_kernel(start_ref, x_ref, tok_ref, rows_ref, *rest, ntile, ck):
    o_ref = rest[-1]
    b, j = pl.program_id(0), pl.program_id(1)
    tt = x_ref.shape[1]
    lo = start_ref[b * (ntile + 1) + j]
    hi = start_ref[b * (ntile + 1) + j + 1]
    c_lo = lo // ck
    c_hi = jnp.where(hi > lo, (hi - 1) // ck + 1, c_lo)
    tokens = j * tt + lax.broadcasted_iota(jnp.int32, (tt, ck), 0)
    o_ref[0] = x_ref[0]

    def body(c, carry):
        onehot = jnp.where(tok_ref[0, c] == tokens, 1.0, 0.0).astype(BF16)
        o_ref[0] += _dot(onehot, _unpack_rows(rows_ref[0, c]))
        return carry

    lax.fori_loop(c_lo, c_hi, body, 0)


def _combine_call(xn, starts, tok, rows, tt, ck, batch0, earlier):
    _, s, d = xn.shape
    nb = tok.shape[0]
    nchunk = tok.shape[1] // ck
    ntile = s // tt
    tile = pl.BlockSpec((1, tt, d), lambda i, t, st: (i + batch0, t, 0))
    in_specs = [tile, pl.BlockSpec((1, nchunk, 1, ck), lambda i, t, st: (i, 0, 0, 0)),
                pl.BlockSpec((1, nchunk, ck, rows.shape[-1]), lambda i, t, st: (i, 0, 0, 0))]
    operands = [starts, xn, tok.reshape(nb, nchunk, 1, ck), rows.reshape(nb, nchunk, ck, rows.shape[-1])]
    aliases = {}
    if earlier is not None:
        in_specs.append(pl.BlockSpec(memory_space=pl.ANY))
        operands.append(earlier)
        aliases = {len(operands) - 1: 0}
    return pl.pallas_call(
        functools.partial(_combine_kernel, ntile=ntile, ck=ck),
        grid_spec=pltpu.PrefetchScalarGridSpec(
            num_scalar_prefetch=1, grid=(nb, ntile), in_specs=in_specs, out_specs=tile),
        out_shape=jax.ShapeDtypeStruct(xn.shape, F32),
        input_output_aliases=aliases,
        compiler_params=_cparams(("parallel", "arbitrary")),
        name="combine",
    )(*operands)


def _block_diag(n, blk):
    i = np.arange(n)
    return jnp.asarray((i[:, None] // blk) == (i[None, :] // blk), dtype=BF16)


def _head_slots(w, heads, width):
    r = w.shape[0]
    return jnp.pad(w.reshape(r, heads, width), ((0, 0), (0, 0), (0, LANES - width))).reshape(r, heads * LANES)


def _rotate_half_cols(w):
    half = B_ROPE // 2
    return jnp.concatenate([-w[..., half:], w[..., :half]], axis=-1)


def _layer_params(l, w_in, norm_mix_g, diff_qk_g, mla_cq_g, w_uq, mla_ckv_g, w_ukv, mla_qk_g, swa_qk_g,
                  w_branch_a, w_branch_b, w_branch_c, w_o, norm_ffn_g, w_router):
    d = D_MODEL
    wi = w_in[l]
    off = np.cumsum([0, 512, 512, 512, B_Q_LORA, B_KV_LORA, B_ROPE, 512, 128, 128, 3 * d])
    piece = lambda k: wi[:, off[k]:off[k + 1]]
    maps_to_heads = lambda w: w.reshape(d, 2, A_HEADS, A_DIM).transpose(0, 2, 1, 3).reshape(d, 512)
    dup = lambda w: jnp.concatenate([w.reshape(d, C_KV_HEADS, 1, C_DIM)] * 2, axis=2).reshape(d, 256)
    kr = piece(5)
    rope_slot = lambda w: jnp.pad(w, ((0, 0), (B_NOPE, LANES - B_QK)))
    w_proj = jnp.concatenate(
        [maps_to_heads(piece(0)), maps_to_heads(piece(1)), piece(2), piece(3), piece(4), piece(6),
         dup(piece(7)), dup(piece(8)), rope_slot(kr), rope_slot(_rotate_half_cols(kr))], axis=1).astype(BF16)
    assert w_proj.shape == (d, _PROJ_COLS)

    wq = w_uq[l].reshape(B_Q_LORA, B_HEADS, B_QK)
    wq_rot = jnp.concatenate([jnp.zeros_like(wq[..., :B_NOPE]), _rotate_half_cols(wq[..., B_NOPE:])], axis=-1)
    w_uq_x = jnp.concatenate([_head_slots(wq.reshape(B_Q_LORA, -1), B_HEADS, B_QK),
                              _head_slots(wq_rot.reshape(B_Q_LORA, -1), B_HEADS, B_QK)], axis=1).astype(BF16)
    wkv = w_ukv[l].reshape(B_KV_LORA, B_HEADS, B_NOPE + B_VDIM)
    w_ukv_x = jnp.concatenate([_head_slots(wkv[..., :B_NOPE].reshape(B_KV_LORA, -1), B_HEADS, B_NOPE),
                               wkv[..., B_NOPE:].reshape(B_KV_LORA, -1)], axis=1).astype(BF16)

    row = lambda v: v.reshape(1, -1).astype(F32)
    slot_gain = lambda g: jnp.tile(jnp.pad(g, (0, LANES - B_QK)), B_HEADS)
    return {
        "gmix": row(norm_mix_g[l]), "w_in": w_proj, "w_uq": w_uq_x, "w_ukv": w_ukv_x,
        "e64": _block_diag(SEG_TILE, 64), "e128": _block_diag(SEG_TILE, LANES),
        "gqa": row(jnp.tile(diff_qk_g[l, 0], 8) * (A_DIM ** -0.5 * LOG2E)), "gka": row(jnp.tile(diff_qk_g[l, 1], 8)),
        "gcq": row(mla_cq_g[l]), "gckv": row(mla_ckv_g[l]),
        "gqb": row(slot_gain(mla_qk_g[l, 0]) * (B_QK ** -0.5 * LOG2E)), "gkb": row(slot_gain(mla_qk_g[l, 1])),
        "gqc": row(jnp.tile(swa_qk_g[l, 0], 8) * (C_DIM ** -0.5)), "gkc": row(jnp.tile(swa_qk_g[l, 1], 4)),
        "w_gate": piece(9).astype(BF16),
        "w_a": w_branch_a[l].astype(BF16), "w_b": w_branch_b[l].astype(BF16), "w_c": w_branch_c[l].astype(BF16),
        "w_o": w_o[l].astype(BF16), "gffn": row(norm_ffn_g[l]), "w_rt": w_router[l].T.astype(F32),
    }


def _rope_slot_tables(positions):
    inv = 1.0 / (ROPE_THETA ** (jnp.arange(0, B_ROPE, 2, dtype=F32) / B_ROPE))
    ang = positions.astype(F32)[..., None] * inv
    cos, sin = jnp.cos(ang), jnp.sin(ang)
    ones = jnp.ones(ang.shape[:-1] + (B_NOPE,), F32)
    pad = jnp.zeros(ang.shape[:-1] + (LANES - B_QK,), F32)
    return (jnp.concatenate([ones, cos, cos, pad], axis=-1),
            jnp.concatenate([jnp.zeros_like(ones), sin, sin, pad], axis=-1))


def _alibi_slopes(n):
    return 2.0 ** (-8.0 * jnp.arange(1, n + 1, dtype=F32) / n)


def kernel(x, positions, norm_mix_g, w_in, diff_qk_g, diff_lambda, diff_out_g, mla_cq_g, w_uq, mla_ckv_g, w_ukv,
           mla_qk_g, swa_qk_g, swa_sink, w_branch_a, w_branch_b, w_branch_c, w_o, norm_ffn_g, w_router,
           w_exp_gate, w_exp_up, w_exp_down):
    b, s, d = x.shape
    depth = w_in.shape[0]
    cap = max(1, EC_CAPACITY * s // N_EXPERTS)
    tm_proj = min(512, s)
    tq = min(256, s)
    tq_a = min(512, s)
    tm_merge = min(512, s)
    tt = min(256, s)
    sc_workers = SC_CORES * SC_SUBCORES
    groups = 2 if (b % 2 == 0 and (b // 2) * N_EXPERTS % sc_workers == 0 and sc_workers % (b // 2) == 0) else 1

    cos_t, sin_t = _rope_slot_tables(positions)
    pos_f = positions.astype(F32)
    pos_c, pos_r = pos_f[:, :, None], pos_f.reshape(b, s // tq_a, 1, tq_a)
    monotone = jnp.all(positions[:, 1:] >= positions[:, :-1])
    lane_bcast = lambda v: jnp.broadcast_to(v[..., None], v.shape + (LANES,)).astype(F32)
    slopes_a = lane_bcast(_alibi_slopes(A_HEADS)[:, None] * LOG2E)
    slopes_c = lane_bcast(_alibi_slopes(C_HEADS).reshape(C_KV_HEADS, C_REP))

    for l in range(depth):
        p = _layer_params(l, w_in, norm_mix_g, diff_qk_g, mla_cq_g, w_uq, mla_ckv_g, w_ukv, mla_qk_g, swa_qk_g,
                          w_branch_a, w_branch_b, w_branch_c, w_o, norm_ffn_g, w_router)
        qa, ka, va, qb, kb, vb, qc, kc, vc = _proj_call(x, cos_t, sin_t, p, tm_proj)
        lam_init = 0.8 - 0.6 * math.exp(-0.3 * l)
        attn_a = functools.partial(_attn_a_call, qa, ka, va, pos_c, pos_r, slopes_a, diff_lambda[l].astype(F32),
                                   diff_out_g[l].reshape(1, -1).astype(F32), lam_init, tq_a)
        oa = lax.cond(monotone, functools.partial(attn_a, True), functools.partial(attn_a, False))
        ob = _attn_b_call(qb, kb, vb, min(512, s))
        oc = _attn_c_call(qc, kc, vc, slopes_c, lane_bcast(swa_sink[l].reshape(C_KV_HEADS, C_REP)))
        xn, h2, aff_t = _merge_call(x, oa, ob, oc, p, tm_merge)
        pos, dest, start = _topk_call(aff_t, cap)
        starts = jnp.concatenate([start[:, 0, ::tt], jnp.full((b, 1), N_EXPERTS * cap, jnp.int32)], axis=1)
        table = h2.reshape(b * s, d // 2)
        x = None
        for grp in range(groups):
            b0, nb = grp * (b // groups), b // groups
            sl = slice(b0, b0 + nb)
            pos2 = pos[sl].reshape(nb * N_EXPERTS, s)
            rows, asel = _sc_gather_call(table, pos2, aff_t[sl].reshape(nb * N_EXPERTS, s), cap, b0)
            y = _expert_call(rows.reshape(nb, N_EXPERTS, cap, d // 2), asel.reshape(nb, N_EXPERTS, cap, SC_LANES),
                             w_exp_gate, w_exp_up, w_exp_down, l)
            y_rows, y_tok = _sc_regroup_call(y.reshape(nb * N_EXPERTS * cap, d // 2),
                                             dest[sl].reshape(nb * N_EXPERTS, s), pos2, cap)
            x = _combine_call(xn, starts[sl].reshape(-1), y_tok.reshape(nb, N_EXPERTS * cap),
                              y_rows.reshape(nb, N_EXPERTS * cap, d // 2), tt, min(256, N_EXPERTS * cap), b0, x)
    return x
```

```python
import functools
import math

import numpy as np
import jax
import jax.numpy as jnp
from jax import lax
from jax.experimental import pallas as pl
from jax.experimental.pallas import tpu as pltpu
from jax.experimental.pallas import tpu_sc as plsc

F32 = jnp.float32
BF16 = jnp.bfloat16

D_MODEL = 1024
EPS = 1e-6
A_HEADS = 4
A_DIM = 64
A_VDIM = 128
B_HEADS = 8
B_NOPE = 64
B_ROPE = 32
B_VDIM = 64
B_QK = B_NOPE + B_ROPE
B_Q_LORA = 384
B_KV_LORA = 256
ROPE_THETA = 10000.0
C_HEADS = 8
C_KV_HEADS = 2
C_REP = C_HEADS // C_KV_HEADS
C_DIM = 64
WINDOW = 128
N_EXPERTS = 16
EC_CAPACITY = 2
D_FF = 1024
LOG2E = math.log2(math.e)
SC_CORES, SC_SUBCORES, SC_LANES = 2, 16, 16
SC_GATHER_ROWS = 128
SEG_TILE = 256
LANES = 128

_QA = 0
_KA = 512
_VA = 1024
_CQ = 1536
_CKV = 1920
_QC = 2176
_KC = 2688
_VC = 2944
_KR = 3200
_KRR = 3328
_PROJ_COLS = 3456

VMEM_LIMIT = 56 * 1024 * 1024

_NT = (((1,), (1,)), ((), ()))
_TN = (((0,), (0,)), ((), ()))


def _cparams(sem):
    return pltpu.CompilerParams(dimension_semantics=sem, vmem_limit_bytes=VMEM_LIMIT)


def _dot(a, b):
    return jnp.dot(a, b, preferred_element_type=F32)


def _seg_sum(x2, e):
    hi = x2.astype(BF16)
    lo = (x2 - hi.astype(F32)).astype(BF16)
    e2 = jnp.concatenate([e, e], axis=0)
    slabs = [_dot(jnp.concatenate([hi[:, c:c + SEG_TILE], lo[:, c:c + SEG_TILE]], axis=1), e2)
             for c in range(0, x2.shape[1], SEG_TILE)]
    return slabs[0] if len(slabs) == 1 else jnp.concatenate(slabs, axis=1)


def _pack_rows(x):
    half = x.shape[1] // 2
    xb = x.astype(BF16).astype(F32)
    lo = pltpu.bitcast(xb[:, :half], jnp.uint32) >> 16
    hi = pltpu.bitcast(xb[:, half:], jnp.uint32) & jnp.uint32(0xFFFF0000)
    return pltpu.bitcast(lo | hi, F32)


def _unpack_rows(words):
    bits = pltpu.bitcast(words, jnp.uint32)
    return jnp.concatenate([pltpu.bitcast(bits << 16, F32), pltpu.bitcast(bits & jnp.uint32(0xFFFF0000), F32)],
                           axis=1).astype(BF16)


def _sigmoid(x):
    return 0.5 * jnp.tanh(0.5 * x) + 0.5


def _row_rms(x, g):
    return x * lax.rsqrt(jnp.mean(x * x, axis=-1, keepdims=True) + EPS) * g


def _proj_kernel(x_ref, gmix_ref, w_ref, wuq_ref, wukv_ref, e64_ref, e128_ref,
                 gqa_ref, gka_ref, gcq_ref, gckv_ref, gqb_ref, gkb_ref, gqc_ref, gkc_ref,
                 cos_ref, sin_ref,
                 qa_o, ka_o, va_o, qb_o, kb_o, vb_o, qc_o, kc_o, vc_o):
    hb = _row_rms(x_ref[0], gmix_ref[...]).astype(BF16)

    projected = _dot(hb, w_ref[...])

    def proj(a, n):
        return projected[:, a:a + n]

    e64 = e64_ref[...]
    e128 = e128_ref[...]

    def seg_norm(v, e, width, g):
        return v * lax.rsqrt(_seg_sum(v * v, e) * (1.0 / width) + EPS) * g

    def store_slots(o_ref, v, n):
        for j in range(n):
            o_ref[0, j] = v[:, LANES * j:LANES * (j + 1)].astype(o_ref.dtype)

    ones_slot = jnp.ones((hb.shape[0], LANES), F32)

    def store_value_slots(o_ref, v, n):
        for j in range(n):
            o_ref[0, j] = jnp.concatenate([v[:, LANES * j:LANES * (j + 1)], ones_slot], axis=1).astype(o_ref.dtype)

    store_slots(qa_o, seg_norm(proj(_QA, 512), e64, A_DIM, gqa_ref[...]), A_HEADS)
    store_slots(ka_o, seg_norm(proj(_KA, 512), e64, A_DIM, gka_ref[...]), A_HEADS)
    store_value_slots(va_o, proj(_VA, 512), A_HEADS)

    cos_t = cos_ref[0]
    sin_t = sin_ref[0]
    cos8 = jnp.concatenate([cos_t] * B_HEADS, axis=1)
    sin8 = jnp.concatenate([sin_t] * B_HEADS, axis=1)
    cq = _row_rms(proj(_CQ, B_Q_LORA), gcq_ref[...]).astype(BF16)
    q2 = _dot(cq, wuq_ref[...])
    qb = q2[:, :1024] * cos8 + q2[:, 1024:] * sin8
    store_slots(qb_o, seg_norm(qb, e128, B_QK, gqb_ref[...]), B_HEADS)
    ckv = _row_rms(proj(_CKV, B_KV_LORA), gckv_ref[...]).astype(BF16)
    kv = _dot(ckv, wukv_ref[...])
    kr = proj(_KR, LANES) * cos_t + proj(_KRR, LANES) * sin_t
    kb = kv[:, :1024] + jnp.concatenate([kr] * B_HEADS, axis=1)
    store_slots(kb_o, seg_norm(kb, e128, B_QK, gkb_ref[...]), B_HEADS)
    store_value_slots(vb_o, kv[:, 1024:], B_HEADS // 2)

    store_slots(qc_o, seg_norm(proj(_QC, 512), e64, C_DIM, gqc_ref[...]), C_HEADS // 2)
    store_slots(kc_o, seg_norm(proj(_KC, 256), e64, C_DIM, gkc_ref[...]), C_KV_HEADS)
    store_value_slots(vc_o, proj(_VC, 256), C_KV_HEADS)


def _proj_call(x, cos_t, sin_t, p, tm):
    b, s, d = x.shape
    full = lambda a: pl.BlockSpec(a.shape, lambda i, j: (0,) * a.ndim, pipeline_mode=pl.Buffered(1))
    slot = lambda nw: pl.BlockSpec((1, nw[0], tm, nw[1]), lambda i, j: (i, 0, j, 0))
    tok = lambda w: pl.BlockSpec((1, tm, w), lambda i, j: (i, j, 0))
    consts = [p["gmix"], p["w_in"], p["w_uq"], p["w_ukv"], p["e64"], p["e128"],
              p["gqa"], p["gka"], p["gcq"], p["gckv"], p["gqb"], p["gkb"], p["gqc"], p["gkc"]]
    slots = [(A_HEADS, LANES), (A_HEADS, LANES), (A_HEADS, 2 * LANES), (B_HEADS, LANES), (B_HEADS, LANES),
             (B_HEADS // 2, 2 * LANES), (C_HEADS // 2, LANES), (C_KV_HEADS, LANES), (C_KV_HEADS, 2 * LANES)]
    return pl.pallas_call(
        _proj_kernel,
        grid=(b, s // tm),
        in_specs=[tok(d)] + [full(a) for a in consts] + [tok(LANES), tok(LANES)],
        out_specs=[slot(nw) for nw in slots],
        out_shape=[jax.ShapeDtypeStruct((b, nw[0], s, nw[1]), BF16) for nw in slots],
        compiler_params=_cparams(("parallel", "parallel")),
        name="proj",
    )(x, *consts, cos_t, sin_t)


def _attn_a_kernel(q_ref, k_ref, v_ref, pc_ref, pr_ref, slope_ref, lam_ref, og_ref, o_ref, s_scr, *, lam_init, mono):
    t = pl.program_id(2)
    q = q_ref[0, 0]
    tq = q.shape[0]
    ck = tq
    nck = k_ref.shape[2] // ck
    lane = lax.broadcasted_iota(jnp.int32, q.shape, 1)
    zero = jnp.zeros_like(q)
    qm = [jnp.where(lane < A_DIM, q, zero), jnp.where(lane >= A_DIM, q, zero)]
    slope = slope_ref[0][:, :1]
    a = slope * pc_ref[0]
    a_lanes = jnp.broadcast_to(a, (tq, LANES))

    def lane_fold_max(x):
        out = x[:, :LANES]
        for j in range(1, ck // LANES):
            out = jnp.maximum(out, x[:, j * LANES:(j + 1) * LANES])
        return out

    mx = [jnp.full((tq, LANES), -jnp.inf, F32)] * 2
    for d in range(nck):
        if mono:
            c = t if d == 0 else lax.rem(t + d, nck)
            k_c = k_ref[0, 0, pl.ds(pl.multiple_of(c * ck, ck), ck), :]
        else:
            c = d
            k_c = k_ref[0, 0, d * ck:(d + 1) * ck, :]
        b = slope * pr_ref[0, c]
        if mono and d > 0:
            sign = jnp.where(t + d < nck, -1.0, 1.0)
            row_part, col_part = sign * b, sign * a_lanes
            for m in range(2):
                sc = lax.dot_general(qm[m], k_c, _NT, preferred_element_type=F32) + row_part
                s_scr[m, c] = sc
                mx[m] = jnp.maximum(mx[m], lane_fold_max(sc) - col_part)
        else:
            bias = jnp.abs(a - b)
            for m in range(2):
                sc = lax.dot_general(qm[m], k_c, _NT, preferred_element_type=F32) - bias
                s_scr[m, c] = sc
                mx[m] = jnp.maximum(mx[m], lane_fold_max(sc))
    v = v_ref[0, 0]
    acc = []
    for m in range(2):
        row_max = jnp.broadcast_to(jnp.max(mx[m], axis=-1, keepdims=True), (tq, LANES))
        es = []
        for c in range(nck):
            if mono:
                sign = jnp.where(c < t, 1.0, jnp.where(c > t, -1.0, 0.0))
                stab = row_max + sign * a_lanes
            else:
                stab = row_max
            es.append(jnp.exp2(s_scr[m, c] - jnp.concatenate([stab] * (ck // LANES), axis=1)).astype(BF16))
        e = es[0] if nck == 1 else jnp.concatenate(es, axis=1)
        acc.append(_dot(e, v))
    lp = lam_ref[...]
    lam = (jnp.exp(jnp.sum(lp[0:1] * lp[1:2], axis=-1, keepdims=True))
           - jnp.exp(jnp.sum(lp[2:3] * lp[3:4], axis=-1, keepdims=True)) + lam_init)
    o = (acc[0][:, :LANES] * (1.0 / acc[0][:, LANES:])
         - acc[1][:, :LANES] * (lam / acc[1][:, LANES:]))
    o_ref[0] = (_row_rms(o, og_ref[...]) * (1.0 - lam_init)).astype(o_ref.dtype)


def _attn_a_call(qa, ka, va, pos_c, pos_r, slopes, lam_p, out_g, lam_init, tq, mono):
    b, h, s, _ = qa.shape
    kv_spec = lambda w: pl.BlockSpec((1, 1, s, w), lambda i, j, t: (i, j, 0, 0))
    return pl.pallas_call(
        functools.partial(_attn_a_kernel, lam_init=lam_init, mono=mono),
        grid=(b, h, s // tq),
        scratch_shapes=[pltpu.VMEM((2, s // tq, tq, tq), F32)],
        in_specs=[pl.BlockSpec((1, 1, tq, LANES), lambda i, j, t: (i, j, t, 0)), kv_spec(LANES), kv_spec(2 * LANES),
                  pl.BlockSpec((1, tq, 1), lambda i, j, t: (i, t, 0)),
                  pl.BlockSpec((1, s // tq, 1, tq), lambda i, j, t: (i, 0, 0, 0)),
                  pl.BlockSpec((1, 1, LANES), lambda i, j, t: (j, 0, 0)),
                  pl.BlockSpec(lam_p.shape, lambda i, j, t: (0, 0)),
                  pl.BlockSpec(out_g.shape, lambda i, j, t: (0, 0))],
        out_specs=pl.BlockSpec((1, tq, LANES), lambda i, j, t: (i, t, j)),
        out_shape=jax.ShapeDtypeStruct((b, s, h * LANES), BF16),
        compiler_params=_cparams(("parallel", "parallel", "arbitrary")),
        name="attn_a",
    )(qa, ka, va, pos_c, pos_r, slopes, lam_p, out_g)


def _attn_b_kernel(q_ref, k_ref, v_ref, o_ref):
    tq = q_ref.shape[2]
    es = []
    for j in range(2):
        s = lax.dot_general(q_ref[0, j], k_ref[0, j], _NT, preferred_element_type=F32)
        es.append(jnp.exp2(s - jnp.max(s, axis=-1, keepdims=True)).astype(BF16))
    acc = _dot(jnp.concatenate(es, axis=0), v_ref[0, 0])
    lane = lax.broadcasted_iota(jnp.int32, (tq, LANES), 1)
    o_ref[0] = jnp.where(lane < B_VDIM, acc[:tq, :LANES] * (1.0 / acc[:tq, LANES:]),
                         acc[tq:, :LANES] * (1.0 / acc[tq:, LANES:])).astype(o_ref.dtype)


def _attn_b_call(qb, kb, vb, tq):
    b, h, s, _ = qb.shape
    return pl.pallas_call(
        _attn_b_kernel,
        grid=(b, h // 2, s // tq),
        in_specs=[pl.BlockSpec((1, 2, tq, LANES), lambda i, j, t: (i, j, t, 0)),
                  pl.BlockSpec((1, 2, s, LANES), lambda i, j, t: (i, j, 0, 0)),
                  pl.BlockSpec((1, 1, s, 2 * LANES), lambda i, j, t: (i, j, 0, 0))],
        out_specs=pl.BlockSpec((1, tq, LANES), lambda i, j, t: (i, t, j)),
        out_shape=jax.ShapeDtypeStruct((b, s, (h // 2) * LANES), BF16),
        compiler_params=_cparams(("parallel", "parallel", "arbitrary")),
        name="attn_b",
    )(qb, kb, vb)


def _attn_c_kernel(q_ref, kp_ref, ko_ref, kn_ref, vp_ref, vo_ref, vn_ref, slope_ref, sink_ref, o_ref, *, seq):
    w = WINDOW
    nsub = ko_ref.shape[2] // w
    n0 = pl.program_id(2) * nsub
    kcat = jnp.concatenate([kp_ref[0, 0], ko_ref[0, 0], kn_ref[0, 0]], axis=0)
    vcat = jnp.concatenate([vp_ref[0, 0], vo_ref[0, 0], vn_ref[0, 0]], axis=0)
    lane = lax.broadcasted_iota(jnp.int32, (w, LANES), 1)
    r_idx = lax.broadcasted_iota(jnp.int32, (w, 3 * w), 0)
    c_idx = lax.broadcasted_iota(jnp.int32, (w, 3 * w), 1)
    arel = jnp.abs(c_idx - w - r_idx)
    dist = arel.astype(F32)
    slopes = slope_ref[0]
    sinks = sink_ref[0]
    bias4 = jnp.concatenate([jnp.where(arel <= w, -slopes[r:r + 1, :1] * dist, -1e30) for r in range(C_REP)], axis=0)
    sink4 = jnp.concatenate([jnp.broadcast_to(sinks[r:r + 1, :], (w, LANES)) for r in range(C_REP)], axis=0)
    c_row = lax.broadcasted_iota(jnp.int32, (1, 3 * w), 1)
    scs = []
    for i in range(nsub):
        parts = []
        for p in range(2):
            q = q_ref[0, p, i * w:(i + 1) * w, :]
            zero = jnp.zeros_like(q)
            parts += [jnp.where(lane < C_DIM, q, zero), jnp.where(lane >= C_DIM, q, zero)]
        qz = jnp.concatenate(parts, axis=0)
        s = lax.dot_general(qz, kcat[i * w:(i + 3) * w], _NT, preferred_element_type=F32)
        kidx = (n0 + i - 1) * w + c_row
        edge = jnp.where(kidx >= 0, jnp.where(kidx < seq, 0.0, -1e30), -1e30)
        scs.append(s + bias4 + edge)
    sc = jnp.concatenate(scs, axis=0)
    sk = jnp.concatenate([sink4] * nsub, axis=0)
    m = jnp.maximum(jnp.broadcast_to(jnp.max(sc, axis=-1, keepdims=True), sk.shape), sk)
    e = jnp.exp(sc - jnp.concatenate([m] * 3, axis=1)).astype(BF16)
    tail = jnp.exp(sk - m)
    for i in range(nsub):
        rows = slice(i * 4 * w, (i + 1) * 4 * w)
        acc = _dot(e[rows], vcat[i * w:(i + 3) * w])
        o = acc[:, :LANES] * (1.0 / (acc[:, LANES:] + tail[rows]))
        pair0 = jnp.where(lane < C_DIM, o[0:w], o[w:2 * w])
        pair1 = jnp.where(lane < C_DIM, o[2 * w:3 * w], o[3 * w:4 * w])
        o_ref[0, i * w:(i + 1) * w, :] = jnp.concatenate([pair0, pair1], axis=1).astype(o_ref.dtype)


def _attn_c_call(qc, kc, vc, slopes, sinks):
    b, _, s, _ = qc.shape
    nb = s // WINDOW
    nsub = min(4, nb)
    tq = nsub * WINDOW
    prev = lambda wd: pl.BlockSpec((1, 1, WINDOW, wd), lambda i, g, n: (i, g, jnp.maximum(n * nsub - 1, 0), 0))
    own = lambda wd: pl.BlockSpec((1, 1, tq, wd), lambda i, g, n: (i, g, n, 0))
    nxt = lambda wd: pl.BlockSpec((1, 1, WINDOW, wd), lambda i, g, n: (i, g, jnp.minimum((n + 1) * nsub, nb - 1), 0))
    kw, vw = kc.shape[-1], vc.shape[-1]
    per_group = pl.BlockSpec((1, C_REP, LANES), lambda i, g, n: (g, 0, 0))
    return pl.pallas_call(
        functools.partial(_attn_c_kernel, seq=s),
        grid=(b, C_KV_HEADS, s // tq),
        in_specs=[pl.BlockSpec((1, 2, tq, LANES), lambda i, g, n: (i, g, n, 0)),
                  prev(kw), own(kw), nxt(kw), prev(vw), own(vw), nxt(vw), per_group, per_group],
        out_specs=pl.BlockSpec((1, tq, 2 * LANES), lambda i, g, n: (i, n, g)),
        out_shape=jax.ShapeDtypeStruct((b, s, C_HEADS * C_DIM), BF16),
        compiler_params=_cparams(("parallel", "parallel", "arbitrary")),
        name="attn_c",
    )(qc, kc, kc, kc, vc, vc, vc, slopes, sinks)


def _merge_kernel(x_ref, gmix_ref, wg_ref, oa_ref, ob_ref, oc_ref, wa_ref, wb_ref, wc_ref, wo_ref,
                  gffn_ref, wr_ref, xn_o, h2_o, aff_o):
    d = D_MODEL
    x = x_ref[0]
    hb = _row_rms(x, gmix_ref[...]).astype(BF16)
    g = _sigmoid(_dot(hb, wg_ref[...]))
    merged = (g[:, :d] * _dot(oa_ref[0], wa_ref[...]) + g[:, d:2 * d] * _dot(ob_ref[0], wb_ref[...])
              + g[:, 2 * d:] * _dot(oc_ref[0], wc_ref[...]))
    xn = x + _dot(merged.astype(BF16), wo_ref[...])
    xn_o[0] = xn
    h2 = _row_rms(xn, gffn_ref[...])
    h2_o[0] = _pack_rows(h2)
    logits = lax.dot_general(wr_ref[...], h2, _NT, preferred_element_type=F32,
                             precision=lax.Precision.HIGHEST)
    ex = jnp.exp(logits - jnp.max(logits, axis=0, keepdims=True))
    aff_o[0] = ex / jnp.sum(ex, axis=0, keepdims=True)


def _merge_call(x, oa, ob, oc, p, tm):
    b, s, d = x.shape
    full = lambda a: pl.BlockSpec(a.shape, lambda i, j: (0,) * a.ndim, pipeline_mode=pl.Buffered(1))
    tok = lambda w: pl.BlockSpec((1, tm, w), lambda i, j: (i, j, 0))
    return pl.pallas_call(
        _merge_kernel,
        grid=(b, s // tm),
        in_specs=[tok(d), full(p["gmix"]), full(p["w_gate"]), tok(512), tok(512), tok(512),
                  full(p["w_a"]), full(p["w_b"]), full(p["w_c"]), full(p["w_o"]), full(p["gffn"]), full(p["w_rt"])],
        out_specs=[tok(d), tok(d // 2), pl.BlockSpec((1, N_EXPERTS, tm), lambda i, j: (i, 0, j))],
        out_shape=[jax.ShapeDtypeStruct((b, s, d), F32), jax.ShapeDtypeStruct((b, s, d // 2), F32),
                   jax.ShapeDtypeStruct((b, N_EXPERTS, s), F32)],
        compiler_params=_cparams(("parallel", "parallel")),
        name="merge",
    )(x, p["gmix"], p["w_gate"], oa, ob, oc, p["w_a"], p["w_b"], p["w_c"], p["w_o"], p["gffn"], p["w_rt"])


def _cumsum_lanes(mask01, chunk):
    rows, s = mask01.shape
    tri = jnp.where(lax.broadcasted_iota(jnp.int32, (chunk, chunk), 0)
                    <= lax.broadcasted_iota(jnp.int32, (chunk, chunk), 1), 1.0, 0.0).astype(BF16)
    carry = jnp.zeros((rows, 1), F32)
    outs = []
    for c in range(s // chunk):
        cs = _dot(mask01[:, c * chunk:(c + 1) * chunk], tri) + carry
        outs.append(cs)
        carry = cs[:, chunk - 1:chunk]
    return jnp.concatenate(outs, axis=1)


def _topk_kernel(aff_ref, pos_o, dest_o, start_o, *, cap, chunk):
    bits = pltpu.bitcast(aff_ref[0], jnp.int32)
    rows = bits.shape[0]
    capf = float(cap)

    def count(mask):
        return jnp.sum(jnp.where(mask, 1.0, 0.0), axis=1, keepdims=True)

    def body(_, c):
        lo, hi = c
        mid = lo + ((hi - lo + 1) >> 1)
        ok = count(bits >= mid) >= capf
        return jnp.where(ok, mid, lo), jnp.where(ok, hi, mid - 1)

    lo0 = jnp.zeros((rows, 1), jnp.int32)
    hi0 = jnp.full((rows, 1), 0x7F800000, jnp.int32)
    thr, _ = lax.fori_loop(0, 31, body, (lo0, hi0))
    gt = bits > thr
    eq = bits == thr
    need = capf - count(gt)
    eq_rank = _cumsum_lanes(jnp.where(eq, 1.0, 0.0).astype(BF16), chunk)
    sel = jnp.where(gt, 1.0, jnp.where(eq, jnp.where(eq_rank <= need, 1.0, 0.0), 0.0))
    sel_b = sel.astype(BF16)
    slot = _cumsum_lanes(sel_b, chunk) - 1.0
    pos_o[0] = jnp.where(sel > 0.0, slot, -1.0).astype(jnp.int32)
    per_token = jnp.broadcast_to(jnp.sum(sel, axis=0, keepdims=True), (8, sel.shape[1]))
    before_token = (_cumsum_lanes(per_token.astype(BF16), chunk) - per_token)[0:1]
    lower = jnp.where(lax.broadcasted_iota(jnp.int32, (rows, rows), 1)
                      < lax.broadcasted_iota(jnp.int32, (rows, rows), 0), 1.0, 0.0).astype(BF16)
    before_expert = _dot(lower, sel_b)
    dest_o[0] = jnp.where(sel > 0.0, before_token + before_expert, -1.0).astype(jnp.int32)
    start_o[0] = before_token.astype(jnp.int32)


def _topk_call(aff_t, cap):
    b, e, s = aff_t.shape
    spec = pl.BlockSpec((1, e, s), lambda i: (i, 0, 0))
    return pl.pallas_call(
        functools.partial(_topk_kernel, cap=cap, chunk=min(512, s)),
        grid=(b,),
        in_specs=[spec],
        out_specs=[spec, spec, pl.BlockSpec((1, 1, s), lambda i: (i, 0, 0))],
        out_shape=[jax.ShapeDtypeStruct((b, e, s), jnp.int32), jax.ShapeDtypeStruct((b, e, s), jnp.int32),
                   jax.ShapeDtypeStruct((b, 1, s), jnp.int32)],
        compiler_params=_cparams(("parallel",)),
        name="topk",
    )(aff_t)


def _sc_gather_call(table, pos2, aff2, cap, batch0):
    npairs, s = pos2.shape
    width = table.shape[1]
    workers = SC_CORES * SC_SUBCORES
    per_worker = npairs // workers
    nchunk = cap // SC_GATHER_ROWS
    assert npairs % workers == 0 and cap % SC_GATHER_ROWS == 0 and s % SC_LANES == 0
    mesh = plsc.VectorSubcoreMesh(core_axis_name="c", subcore_axis_name="s")

    @functools.partial(
        pl.kernel, mesh=mesh,
        out_type=[jax.ShapeDtypeStruct((npairs * cap, width), F32),
                  jax.ShapeDtypeStruct((npairs * cap * SC_LANES,), F32)],
        scratch_types=[pltpu.VMEM((s,), jnp.int32), pltpu.VMEM((s,), F32),
                       pltpu.VMEM((nchunk, SC_GATHER_ROWS), jnp.int32), pltpu.VMEM((cap,), F32),
                       pltpu.VMEM((cap * SC_LANES,), F32), pltpu.VMEM((SC_GATHER_ROWS, width), F32),
                       pltpu.SemaphoreType.DMA],
        compiler_params=pltpu.CompilerParams(needs_layout_passes=False),
        name="sc_gather",
    )
    def gather(table_hbm, pos_hbm, aff_hbm, rows_hbm, asel_hbm, pos_v, aff_v, idx_v, aslot_v, asplat_v, rows_v, sem):
        wid = lax.axis_index("s") * SC_CORES + lax.axis_index("c")
        lanes = lax.iota(jnp.int32, SC_LANES)

        @pl.loop(0, per_worker)
        def _(j):
            pair = wid * per_worker + j
            row0 = (pair // N_EXPERTS + batch0) * s
            pltpu.sync_copy(pos_hbm.at[pair], pos_v)
            pltpu.sync_copy(aff_hbm.at[pair], aff_v)

            @pl.loop(0, s // SC_LANES)
            def _(i):
                sl = pl.ds(i * SC_LANES, SC_LANES)
                slot = pos_v[sl]
                chosen = slot >= 0
                token_row = lanes + (i * SC_LANES + row0)
                plsc.store_scatter(idx_v, [slot >> (SC_GATHER_ROWS.bit_length() - 1), slot & (SC_GATHER_ROWS - 1)], token_row, mask=chosen)
                plsc.store_scatter(aslot_v, [slot], aff_v[sl], mask=chosen)

            @pl.loop(0, cap)
            def _(r):
                asplat_v[pl.ds(r * SC_LANES, SC_LANES)] = plsc.load_gather(
                    aslot_v, [jnp.full((SC_LANES,), r, jnp.int32)])

            pltpu.sync_copy(asplat_v, asel_hbm.at[pl.ds(pair * (cap * SC_LANES), cap * SC_LANES)])
            for c in range(nchunk):
                pltpu.async_copy(table_hbm.at[idx_v.at[c]], rows_v, sem).wait()
                pltpu.sync_copy(rows_v, rows_hbm.at[pl.ds(pair * cap + c * SC_GATHER_ROWS, SC_GATHER_ROWS)])

    return gather(table, pos2, aff2)


def _expert_kernel(xin_ref, asel_ref, wg_ref, wu_ref, wd_ref, y_o, wg_s, wu_s, wd_s):
    @pl.when(pl.program_id(1) == 0)
    def _():
        wg_s[...] = wg_ref[0, 0].astype(BF16)
        wu_s[...] = wu_ref[0, 0].astype(BF16)
        wd_s[...] = wd_ref[0, 0].astype(BF16)

    xin = _unpack_rows(xin_ref[0, 0])
    g = _dot(xin, wg_s[...])
    u = _dot(xin, wu_s[...])
    hid = (g * _sigmoid(g) * u).astype(BF16)
    y_o[0, 0] = _pack_rows(_dot(hid, wd_s[...]) * asel_ref[0, 0][:, :1])


def _expert_call(xin, asel, wg, wu, wd, layer):
    b, e, cap, _ = xin.shape
    d, f = wg.shape[-2:]
    tokens = lambda w: pl.BlockSpec((1, 1, cap, w), lambda j, i: (i, j, 0, 0))
    weight = lambda r, c: pl.BlockSpec((1, 1, r, c), lambda j, i: (layer, j, 0, 0))
    return pl.pallas_call(
        _expert_kernel,
        grid=(e, b),
        in_specs=[tokens(xin.shape[-1]), tokens(asel.shape[-1]), weight(d, f), weight(d, f), weight(f, d)],
        out_specs=tokens(d // 2),
        out_shape=jax.ShapeDtypeStruct((b, e, cap, d // 2), F32),
        scratch_shapes=[pltpu.VMEM((d, f), BF16), pltpu.VMEM((d, f), BF16), pltpu.VMEM((f, d), BF16)],
        compiler_params=_cparams(("parallel", "arbitrary")),
        name="expert",
    )(xin, asel, wg, wu, wd)


def _sc_regroup_call(y_rows, dest2, pos2, cap):
    npairs, s = dest2.shape
    width = y_rows.shape[1]
    workers = SC_CORES * SC_SUBCORES
    batches = npairs // N_EXPERTS
    per_batch = N_EXPERTS * cap
    split = workers // batches
    span = per_batch // split
    nchunk = span // SC_GATHER_ROWS
    assert workers % batches == 0 and per_batch % split == 0 and span % SC_GATHER_ROWS == 0 and s % SC_LANES == 0
    mesh = plsc.VectorSubcoreMesh(core_axis_name="c", subcore_axis_name="s")

    @functools.partial(
        pl.kernel, mesh=mesh,
        out_type=[jax.ShapeDtypeStruct((npairs * cap, width), F32),
                  jax.ShapeDtypeStruct((npairs * cap,), jnp.int32)],
        scratch_types=[pltpu.VMEM((s,), jnp.int32), pltpu.VMEM((s,), jnp.int32),
                       pltpu.VMEM((nchunk, SC_GATHER_ROWS), jnp.int32), pltpu.VMEM((span,), jnp.int32),
                       pltpu.VMEM((SC_GATHER_ROWS, width), F32), pltpu.SemaphoreType.DMA],
        compiler_params=pltpu.CompilerParams(needs_layout_passes=False),
        name="sc_regroup",
    )
    def regroup(y_hbm, dest_hbm, pos_hbm, rows_hbm, tok_hbm, dest_v, pos_v, src_v, tok_v, rows_v, sem):
        wid = lax.axis_index("s") * SC_CORES + lax.axis_index("c")
        batch = wid // split
        first = (wid % split) * span
        lanes = lax.iota(jnp.int32, SC_LANES)

        @pl.loop(0, N_EXPERTS)
        def _(e):
            pair = batch * N_EXPERTS + e
            pltpu.sync_copy(dest_hbm.at[pair], dest_v)
            pltpu.sync_copy(pos_hbm.at[pair], pos_v)

            @pl.loop(0, s // SC_LANES)
            def _(i):
                sl = pl.ds(i * SC_LANES, SC_LANES)
                local = dest_v[sl] - first
                mine = (local >= 0) & (local < span)
                plsc.store_scatter(src_v, [local >> (SC_GATHER_ROWS.bit_length() - 1), local & (SC_GATHER_ROWS - 1)],
                                   pos_v[sl] + pair * cap, mask=mine)
                plsc.store_scatter(tok_v, [local], lanes + i * SC_LANES, mask=mine)

        out0 = batch * per_batch + first
        pltpu.sync_copy(tok_v, tok_hbm.at[pl.ds(out0, span)])
        for c in range(nchunk):
            pltpu.async_copy(y_hbm.at[src_v.at[c]], rows_v, sem).wait()
            pltpu.sync_copy(rows_v, rows_hbm.at[pl.ds(out0 + c * SC_GATHER_ROWS, SC_GATHER_ROWS)])

    return regroup(y_rows, dest2, pos2)


def _combine_kernel(start_ref, x_ref, tok_ref, rows_ref, *rest, ntile, ck):
    o_ref = rest[-1]
    b, j = pl.program_id(0), pl.program_id(1)
    tt = x_ref.shape[1]
    lo = start_ref[b * (ntile + 1) + j]
    hi = start_ref[b * (ntile + 1) + j + 1]
    c_lo = lo // ck
    c_hi = jnp.where(hi > lo, (hi - 1) // ck + 1, c_lo)
    tokens = j * tt + lax.broadcasted_iota(jnp.int32, (tt, ck), 0)
    o_ref[0] = x_ref[0]

    def body(c, carry):
        onehot = jnp.where(tok_ref[0, c] == tokens, 1.0, 0.0).astype(BF16)
        o_ref[0] += _dot(onehot, _unpack_rows(rows_ref[0, c]))
        return carry

    lax.fori_loop(c_lo, c_hi, body, 0)


def _combine_call(xn, starts, tok, rows, tt, ck, batch0, earlier):
    _, s, d = xn.shape
    nb = tok.shape[0]
    nchunk = tok.shape[1] // ck
    ntile = s // tt
    tile = pl.BlockSpec((1, tt, d), lambda i, t, st: (i + batch0, t, 0))
    in_specs = [tile, pl.BlockSpec((1, nchunk, 1, ck), lambda i, t, st: (i, 0, 0, 0)),
                pl.BlockSpec((1, nchunk, ck, rows.shape[-1]), lambda i, t, st: (i, 0, 0, 0))]
    operands = [starts, xn, tok.reshape(nb, nchunk, 1, ck), rows.reshape(nb, nchunk, ck, rows.shape[-1])]
    aliases = {}
    if earlier is not None:
        in_specs.append(pl.BlockSpec(memory_space=pl.ANY))
        operands.append(earlier)
        aliases = {len(operands) - 1: 0}
    return pl.pallas_call(
        functools.partial(_combine_kernel, ntile=ntile, ck=ck),
        grid_spec=pltpu.PrefetchScalarGridSpec(
            num_scalar_prefetch=1, grid=(nb, ntile), in_specs=in_specs, out_specs=tile),
        out_shape=jax.ShapeDtypeStruct(xn.shape, F32),
        input_output_aliases=aliases,
        compiler_params=_cparams(("parallel", "arbitrary")),
        name="combine",
    )(*operands)


def _block_diag(n, blk):
    i = np.arange(n)
    return jnp.asarray((i[:, None] // blk) == (i[None, :] // blk), dtype=BF16)


def _head_slots(w, heads, width):
    r = w.shape[0]
    return jnp.pad(w.reshape(r, heads, width), ((0, 0), (0, 0), (0, LANES - width))).reshape(r, heads * LANES)


def _rotate_half_cols(w):
    half = B_ROPE // 2
    return jnp.concatenate([-w[..., half:], w[..., :half]], axis=-1)


def _layer_params(l, w_in, norm_mix_g, diff_qk_g, mla_cq_g, w_uq, mla_ckv_g, w_ukv, mla_qk_g, swa_qk_g,
                  w_branch_a, w_branch_b, w_branch_c, w_o, norm_ffn_g, w_router):
    d = D_MODEL
    wi = w_in[l]
    off = np.cumsum([0, 512, 512, 512, B_Q_LORA, B_KV_LORA, B_ROPE, 512, 128, 128, 3 * d])
    piece = lambda k: wi[:, off[k]:off[k + 1]]
    maps_to_heads = lambda w: w.reshape(d, 2, A_HEADS, A_DIM).transpose(0, 2, 1, 3).reshape(d, 512)
    dup = lambda w: jnp.concatenate([w.reshape(d, C_KV_HEADS, 1, C_DIM)] * 2, axis=2).reshape(d, 256)
    kr = piece(5)
    rope_slot = lambda w: jnp.pad(w, ((0, 0), (B_NOPE, LANES - B_QK)))
    w_proj = jnp.concatenate(
        [maps_to_heads(piece(0)), maps_to_heads(piece(1)), piece(2), piece(3), piece(4), piece(6),
         dup(piece(7)), dup(piece(8)), rope_slot(kr), rope_slot(_rotate_half_cols(kr))], axis=1).astype(BF16)
    assert w_proj.shape == (d, _PROJ_COLS)

    wq = w_uq[l].reshape(B_Q_LORA, B_HEADS, B_QK)
    wq_rot = jnp.concatenate([jnp.zeros_like(wq[..., :B_NOPE]), _rotate_half_cols(wq[..., B_NOPE:])], axis=-1)
    w_uq_x = jnp.concatenate([_head_slots(wq.reshape(B_Q_LORA, -1), B_HEADS, B_QK),
                              _head_slots(wq_rot.reshape(B_Q_LORA, -1), B_HEADS, B_QK)], axis=1).astype(BF16)
    wkv = w_ukv[l].reshape(B_KV_LORA, B_HEADS, B_NOPE + B_VDIM)
    w_ukv_x = jnp.concatenate([_head_slots(wkv[..., :B_NOPE].reshape(B_KV_LORA, -1), B_HEADS, B_NOPE),
                               wkv[..., B_NOPE:].reshape(B_KV_LORA, -1)], axis=1).astype(BF16)

    row = lambda v: v.reshape(1, -1).astype(F32)
    slot_gain = lambda g: jnp.tile(jnp.pad(g, (0, LANES - B_QK)), B_HEADS)
    return {
        "gmix": row(norm_mix_g[l]), "w_in": w_proj, "w_uq": w_uq_x, "w_ukv": w_ukv_x,
        "e64": _block_diag(SEG_TILE, 64), "e128": _block_diag(SEG_TILE, LANES),
        "gqa": row(jnp.tile(diff_qk_g[l, 0], 8) * (A_DIM ** -0.5 * LOG2E)), "gka": row(jnp.tile(diff_qk_g[l, 1], 8)),
        "gcq": row(mla_cq_g[l]), "gckv": row(mla_ckv_g[l]),
        "gqb": row(slot_gain(mla_qk_g[l, 0]) * (B_QK ** -0.5 * LOG2E)), "gkb": row(slot_gain(mla_qk_g[l, 1])),
        "gqc": row(jnp.tile(swa_qk_g[l, 0], 8) * (C_DIM ** -0.5)), "gkc": row(jnp.tile(swa_qk_g[l, 1], 4)),
        "w_gate": piece(9).astype(BF16),
        "w_a": w_branch_a[l].astype(BF16), "w_b": w_branch_b[l].astype(BF16), "w_c": w_branch_c[l].astype(BF16),
        "w_o": w_o[l].astype(BF16), "gffn": row(norm_ffn_g[l]), "w_rt": w_router[l].T.astype(F32),
    }


def _rope_slot_tables(positions):
    inv = 1.0 / (ROPE_THETA ** (jnp.arange(0, B_ROPE, 2, dtype=F32) / B_ROPE))
    ang = positions.astype(F32)[..., None] * inv
    cos, sin = jnp.cos(ang), jnp.sin(ang)
    ones = jnp.ones(ang.shape[:-1] + (B_NOPE,), F32)
    pad = jnp.zeros(ang.shape[:-1] + (LANES - B_QK,), F32)
    return (jnp.concatenate([ones, cos, cos, pad], axis=-1),
            jnp.concatenate([jnp.zeros_like(ones), sin, sin, pad], axis=-1))


def _alibi_slopes(n):
    return 2.0 ** (-8.0 * jnp.arange(1, n + 1, dtype=F32) / n)


def kernel(x, positions, norm_mix_g, w_in, diff_qk_g, diff_lambda, diff_out_g, mla_cq_g, w_uq, mla_ckv_g, w_ukv,
           mla_qk_g, swa_qk_g, swa_sink, w_branch_a, w_branch_b, w_branch_c, w_o, norm_ffn_g, w_router,
           w_exp_gate, w_exp_up, w_exp_down):
    b, s, d = x.shape
    depth = w_in.shape[0]
    cap = max(1, EC_CAPACITY * s // N_EXPERTS)
    tm_proj = min(512, s)
    tq = min(256, s)
    tq_a = min(512, s)
    tm_merge = min(512, s)
    tt = min(256, s)
    sc_workers = SC_CORES * SC_SUBCORES
    groups = 2 if (b % 2 == 0 and (b // 2) * N_EXPERTS % sc_workers == 0 and sc_workers % (b // 2) == 0) else 1

    cos_t, sin_t = _rope_slot_tables(positions)
    pos_f = positions.astype(F32)
    pos_c, pos_r = pos_f[:, :, None], pos_f.reshape(b, s // tq_a, 1, tq_a)
    monotone = jnp.all(positions[:, 1:] >= positions[:, :-1])
    lane_bcast = lambda v: jnp.broadcast_to(v[..., None], v.shape + (LANES,)).astype(F32)
    slopes_a = lane_bcast(_alibi_slopes(A_HEADS)[:, None] * LOG2E)
    slopes_c = lane_bcast(_alibi_slopes(C_HEADS).reshape(C_KV_HEADS, C_REP))

    for l in range(depth):
        p = _layer_params(l, w_in, norm_mix_g, diff_qk_g, mla_cq_g, w_uq, mla_ckv_g, w_ukv, mla_qk_g, swa_qk_g,
                          w_branch_a, w_branch_b, w_branch_c, w_o, norm_ffn_g, w_router)
        qa, ka, va, qb, kb, vb, qc, kc, vc = _proj_call(x, cos_t, sin_t, p, tm_proj)
        lam_init = 0.8 - 0.6 * math.exp(-0.3 * l)
        attn_a = functools.partial(_attn_a_call, qa, ka, va, pos_c, pos_r, slopes_a, diff_lambda[l].astype(F32),
                                   diff_out_g[l].reshape(1, -1).astype(F32), lam_init, tq_a)
        oa = lax.cond(monotone, functools.partial(attn_a, True), functools.partial(attn_a, False))
        ob = _attn_b_call(qb, kb, vb, min(512, s))
        oc = _attn_c_call(qc, kc, vc, slopes_c, lane_bcast(swa_sink[l].reshape(C_KV_HEADS, C_REP)))
        xn, h2, aff_t = _merge_call(x, oa, ob, oc, p, tm_merge)
        pos, dest, start = _topk_call(aff_t, cap)
        starts = jnp.concatenate([start[:, 0, ::tt], jnp.full((b, 1), N_EXPERTS * cap, jnp.int32)], axis=1)
        table = h2.reshape(b * s, d // 2)
        x = None
        for grp in range(groups):
            b0, nb = grp * (b // groups), b // groups
            sl = slice(b0, b0 + nb)
            pos2 = pos[sl].reshape(nb * N_EXPERTS, s)
            rows, asel = _sc_gather_call(table, pos2, aff_t[sl].reshape(nb * N_EXPERTS, s), cap, b0)
            y = _expert_call(rows.reshape(nb, N_EXPERTS, cap, d // 2), asel.reshape(nb, N_EXPERTS, cap, SC_LANES),
                             w_exp_gate, w_exp_up, w_exp_down, l)
            y_rows, y_tok = _sc_regroup_call(y.reshape(nb * N_EXPERTS * cap, d // 2),
                                             dest[sl].reshape(nb * N_EXPERTS, s), pos2, cap)
            x = _combine_call(xn, starts[sl].reshape(-1), y_tok.reshape(nb, N_EXPERTS * cap),
                              y_rows.reshape(nb, N_EXPERTS * cap, d // 2), tt, min(256, N_EXPERTS * cap), b0, x)
    return x
```

```python
import functools
import math

import numpy as np
import jax
import jax.numpy as jnp
from jax import lax
from jax.experimental import pallas as pl
from jax.experimental.pallas import tpu as pltpu
from jax.experimental.pallas import tpu_sc as plsc

F32 = jnp.float32
BF16 = jnp.bfloat16

D_MODEL = 1024
EPS = 1e-6
A_HEADS = 4
A_DIM = 64
A_VDIM = 128
B_HEADS = 8
B_NOPE = 64
B_ROPE = 32
B_VDIM = 64
B_QK = B_NOPE + B_ROPE
B_Q_LORA = 384
B_KV_LORA = 256
ROPE_THETA = 10000.0
C_HEADS = 8
C_KV_HEADS = 2
C_REP = C_HEADS // C_KV_HEADS
C_DIM = 64
WINDOW = 128
N_EXPERTS = 16
EC_CAPACITY = 2
D_FF = 1024
LOG2E = math.log2(math.e)
SC_CORES, SC_SUBCORES, SC_LANES = 2, 16, 16
SC_GATHER_ROWS = 128
MOE_GROUPS = 4
SEG_TILE = 256
LANES = 128

_QA = 0
_KA = 512
_VA = 1024
_CQ = 1536
_CKV = 1920
_QC = 2176
_KC = 2688
_VC = 2944
_KR = 3200
_KRR = 3328
_PROJ_COLS = 3456

VMEM_LIMIT = 56 * 1024 * 1024

_NT = (((1,), (1,)), ((), ()))
_TN = (((0,), (0,)), ((), ()))


def _cparams(sem):
    return pltpu.CompilerParams(dimension_semantics=sem, vmem_limit_bytes=VMEM_LIMIT)


def _dot(a, b):
    return jnp.dot(a, b, preferred_element_type=F32)


def _seg_sum(x2, e):
    hi = x2.astype(BF16)
    lo = (x2 - hi.astype(F32)).astype(BF16)
    e2 = jnp.concatenate([e, e], axis=0)
    slabs = [_dot(jnp.concatenate([hi[:, c:c + SEG_TILE], lo[:, c:c + SEG_TILE]], axis=1), e2)
             for c in range(0, x2.shape[1], SEG_TILE)]
    return slabs[0] if len(slabs) == 1 else jnp.concatenate(slabs, axis=1)


def _pack_rows(x):
    half = x.shape[1] // 2
    xb = x.astype(BF16).astype(F32)
    lo = pltpu.bitcast(xb[:, :half], jnp.uint32) >> 16
    hi = pltpu.bitcast(xb[:, half:], jnp.uint32) & jnp.uint32(0xFFFF0000)
    return pltpu.bitcast(lo | hi, F32)


def _unpack_rows(words):
    bits = pltpu.bitcast(words, jnp.uint32)
    return jnp.concatenate([pltpu.bitcast(bits << 16, F32), pltpu.bitcast(bits & jnp.uint32(0xFFFF0000), F32)],
                           axis=1).astype(BF16)


def _sigmoid(x):
    return 0.5 * jnp.tanh(0.5 * x) + 0.5


def _row_rms(x, g):
    return x * lax.rsqrt(jnp.mean(x * x, axis=-1, keepdims=True) + EPS) * g


def _proj_kernel(x_ref, gmix_ref, w_ref, wuq_ref, wukv_ref, e64_ref, e128_ref,
                 gqa_ref, gka_ref, gcq_ref, gckv_ref, gqb_ref, gkb_ref, gqc_ref, gkc_ref,
                 cos_ref, sin_ref,
                 qa_o, ka_o, va_o, qb_o, kb_o, vb_o, qc_o, kc_o, vc_o):
    hb = _row_rms(x_ref[0], gmix_ref[...]).astype(BF16)

    projected = _dot(hb, w_ref[...])

    def proj(a, n):
        return projected[:, a:a + n]

    e64 = e64_ref[...]
    e128 = e128_ref[...]

    def seg_norm(v, e, width, g):
        return v * lax.rsqrt(_seg_sum(v * v, e) * (1.0 / width) + EPS) * g

    def store_slots(o_ref, v, n):
        for j in range(n):
            o_ref[0, j] = v[:, LANES * j:LANES * (j + 1)].astype(o_ref.dtype)

    ones_slot = jnp.ones((hb.shape[0], LANES), F32)

    def store_value_slots(o_ref, v, n):
        for j in range(n):
            o_ref[0, j] = jnp.concatenate([v[:, LANES * j:LANES * (j + 1)], ones_slot], axis=1).astype(o_ref.dtype)

    store_slots(qa_o, seg_norm(proj(_QA, 512), e64, A_DIM, gqa_ref[...]), A_HEADS)
    store_slots(ka_o, seg_norm(proj(_KA, 512), e64, A_DIM, gka_ref[...]), A_HEADS)
    store_value_slots(va_o, proj(_VA, 512), A_HEADS)

    cos_t = cos_ref[0]
    sin_t = sin_ref[0]
    cos8 = jnp.concatenate([cos_t] * B_HEADS, axis=1)
    sin8 = jnp.concatenate([sin_t] * B_HEADS, axis=1)
    cq = _row_rms(proj(_CQ, B_Q_LORA), gcq_ref[...]).astype(BF16)
    q2 = _dot(cq, wuq_ref[...])
    qb = q2[:, :1024] * cos8 + q2[:, 1024:] * sin8
    store_slots(qb_o, seg_norm(qb, e128, B_QK, gqb_ref[...]), B_HEADS)
    ckv = _row_rms(proj(_CKV, B_KV_LORA), gckv_ref[...]).astype(BF16)
    kv = _dot(ckv, wukv_ref[...])
    kr = proj(_KR, LANES) * cos_t + proj(_KRR, LANES) * sin_t
    kb = kv[:, :1024] + jnp.concatenate([kr] * B_HEADS, axis=1)
    store_slots(kb_o, seg_norm(kb, e128, B_QK, gkb_ref[...]), B_HEADS)
    store_value_slots(vb_o, kv[:, 1024:], B_HEADS // 2)

    store_slots(qc_o, seg_norm(proj(_QC, 512), e64, C_DIM, gqc_ref[...]), C_HEADS // 2)
    store_slots(kc_o, seg_norm(proj(_KC, 256), e64, C_DIM, gkc_ref[...]), C_KV_HEADS)
    store_value_slots(vc_o, proj(_VC, 256), C_KV_HEADS)


def _proj_call(x, cos_t, sin_t, p, tm):
    b, s, d = x.shape
    full = lambda a: pl.BlockSpec(a.shape, lambda i, j: (0,) * a.ndim, pipeline_mode=pl.Buffered(1))
    slot = lambda nw: pl.BlockSpec((1, nw[0], tm, nw[1]), lambda i, j: (i, 0, j, 0))
    tok = lambda w: pl.BlockSpec((1, tm, w), lambda i, j: (i, j, 0))
    consts = [p["gmix"], p["w_in"], p["w_uq"], p["w_ukv"], p["e64"], p["e128"],
              p["gqa"], p["gka"], p["gcq"], p["gckv"], p["gqb"], p["gkb"], p["gqc"], p["gkc"]]
    slots = [(A_HEADS, LANES), (A_HEADS, LANES), (A_HEADS, 2 * LANES), (B_HEADS, LANES), (B_HEADS, LANES),
             (B_HEADS // 2, 2 * LANES), (C_HEADS // 2, LANES), (C_KV_HEADS, LANES), (C_KV_HEADS, 2 * LANES)]
    return pl.pallas_call(
        _proj_kernel,
        grid=(b, s // tm),
        in_specs=[tok(d)] + [full(a) for a in consts] + [tok(LANES), tok(LANES)],
        out_specs=[slot(nw) for nw in slots],
        out_shape=[jax.ShapeDtypeStruct((b, nw[0], s, nw[1]), BF16) for nw in slots],
        compiler_params=_cparams(("parallel", "parallel")),
        name="proj",
    )(x, *consts, cos_t, sin_t)


def _attn_a_kernel(q_ref, k_ref, v_ref, pc_ref, pr_ref, slope_ref, lam_ref, og_ref, o_ref, s_scr, *, lam_init, mono):
    t = pl.program_id(2)
    q = q_ref[0, 0]
    tq = q.shape[0]
    ck = tq
    nck = k_ref.shape[2] // ck
    lane = lax.broadcasted_iota(jnp.int32, q.shape, 1)
    zero = jnp.zeros_like(q)
    qm = [jnp.where(lane < A_DIM, q, zero), jnp.where(lane >= A_DIM, q, zero)]
    slope = slope_ref[0][:, :1]
    a = slope * pc_ref[0]
    a_lanes = jnp.broadcast_to(a, (tq, LANES))

    def lane_fold_max(x):
        out = x[:, :LANES]
        for j in range(1, ck // LANES):
            out = jnp.maximum(out, x[:, j * LANES:(j + 1) * LANES])
        return out

    mx = [jnp.full((tq, LANES), -jnp.inf, F32)] * 2
    for d in range(nck):
        if mono:
            c = t if d == 0 else lax.rem(t + d, nck)
            k_c = k_ref[0, 0, pl.ds(pl.multiple_of(c * ck, ck), ck), :]
        else:
            c = d
            k_c = k_ref[0, 0, d * ck:(d + 1) * ck, :]
        b = slope * pr_ref[0, c]
        if mono and d > 0:
            sign = jnp.where(t + d < nck, -1.0, 1.0)
            row_part, col_part = sign * b, sign * a_lanes
            for m in range(2):
                sc = lax.dot_general(qm[m], k_c, _NT, preferred_element_type=F32) + row_part
                s_scr[m, c] = sc
                mx[m] = jnp.maximum(mx[m], lane_fold_max(sc) - col_part)
        else:
            bias = jnp.abs(a - b)
            for m in range(2):
                sc = lax.dot_general(qm[m], k_c, _NT, preferred_element_type=F32) - bias
                s_scr[m, c] = sc
                mx[m] = jnp.maximum(mx[m], lane_fold_max(sc))
    v = v_ref[0, 0]
    acc = []
    for m in range(2):
        row_max = jnp.broadcast_to(jnp.max(mx[m], axis=-1, keepdims=True), (tq, LANES))
        es = []
        for c in range(nck):
            if mono:
                sign = jnp.where(c < t, 1.0, jnp.where(c > t, -1.0, 0.0))
                stab = row_max + sign * a_lanes
            else:
                stab = row_max
            es.append(jnp.exp2(s_scr[m, c] - jnp.concatenate([stab] * (ck // LANES), axis=1)).astype(BF16))
        e = es[0] if nck == 1 else jnp.concatenate(es, axis=1)
        acc.append(_dot(e, v))
    lp = lam_ref[...]
    lam = (jnp.exp(jnp.sum(lp[0:1] * lp[1:2], axis=-1, keepdims=True))
           - jnp.exp(jnp.sum(lp[2:3] * lp[3:4], axis=-1, keepdims=True)) + lam_init)
    o = (acc[0][:, :LANES] * (1.0 / acc[0][:, LANES:])
         - acc[1][:, :LANES] * (lam / acc[1][:, LANES:]))
    o_ref[0] = (_row_rms(o, og_ref[...]) * (1.0 - lam_init)).astype(o_ref.dtype)


def _attn_a_call(qa, ka, va, pos_c, pos_r, slopes, lam_p, out_g, lam_init, tq, mono):
    b, h, s, _ = qa.shape
    kv_spec = lambda w: pl.BlockSpec((1, 1, s, w), lambda i, j, t: (i, j, 0, 0))
    return pl.pallas_call(
        functools.partial(_attn_a_kernel, lam_init=lam_init, mono=mono),
        grid=(b, h, s // tq),
        scratch_shapes=[pltpu.VMEM((2, s // tq, tq, tq), F32)],
        in_specs=[pl.BlockSpec((1, 1, tq, LANES), lambda i, j, t: (i, j, t, 0)), kv_spec(LANES), kv_spec(2 * LANES),
                  pl.BlockSpec((1, tq, 1), lambda i, j, t: (i, t, 0)),
                  pl.BlockSpec((1, s // tq, 1, tq), lambda i, j, t: (i, 0, 0, 0)),
                  pl.BlockSpec((1, 1, LANES), lambda i, j, t: (j, 0, 0)),
                  pl.BlockSpec(lam_p.shape, lambda i, j, t: (0, 0)),
                  pl.BlockSpec(out_g.shape, lambda i, j, t: (0, 0))],
        out_specs=pl.BlockSpec((1, tq, LANES), lambda i, j, t: (i, t, j)),
        out_shape=jax.ShapeDtypeStruct((b, s, h * LANES), BF16),
        compiler_params=_cparams(("parallel", "parallel", "arbitrary")),
        name="attn_a",
    )(qa, ka, va, pos_c, pos_r, slopes, lam_p, out_g)


def _attn_b_kernel(q_ref, k_ref, v_ref, o_ref):
    tq = q_ref.shape[2]
    es = []
    for j in range(2):
        s = lax.dot_general(q_ref[0, j], k_ref[0, j], _NT, preferred_element_type=F32)
        es.append(jnp.exp2(s - jnp.max(s, axis=-1, keepdims=True)).astype(BF16))
    acc = _dot(jnp.concatenate(es, axis=0), v_ref[0, 0])
    lane = lax.broadcasted_iota(jnp.int32, (tq, LANES), 1)
    o_ref[0] = jnp.where(lane < B_VDIM, acc[:tq, :LANES] * (1.0 / acc[:tq, LANES:]),
                         acc[tq:, :LANES] * (1.0 / acc[tq:, LANES:])).astype(o_ref.dtype)


def _attn_b_call(qb, kb, vb, tq):
    b, h, s, _ = qb.shape
    return pl.pallas_call(
        _attn_b_kernel,
        grid=(b, h // 2, s // tq),
        in_specs=[pl.BlockSpec((1, 2, tq, LANES), lambda i, j, t: (i, j, t, 0)),
                  pl.BlockSpec((1, 2, s, LANES), lambda i, j, t: (i, j, 0, 0)),
                  pl.BlockSpec((1, 1, s, 2 * LANES), lambda i, j, t: (i, j, 0, 0))],
        out_specs=pl.BlockSpec((1, tq, LANES), lambda i, j, t: (i, t, j)),
        out_shape=jax.ShapeDtypeStruct((b, s, (h // 2) * LANES), BF16),
        compiler_params=_cparams(("parallel", "parallel", "arbitrary")),
        name="attn_b",
    )(qb, kb, vb)


def _attn_c_kernel(q_ref, kp_ref, ko_ref, kn_ref, vp_ref, vo_ref, vn_ref, slope_ref, sink_ref, o_ref, *, seq):
    w = WINDOW
    nsub = ko_ref.shape[2] // w
    n0 = pl.program_id(2) * nsub
    kcat = jnp.concatenate([kp_ref[0, 0], ko_ref[0, 0], kn_ref[0, 0]], axis=0)
    vcat = jnp.concatenate([vp_ref[0, 0], vo_ref[0, 0], vn_ref[0, 0]], axis=0)
    lane = lax.broadcasted_iota(jnp.int32, (w, LANES), 1)
    r_idx = lax.broadcasted_iota(jnp.int32, (w, 3 * w), 0)
    c_idx = lax.broadcasted_iota(jnp.int32, (w, 3 * w), 1)
    arel = jnp.abs(c_idx - w - r_idx)
    dist = arel.astype(F32)
    slopes = slope_ref[0]
    sinks = sink_ref[0]
    bias4 = jnp.concatenate([jnp.where(arel <= w, -slopes[r:r + 1, :1] * dist, -1e30) for r in range(C_REP)], axis=0)
    sink4 = jnp.concatenate([jnp.broadcast_to(sinks[r:r + 1, :], (w, LANES)) for r in range(C_REP)], axis=0)
    c_row = lax.broadcasted_iota(jnp.int32, (1, 3 * w), 1)
    scs = []
    for i in range(nsub):
        parts = []
        for p in range(2):
            q = q_ref[0, p, i * w:(i + 1) * w, :]
            zero = jnp.zeros_like(q)
            parts += [jnp.where(lane < C_DIM, q, zero), jnp.where(lane >= C_DIM, q, zero)]
        qz = jnp.concatenate(parts, axis=0)
        s = lax.dot_general(qz, kcat[i * w:(i + 3) * w], _NT, preferred_element_type=F32)
        kidx = (n0 + i - 1) * w + c_row
        edge = jnp.where(kidx >= 0, jnp.where(kidx < seq, 0.0, -1e30), -1e30)
        scs.append(s + bias4 + edge)
    sc = jnp.concatenate(scs, axis=0)
    sk = jnp.concatenate([sink4] * nsub, axis=0)
    m = jnp.maximum(jnp.broadcast_to(jnp.max(sc, axis=-1, keepdims=True), sk.shape), sk)
    e = jnp.exp(sc - jnp.concatenate([m] * 3, axis=1)).astype(BF16)
    tail = jnp.exp(sk - m)
    for i in range(nsub):
        rows = slice(i * 4 * w, (i + 1) * 4 * w)
        acc = _dot(e[rows], vcat[i * w:(i + 3) * w])
        o = acc[:, :LANES] * (1.0 / (acc[:, LANES:] + tail[rows]))
        pair0 = jnp.where(lane < C_DIM, o[0:w], o[w:2 * w])
        pair1 = jnp.where(lane < C_DIM, o[2 * w:3 * w], o[3 * w:4 * w])
        o_ref[0, i * w:(i + 1) * w, :] = jnp.concatenate([pair0, pair1], axis=1).astype(o_ref.dtype)


def _attn_c_call(qc, kc, vc, slopes, sinks):
    b, _, s, _ = qc.shape
    nb = s // WINDOW
    nsub = min(16, nb)
    tq = nsub * WINDOW
    prev = lambda wd: pl.BlockSpec((1, 1, WINDOW, wd), lambda i, g, n: (i, g, jnp.maximum(n * nsub - 1, 0), 0))
    own = lambda wd: pl.BlockSpec((1, 1, tq, wd), lambda i, g, n: (i, g, n, 0))
    nxt = lambda wd: pl.BlockSpec((1, 1, WINDOW, wd), lambda i, g, n: (i, g, jnp.minimum((n + 1) * nsub, nb - 1), 0))
    kw, vw = kc.shape[-1], vc.shape[-1]
    per_group = pl.BlockSpec((1, C_REP, LANES), lambda i, g, n: (g, 0, 0))
    return pl.pallas_call(
        functools.partial(_attn_c_kernel, seq=s),
        grid=(b, C_KV_HEADS, s // tq),
        in_specs=[pl.BlockSpec((1, 2, tq, LANES), lambda i, g, n: (i, g, n, 0)),
                  prev(kw), own(kw), nxt(kw), prev(vw), own(vw), nxt(vw), per_group, per_group],
        out_specs=pl.BlockSpec((1, tq, 2 * LANES), lambda i, g, n: (i, n, g)),
        out_shape=jax.ShapeDtypeStruct((b, s, C_HEADS * C_DIM), BF16),
        compiler_params=_cparams(("parallel", "parallel", "arbitrary")),
        name="attn_c",
    )(qc, kc, kc, kc, vc, vc, vc, slopes, sinks)


def _merge_kernel(x_ref, gmix_ref, wg_ref, oa_ref, ob_ref, oc_ref, wa_ref, wb_ref, wc_ref, wo_ref,
                  gffn_ref, wr_ref, xn_o, h2_o, aff_o):
    d = D_MODEL
    x = x_ref[0]
    hb = _row_rms(x, gmix_ref[...]).astype(BF16)
    g = _sigmoid(_dot(hb, wg_ref[...]))
    merged = (g[:, :d] * _dot(oa_ref[0], wa_ref[...]) + g[:, d:2 * d] * _dot(ob_ref[0], wb_ref[...])
              + g[:, 2 * d:] * _dot(oc_ref[0], wc_ref[...]))
    xn = x + _dot(merged.astype(BF16), wo_ref[...])
    xn_o[0] = xn
    h2 = _row_rms(xn, gffn_ref[...])
    h2_o[0] = _pack_rows(h2)
    logits = lax.dot_general(wr_ref[...], h2, _NT, preferred_element_type=F32,
                             precision=lax.Precision.HIGHEST)
    ex = jnp.exp(logits - jnp.max(logits, axis=0, keepdims=True))
    aff_o[0] = ex / jnp.sum(ex, axis=0, keepdims=True)


def _merge_call(x, oa, ob, oc, p, tm):
    b, s, d = x.shape
    full = lambda a: pl.BlockSpec(a.shape, lambda i, j: (0,) * a.ndim, pipeline_mode=pl.Buffered(1))
    tok = lambda w: pl.BlockSpec((1, tm, w), lambda i, j: (i, j, 0))
    return pl.pallas_call(
        _merge_kernel,
        grid=(b, s // tm),
        in_specs=[tok(d), full(p["gmix"]), full(p["w_gate"]), tok(512), tok(512), tok(512),
                  full(p["w_a"]), full(p["w_b"]), full(p["w_c"]), full(p["w_o"]), full(p["gffn"]), full(p["w_rt"])],
        out_specs=[tok(d), tok(d // 2), pl.BlockSpec((1, N_EXPERTS, tm), lambda i, j: (i, 0, j))],
        out_shape=[jax.ShapeDtypeStruct((b, s, d), F32), jax.ShapeDtypeStruct((b, s, d // 2), F32),
                   jax.ShapeDtypeStruct((b, N_EXPERTS, s), F32)],
        compiler_params=_cparams(("parallel", "parallel")),
        name="merge",
    )(x, p["gmix"], p["w_gate"], oa, ob, oc, p["w_a"], p["w_b"], p["w_c"], p["w_o"], p["gffn"], p["w_rt"])


def _cumsum_lanes(mask01, chunk):
    rows, s = mask01.shape
    tri = jnp.where(lax.broadcasted_iota(jnp.int32, (chunk, chunk), 0)
                    <= lax.broadcasted_iota(jnp.int32, (chunk, chunk), 1), 1.0, 0.0).astype(BF16)
    carry = jnp.zeros((rows, 1), F32)
    outs = []
    for c in range(s // chunk):
        cs = _dot(mask01[:, c * chunk:(c + 1) * chunk], tri) + carry
        outs.append(cs)
        carry = cs[:, chunk - 1:chunk]
    return jnp.concatenate(outs, axis=1)


def _topk_kernel(aff_ref, pos_o, dest_o, start_o, *, cap, chunk):
    bits = pltpu.bitcast(aff_ref[0], jnp.int32)
    rows = bits.shape[0]
    capf = float(cap)

    def count(mask):
        return jnp.sum(jnp.where(mask, 1.0, 0.0), axis=1, keepdims=True)

    def body(_, c):
        lo, hi = c
        mid = lo + ((hi - lo + 1) >> 1)
        ok = count(bits >= mid) >= capf
        return jnp.where(ok, mid, lo), jnp.where(ok, hi, mid - 1)

    lo0 = jnp.zeros((rows, 1), jnp.int32)
    hi0 = jnp.full((rows, 1), 0x7F800000, jnp.int32)
    thr, _ = lax.fori_loop(0, 31, body, (lo0, hi0))
    gt = bits > thr
    eq = bits == thr
    need = capf - count(gt)
    eq_rank = _cumsum_lanes(jnp.where(eq, 1.0, 0.0).astype(BF16), chunk)
    sel = jnp.where(gt, 1.0, jnp.where(eq, jnp.where(eq_rank <= need, 1.0, 0.0), 0.0))
    sel_b = sel.astype(BF16)
    slot = _cumsum_lanes(sel_b, chunk) - 1.0
    pos_o[0] = jnp.where(sel > 0.0, slot, -1.0).astype(jnp.int32)
    per_token = jnp.broadcast_to(jnp.sum(sel, axis=0, keepdims=True), (8, sel.shape[1]))
    before_token = (_cumsum_lanes(per_token.astype(BF16), chunk) - per_token)[0:1]
    lower = jnp.where(lax.broadcasted_iota(jnp.int32, (rows, rows), 1)
                      < lax.broadcasted_iota(jnp.int32, (rows, rows), 0), 1.0, 0.0).astype(BF16)
    before_expert = _dot(lower, sel_b)
    dest_o[0] = jnp.where(sel > 0.0, before_token + before_expert, -1.0).astype(jnp.int32)
    start_o[0] = before_token.astype(jnp.int32)


def _topk_call(aff_t, cap):
    b, e, s = aff_t.shape
    spec = pl.BlockSpec((1, e, s), lambda i: (i, 0, 0))
    return pl.pallas_call(
        functools.partial(_topk_kernel, cap=cap, chunk=min(512, s)),
        grid=(b,),
        in_specs=[spec],
        out_specs=[spec, spec, pl.BlockSpec((1, 1, s), lambda i: (i, 0, 0))],
        out_shape=[jax.ShapeDtypeStruct((b, e, s), jnp.int32), jax.ShapeDtypeStruct((b, e, s), jnp.int32),
                   jax.ShapeDtypeStruct((b, 1, s), jnp.int32)],
        compiler_params=_cparams(("parallel",)),
        name="topk",
    )(aff_t)


def _sc_gather_call(table, pos2, aff2, cap, batch0):
    npairs, s = pos2.shape
    width = table.shape[1]
    workers = SC_CORES * SC_SUBCORES
    per_worker = npairs // workers
    nchunk = cap // SC_GATHER_ROWS
    assert npairs % workers == 0 and cap % SC_GATHER_ROWS == 0 and s % SC_LANES == 0
    mesh = plsc.VectorSubcoreMesh(core_axis_name="c", subcore_axis_name="s")

    @functools.partial(
        pl.kernel, mesh=mesh,
        out_type=[jax.ShapeDtypeStruct((npairs * cap, width), F32),
                  jax.ShapeDtypeStruct((npairs * cap * SC_LANES,), F32)],
        scratch_types=[pltpu.VMEM((s,), jnp.int32), pltpu.VMEM((s,), F32),
                       pltpu.VMEM((nchunk, SC_GATHER_ROWS), jnp.int32), pltpu.VMEM((cap,), F32),
                       pltpu.VMEM((cap * SC_LANES,), F32), pltpu.VMEM((SC_GATHER_ROWS, width), F32),
                       pltpu.SemaphoreType.DMA],
        compiler_params=pltpu.CompilerParams(needs_layout_passes=False),
        name="sc_gather",
    )
    def gather(table_hbm, pos_hbm, aff_hbm, rows_hbm, asel_hbm, pos_v, aff_v, idx_v, aslot_v, asplat_v, rows_v, sem):
        wid = lax.axis_index("s") * SC_CORES + lax.axis_index("c")
        lanes = lax.iota(jnp.int32, SC_LANES)

        @pl.loop(0, per_worker)
        def _(j):
            pair = wid * per_worker + j
            row0 = (pair // N_EXPERTS + batch0) * s
            pltpu.sync_copy(pos_hbm.at[pair], pos_v)
            pltpu.sync_copy(aff_hbm.at[pair], aff_v)

            @pl.loop(0, s // SC_LANES)
            def _(i):
                sl = pl.ds(i * SC_LANES, SC_LANES)
                slot = pos_v[sl]
                chosen = slot >= 0
                token_row = lanes + (i * SC_LANES + row0)
                plsc.store_scatter(idx_v, [slot >> (SC_GATHER_ROWS.bit_length() - 1), slot & (SC_GATHER_ROWS - 1)], token_row, mask=chosen)
                plsc.store_scatter(aslot_v, [slot], aff_v[sl], mask=chosen)

            @pl.loop(0, cap)
            def _(r):
                asplat_v[pl.ds(r * SC_LANES, SC_LANES)] = plsc.load_gather(
                    aslot_v, [jnp.full((SC_LANES,), r, jnp.int32)])

            pltpu.sync_copy(asplat_v, asel_hbm.at[pl.ds(pair * (cap * SC_LANES), cap * SC_LANES)])
            for c in range(nchunk):
                pltpu.async_copy(table_hbm.at[idx_v.at[c]], rows_v, sem).wait()
                pltpu.sync_copy(rows_v, rows_hbm.at[pl.ds(pair * cap + c * SC_GATHER_ROWS, SC_GATHER_ROWS)])

    return gather(table, pos2, aff2)


def _expert_kernel(xin_ref, asel_ref, wg_ref, wu_ref, wd_ref, y_o, wg_s, wu_s, wd_s):
    @pl.when(pl.program_id(1) == 0)
    def _():
        wg_s[...] = wg_ref[0, 0].astype(BF16)
        wu_s[...] = wu_ref[0, 0].astype(BF16)
        wd_s[...] = wd_ref[0, 0].astype(BF16)

    xin = _unpack_rows(xin_ref[0, 0])
    g = _dot(xin, wg_s[...])
    u = _dot(xin, wu_s[...])
    hid = (g * _sigmoid(g) * u).astype(BF16)
    y_o[0, 0] = _pack_rows(_dot(hid, wd_s[...]) * asel_ref[0, 0][:, :1])


def _expert_call(xin, asel, wg, wu, wd, layer):
    b, e, cap, _ = xin.shape
    d, f = wg.shape[-2:]
    tokens = lambda w: pl.BlockSpec((1, 1, cap, w), lambda j, i: (i, j, 0, 0))
    weight = lambda r, c: pl.BlockSpec((1, 1, r, c), lambda j, i: (layer, j, 0, 0))
    return pl.pallas_call(
        _expert_kernel,
        grid=(e, b),
        in_specs=[tokens(xin.shape[-1]), tokens(asel.shape[-1]), weight(d, f), weight(d, f), weight(f, d)],
        out_specs=tokens(d // 2),
        out_shape=jax.ShapeDtypeStruct((b, e, cap, d // 2), F32),
        scratch_shapes=[pltpu.VMEM((d, f), BF16), pltpu.VMEM((d, f), BF16), pltpu.VMEM((f, d), BF16)],
        compiler_params=_cparams(("parallel", "arbitrary")),
        name="expert",
    )(xin, asel, wg, wu, wd)


def _sc_regroup_call(y_rows, dest2, pos2, cap):
    npairs, s = dest2.shape
    width = y_rows.shape[1]
    workers = SC_CORES * SC_SUBCORES
    batches = npairs // N_EXPERTS
    per_batch = N_EXPERTS * cap
    split = workers // batches
    span = per_batch // split
    nchunk = span // SC_GATHER_ROWS
    assert workers % batches == 0 and per_batch % split == 0 and span % SC_GATHER_ROWS == 0 and s % SC_LANES == 0
    mesh = plsc.VectorSubcoreMesh(core_axis_name="c", subcore_axis_name="s")

    @functools.partial(
        pl.kernel, mesh=mesh,
        out_type=[jax.ShapeDtypeStruct((npairs * cap, width), F32),
                  jax.ShapeDtypeStruct((npairs * cap,), jnp.int32)],
        scratch_types=[pltpu.VMEM((s,), jnp.int32), pltpu.VMEM((s,), jnp.int32),
                       pltpu.VMEM((nchunk, SC_GATHER_ROWS), jnp.int32), pltpu.VMEM((span,), jnp.int32),
                       pltpu.VMEM((SC_GATHER_ROWS, width), F32), pltpu.SemaphoreType.DMA],
        compiler_params=pltpu.CompilerParams(needs_layout_passes=False),
        name="sc_regroup",
    )
    def regroup(y_hbm, dest_hbm, pos_hbm, rows_hbm, tok_hbm, dest_v, pos_v, src_v, tok_v, rows_v, sem):
        wid = lax.axis_index("s") * SC_CORES + lax.axis_index("c")
        batch = wid // split
        first = (wid % split) * span
        lanes = lax.iota(jnp.int32, SC_LANES)

        @pl.loop(0, N_EXPERTS)
        def _(e):
            pair = batch * N_EXPERTS + e
            pltpu.sync_copy(dest_hbm.at[pair], dest_v)
            pltpu.sync_copy(pos_hbm.at[pair], pos_v)

            @pl.loop(0, s // SC_LANES)
            def _(i):
                sl = pl.ds(i * SC_LANES, SC_LANES)
                local = dest_v[sl] - first
                mine = (local >= 0) & (local < span)
                plsc.store_scatter(src_v, [local >> (SC_GATHER_ROWS.bit_length() - 1), local & (SC_GATHER_ROWS - 1)],
                                   pos_v[sl] + pair * cap, mask=mine)
                plsc.store_scatter(tok_v, [local], lanes + i * SC_LANES, mask=mine)

        out0 = batch * per_batch + first
        pltpu.sync_copy(tok_v, tok_hbm.at[pl.ds(out0, span)])
        for c in range(nchunk):
            pltpu.async_copy(y_hbm.at[src_v.at[c]], rows_v, sem).wait()
            pltpu.sync_copy(rows_v, rows_hbm.at[pl.ds(out0 + c * SC_GATHER_ROWS, SC_GATHER_ROWS)])

    return regroup(y_rows, dest2, pos2)


def _combine_kernel(start_ref, x_ref, tok_ref, rows_ref, *rest, ntile, ck):
    o_ref = rest[-1]
    b, j = pl.program_id(0), pl.program_id(1)
    tt = x_ref.shape[1]
    lo = start_ref[b * (ntile + 1) + j]
    hi = start_ref[b * (ntile + 1) + j + 1]
    c_lo = lo // ck
    c_hi = jnp.where(hi > lo, (hi - 1) // ck + 1, c_lo)
    tokens = j * tt + lax.broadcasted_iota(jnp.int32, (tt, ck), 0)
    o_ref[0] = x_ref[0]

    def body(c, carry):
        onehot = jnp.where(tok_ref[0, c] == tokens, 1.0, 0.0).astype(BF16)
        o_ref[0] += _dot(onehot, _unpack_rows(rows_ref[0, c]))
        return carry

    lax.fori_loop(c_lo, c_hi, body, 0)


def _combine_call(xn, starts, tok, rows, tt, ck, batch0, earlier):
    _, s, d = xn.shape
    nb = tok.shape[0]
    nchunk = tok.shape[1] // ck
    ntile = s // tt
    tile = pl.BlockSpec((1, tt, d), lambda i, t, st: (i + batch0, t, 0))
    in_specs = [tile, pl.BlockSpec((1, nchunk, 1, ck), lambda i, t, st: (i, 0, 0, 0)),
                pl.BlockSpec((1, nchunk, ck, rows.shape[-1]), lambda i, t, st: (i, 0, 0, 0))]
    operands = [starts, xn, tok.reshape(nb, nchunk, 1, ck), rows.reshape(nb, nchunk, ck, rows.shape[-1])]
    aliases = {}
    if earlier is not None:
        in_specs.append(pl.BlockSpec(memory_space=pl.ANY))
        operands.append(earlier)
        aliases = {len(operands) - 1: 0}
    return pl.pallas_call(
        functools.partial(_combine_kernel, ntile=ntile, ck=ck),
        grid_spec=pltpu.PrefetchScalarGridSpec(
            num_scalar_prefetch=1, grid=(nb, ntile), in_specs=in_specs, out_specs=tile),
        out_shape=jax.ShapeDtypeStruct(xn.shape, F32),
        input_output_aliases=aliases,
        compiler_params=_cparams(("parallel", "arbitrary")),
        name="combine",
    )(*operands)


def _block_diag(n, blk):
    i = np.arange(n)
    return jnp.asarray((i[:, None] // blk) == (i[None, :] // blk), dtype=BF16)


def _head_slots(w, heads, width):
    r = w.shape[0]
    return jnp.pad(w.reshape(r, heads, width), ((0, 0), (0, 0), (0, LANES - width))).reshape(r, heads * LANES)


def _rotate_half_cols(w):
    half = B_ROPE // 2
    return jnp.concatenate([-w[..., half:], w[..., :half]], axis=-1)


def _layer_params(l, w_in, norm_mix_g, diff_qk_g, mla_cq_g, w_uq, mla_ckv_g, w_ukv, mla_qk_g, swa_qk_g,
                  w_branch_a, w_branch_b, w_branch_c, w_o, norm_ffn_g, w_router):
    d = D_MODEL
    wi = w_in[l]
    off = np.cumsum([0, 512, 512, 512, B_Q_LORA, B_KV_LORA, B_ROPE, 512, 128, 128, 3 * d])
    piece = lambda k: wi[:, off[k]:off[k + 1]]
    maps_to_heads = lambda w: w.reshape(d, 2, A_HEADS, A_DIM).transpose(0, 2, 1, 3).reshape(d, 512)
    dup = lambda w: jnp.concatenate([w.reshape(d, C_KV_HEADS, 1, C_DIM)] * 2, axis=2).reshape(d, 256)
    kr = piece(5)
    rope_slot = lambda w: jnp.pad(w, ((0, 0), (B_NOPE, LANES - B_QK)))
    w_proj = jnp.concatenate(
        [maps_to_heads(piece(0)), maps_to_heads(piece(1)), piece(2), piece(3), piece(4), piece(6),
         dup(piece(7)), dup(piece(8)), rope_slot(kr), rope_slot(_rotate_half_cols(kr))], axis=1).astype(BF16)
    assert w_proj.shape == (d, _PROJ_COLS)

    wq = w_uq[l].reshape(B_Q_LORA, B_HEADS, B_QK)
    wq_rot = jnp.concatenate([jnp.zeros_like(wq[..., :B_NOPE]), _rotate_half_cols(wq[..., B_NOPE:])], axis=-1)
    w_uq_x = jnp.concatenate([_head_slots(wq.reshape(B_Q_LORA, -1), B_HEADS, B_QK),
                              _head_slots(wq_rot.reshape(B_Q_LORA, -1), B_HEADS, B_QK)], axis=1).astype(BF16)
    wkv = w_ukv[l].reshape(B_KV_LORA, B_HEADS, B_NOPE + B_VDIM)
    w_ukv_x = jnp.concatenate([_head_slots(wkv[..., :B_NOPE].reshape(B_KV_LORA, -1), B_HEADS, B_NOPE),
                               wkv[..., B_NOPE:].reshape(B_KV_LORA, -1)], axis=1).astype(BF16)

    row = lambda v: v.reshape(1, -1).astype(F32)
    slot_gain = lambda g: jnp.tile(jnp.pad(g, (0, LANES - B_QK)), B_HEADS)
    return {
        "gmix": row(norm_mix_g[l]), "w_in": w_proj, "w_uq": w_uq_x, "w_ukv": w_ukv_x,
        "e64": _block_diag(SEG_TILE, 64), "e128": _block_diag(SEG_TILE, LANES),
        "gqa": row(jnp.tile(diff_qk_g[l, 0], 8) * (A_DIM ** -0.5 * LOG2E)), "gka": row(jnp.tile(diff_qk_g[l, 1], 8)),
        "gcq": row(mla_cq_g[l]), "gckv": row(mla_ckv_g[l]),
        "gqb": row(slot_gain(mla_qk_g[l, 0]) * (B_QK ** -0.5 * LOG2E)), "gkb": row(slot_gain(mla_qk_g[l, 1])),
        "gqc": row(jnp.tile(swa_qk_g[l, 0], 8) * (C_DIM ** -0.5)), "gkc": row(jnp.tile(swa_qk_g[l, 1], 4)),
        "w_gate": piece(9).astype(BF16),
        "w_a": w_branch_a[l].astype(BF16), "w_b": w_branch_b[l].astype(BF16), "w_c": w_branch_c[l].astype(BF16),
        "w_o": w_o[l].astype(BF16), "gffn": row(norm_ffn_g[l]), "w_rt": w_router[l].T.astype(F32),
    }


def _rope_slot_tables(positions):
    inv = 1.0 / (ROPE_THETA ** (jnp.arange(0, B_ROPE, 2, dtype=F32) / B_ROPE))
    ang = positions.astype(F32)[..., None] * inv
    cos, sin = jnp.cos(ang), jnp.sin(ang)
    ones = jnp.ones(ang.shape[:-1] + (B_NOPE,), F32)
    pad = jnp.zeros(ang.shape[:-1] + (LANES - B_QK,), F32)
    return (jnp.concatenate([ones, cos, cos, pad], axis=-1),
            jnp.concatenate([jnp.zeros_like(ones), sin, sin, pad], axis=-1))


def _alibi_slopes(n):
    return 2.0 ** (-8.0 * jnp.arange(1, n + 1, dtype=F32) / n)


def kernel(x, positions, norm_mix_g, w_in, diff_qk_g, diff_lambda, diff_out_g, mla_cq_g, w_uq, mla_ckv_g, w_ukv,
           mla_qk_g, swa_qk_g, swa_sink, w_branch_a, w_branch_b, w_branch_c, w_o, norm_ffn_g, w_router,
           w_exp_gate, w_exp_up, w_exp_down):
    b, s, d = x.shape
    depth = w_in.shape[0]
    cap = max(1, EC_CAPACITY * s // N_EXPERTS)
    tm_proj = min(512, s)
    tq = min(256, s)
    tq_a = min(512, s)
    tm_merge = min(512, s)
    tt = min(256, s)
    sc_workers = SC_CORES * SC_SUBCORES
    fits = lambda g: b % g == 0 and (b // g) * N_EXPERTS % sc_workers == 0 and sc_workers % (b // g) == 0
    groups = next((g for g in (MOE_GROUPS, 2) if fits(g)), 1)

    cos_t, sin_t = _rope_slot_tables(positions)
    pos_f = positions.astype(F32)
    pos_c, pos_r = pos_f[:, :, None], pos_f.reshape(b, s // tq_a, 1, tq_a)
    monotone = jnp.all(positions[:, 1:] >= positions[:, :-1])
    lane_bcast = lambda v: jnp.broadcast_to(v[..., None], v.shape + (LANES,)).astype(F32)
    slopes_a = lane_bcast(_alibi_slopes(A_HEADS)[:, None] * LOG2E)
    slopes_c = lane_bcast(_alibi_slopes(C_HEADS).reshape(C_KV_HEADS, C_REP))

    for l in range(depth):
        p = _layer_params(l, w_in, norm_mix_g, diff_qk_g, mla_cq_g, w_uq, mla_ckv_g, w_ukv, mla_qk_g, swa_qk_g,
                          w_branch_a, w_branch_b, w_branch_c, w_o, norm_ffn_g, w_router)
        qa, ka, va, qb, kb, vb, qc, kc, vc = _proj_call(x, cos_t, sin_t, p, tm_proj)
        lam_init = 0.8 - 0.6 * math.exp(-0.3 * l)
        attn_a = functools.partial(_attn_a_call, qa, ka, va, pos_c, pos_r, slopes_a, diff_lambda[l].astype(F32),
                                   diff_out_g[l].reshape(1, -1).astype(F32), lam_init, tq_a)
        oa = lax.cond(monotone, functools.partial(attn_a, True), functools.partial(attn_a, False))
        ob = _attn_b_call(qb, kb, vb, min(512, s))
        oc = _attn_c_call(qc, kc, vc, slopes_c, lane_bcast(swa_sink[l].reshape(C_KV_HEADS, C_REP)))
        xn, h2, aff_t = _merge_call(x, oa, ob, oc, p, tm_merge)
        pos, dest, start = _topk_call(aff_t, cap)
        starts = jnp.concatenate([start[:, 0, ::tt], jnp.full((b, 1), N_EXPERTS * cap, jnp.int32)], axis=1)
        table = h2.reshape(b * s, d // 2)
        x = None
        for grp in range(groups):
            b0, nb = grp * (b // groups), b // groups
            sl = slice(b0, b0 + nb)
            pos2 = pos[sl].reshape(nb * N_EXPERTS, s)
            rows, asel = _sc_gather_call(table, pos2, aff_t[sl].reshape(nb * N_EXPERTS, s), cap, b0)
            y = _expert_call(rows.reshape(nb, N_EXPERTS, cap, d // 2), asel.reshape(nb, N_EXPERTS, cap, SC_LANES),
                             w_exp_gate, w_exp_up, w_exp_down, l)
            y_rows, y_tok = _sc_regroup_call(y.reshape(nb * N_EXPERTS * cap, d // 2),
                                             dest[sl].reshape(nb * N_EXPERTS, s), pos2, cap)
            x = _combine_call(xn, starts[sl].reshape(-1), y_tok.reshape(nb, N_EXPERTS * cap),
                              y_rows.reshape(nb, N_EXPERTS * cap, d // 2), tt, min(256, N_EXPERTS * cap), b0, x)
    return x
```

```python
import functools
import math

import numpy as np
import jax
import jax.numpy as jnp
from jax import lax
from jax.experimental import pallas as pl
from jax.experimental.pallas import tpu as pltpu
from jax.experimental.pallas import tpu_sc as plsc

F32 = jnp.float32
BF16 = jnp.bfloat16

D_MODEL = 1024
EPS = 1e-6
A_HEADS = 4
A_DIM = 64
B_HEADS = 8
B_NOPE = 64
B_ROPE = 32
B_VDIM = 64
B_QK = B_NOPE + B_ROPE
B_Q_LORA = 384
B_KV_LORA = 256
ROPE_THETA = 10000.0
C_HEADS = 8
C_KV_HEADS = 2
C_REP = C_HEADS // C_KV_HEADS
C_DIM = 64
WINDOW = 128
N_EXPERTS = 16
EC_CAPACITY = 2
LOG2E = math.log2(math.e)
SC_CORES, SC_SUBCORES, SC_LANES = 2, 16, 16
SC_GATHER_ROWS = 128
SEG_TILE = 256
LANES = 128

_QA = 0
_KA = 512
_VA = 1024
_CQ = 1536
_CKV = 1920
_QC = 2176
_KC = 2688
_VC = 2944
_KR = 3200
_KRR = 3328
_PROJ_COLS = 3456

VMEM_LIMIT = 56 * 1024 * 1024

_NT = (((1,), (1,)), ((), ()))


def _cparams(sem):
    return pltpu.CompilerParams(dimension_semantics=sem, vmem_limit_bytes=VMEM_LIMIT)


def _dot(a, b):
    return jnp.dot(a, b, preferred_element_type=F32)


def _seg_sum(x2, e):
    hi = x2.astype(BF16)
    lo = (x2 - hi.astype(F32)).astype(BF16)
    e2 = jnp.concatenate([e, e], axis=0)
    slabs = [_dot(jnp.concatenate([hi[:, c:c + SEG_TILE], lo[:, c:c + SEG_TILE]], axis=1), e2)
             for c in range(0, x2.shape[1], SEG_TILE)]
    return slabs[0] if len(slabs) == 1 else jnp.concatenate(slabs, axis=1)


def _pack_rows(x):
    half = x.shape[1] // 2
    xb = x.astype(BF16).astype(F32)
    lo = pltpu.bitcast(xb[:, :half], jnp.uint32) >> 16
    hi = pltpu.bitcast(xb[:, half:], jnp.uint32) & jnp.uint32(0xFFFF0000)
    return pltpu.bitcast(lo | hi, F32)


def _unpack_rows(words):
    bits = pltpu.bitcast(words, jnp.uint32)
    return jnp.concatenate([pltpu.bitcast(bits << 16, F32), pltpu.bitcast(bits & jnp.uint32(0xFFFF0000), F32)],
                           axis=1).astype(BF16)


def _sigmoid(x):
    return 0.5 * jnp.tanh(0.5 * x) + 0.5


def _row_rms(x, g):
    return x * lax.rsqrt(jnp.mean(x * x, axis=-1, keepdims=True) + EPS) * g


def _proj_kernel(x_ref, gmix_ref, w_ref, wuq_ref, wukv_ref, e64_ref, e128_ref,
                 gqa_ref, gka_ref, gcq_ref, gckv_ref, gqb_ref, gkb_ref, gqc_ref, gkc_ref,
                 cos_ref, sin_ref,
                 qa_o, ka_o, va_o, qb_o, kb_o, vb_o, qc_o, kc_o, vc_o):
    hb = _row_rms(x_ref[0], gmix_ref[...]).astype(BF16)

    projected = _dot(hb, w_ref[...])

    def proj(a, n):
        return projected[:, a:a + n]

    e64 = e64_ref[...]
    e128 = e128_ref[...]

    def seg_norm(v, e, width, g):
        return v * lax.rsqrt(_seg_sum(v * v, e) * (1.0 / width) + EPS) * g

    def store_slots(o_ref, v, n):
        for j in range(n):
            o_ref[0, j] = v[:, LANES * j:LANES * (j + 1)].astype(o_ref.dtype)

    ones_slot = jnp.ones((hb.shape[0], LANES), F32)

    def store_value_slots(o_ref, v, n):
        for j in range(n):
            o_ref[0, j] = jnp.concatenate([v[:, LANES * j:LANES * (j + 1)], ones_slot], axis=1).astype(o_ref.dtype)

    store_slots(qa_o, seg_norm(proj(_QA, 512), e64, A_DIM, gqa_ref[...]), A_HEADS)
    store_slots(ka_o, seg_norm(proj(_KA, 512), e64, A_DIM, gka_ref[...]), A_HEADS)
    store_value_slots(va_o, proj(_VA, 512), A_HEADS)

    cos_t = cos_ref[0]
    sin_t = sin_ref[0]
    cos8 = jnp.concatenate([cos_t] * B_HEADS, axis=1)
    sin8 = jnp.concatenate([sin_t] * B_HEADS, axis=1)
    cq = _row_rms(proj(_CQ, B_Q_LORA), gcq_ref[...]).astype(BF16)
    q2 = _dot(cq, wuq_ref[...])
    qb = q2[:, :1024] * cos8 + q2[:, 1024:] * sin8
    store_slots(qb_o, seg_norm(qb, e128, B_QK, gqb_ref[...]), B_HEADS)
    ckv = _row_rms(proj(_CKV, B_KV_LORA), gckv_ref[...]).astype(BF16)
    kv = _dot(ckv, wukv_ref[...])
    kr = proj(_KR, LANES) * cos_t + proj(_KRR, LANES) * sin_t
    kb = kv[:, :1024] + jnp.concatenate([kr] * B_HEADS, axis=1)
    store_slots(kb_o, seg_norm(kb, e128, B_QK, gkb_ref[...]), B_HEADS)
    store_value_slots(vb_o, kv[:, 1024:], B_HEADS // 2)

    store_slots(qc_o, seg_norm(proj(_QC, 512), e64, C_DIM, gqc_ref[...]), C_HEADS // 2)
    store_slots(kc_o, seg_norm(proj(_KC, 256), e64, C_DIM, gkc_ref[...]), C_KV_HEADS)
    store_value_slots(vc_o, proj(_VC, 256), C_KV_HEADS)


def _proj_call(x, cos_t, sin_t, p, tm):
    b, s, d = x.shape
    full = lambda a: pl.BlockSpec(a.shape, lambda i, j: (0,) * a.ndim, pipeline_mode=pl.Buffered(1))
    slot = lambda nw: pl.BlockSpec((1, nw[0], tm, nw[1]), lambda i, j: (i, 0, j, 0))
    tok = lambda w: pl.BlockSpec((1, tm, w), lambda i, j: (i, j, 0))
    consts = [p["gmix"], p["w_in"], p["w_uq"], p["w_ukv"], p["e64"], p["e128"],
              p["gqa"], p["gka"], p["gcq"], p["gckv"], p["gqb"], p["gkb"], p["gqc"], p["gkc"]]
    slots = [(A_HEADS, LANES), (A_HEADS, LANES), (A_HEADS, 2 * LANES), (B_HEADS, LANES), (B_HEADS, LANES),
             (B_HEADS // 2, 2 * LANES), (C_HEADS // 2, LANES), (C_KV_HEADS, LANES), (C_KV_HEADS, 2 * LANES)]
    return pl.pallas_call(
        _proj_kernel,
        grid=(b, s // tm),
        in_specs=[tok(d)] + [full(a) for a in consts] + [tok(LANES), tok(LANES)],
        out_specs=[slot(nw) for nw in slots],
        out_shape=[jax.ShapeDtypeStruct((b, nw[0], s, nw[1]), BF16) for nw in slots],
        compiler_params=_cparams(("parallel", "parallel")),
        name="proj",
    )(x, *consts, cos_t, sin_t)


def _attn_a_kernel(q_ref, k_ref, v_ref, pc_ref, pr_ref, slope_ref, lam_ref, og_ref, o_ref, s_scr, *, lam_init, mono):
    t = pl.program_id(2)
    q = q_ref[0, 0]
    tq = q.shape[0]
    ck = tq
    nck = k_ref.shape[2] // ck
    lane = lax.broadcasted_iota(jnp.int32, q.shape, 1)
    zero = jnp.zeros_like(q)
    qm = [jnp.where(lane < A_DIM, q, zero), jnp.where(lane >= A_DIM, q, zero)]
    slope = slope_ref[0][:, :1]
    a = slope * pc_ref[0]
    a_lanes = jnp.broadcast_to(a, (tq, LANES))

    def lane_fold_max(x):
        out = x[:, :LANES]
        for j in range(1, ck // LANES):
            out = jnp.maximum(out, x[:, j * LANES:(j + 1) * LANES])
        return out

    mx = [jnp.full((tq, LANES), -jnp.inf, F32)] * 2
    for d in range(nck):
        if mono:
            c = t if d == 0 else lax.rem(t + d, nck)
            k_c = k_ref[0, 0, pl.ds(pl.multiple_of(c * ck, ck), ck), :]
        else:
            c = d
            k_c = k_ref[0, 0, d * ck:(d + 1) * ck, :]
        b = slope * pr_ref[0, c]
        if mono and d > 0:
            sign = jnp.where(t + d < nck, -1.0, 1.0)
            row_part, col_part = sign * b, sign * a_lanes
            for m in range(2):
                sc = lax.dot_general(qm[m], k_c, _NT, preferred_element_type=F32) + row_part
                s_scr[m, c] = sc
                mx[m] = jnp.maximum(mx[m], lane_fold_max(sc) - col_part)
        else:
            bias = jnp.abs(a - b)
            for m in range(2):
                sc = lax.dot_general(qm[m], k_c, _NT, preferred_element_type=F32) - bias
                s_scr[m, c] = sc
                mx[m] = jnp.maximum(mx[m], lane_fold_max(sc))
    v = v_ref[0, 0]
    acc = []
    for m in range(2):
        row_max = jnp.broadcast_to(jnp.max(mx[m], axis=-1, keepdims=True), (tq, LANES))
        es = []
        for c in range(nck):
            if mono:
                sign = jnp.where(c < t, 1.0, jnp.where(c > t, -1.0, 0.0))
                stab = row_max + sign * a_lanes
            else:
                stab = row_max
            es.append(jnp.exp2(s_scr[m, c] - jnp.concatenate([stab] * (ck // LANES), axis=1)).astype(BF16))
        e = es[0] if nck == 1 else jnp.concatenate(es, axis=1)
        acc.append(_dot(e, v))
    lp = lam_ref[...]
    lam = (jnp.exp(jnp.sum(lp[0:1] * lp[1:2], axis=-1, keepdims=True))
           - jnp.exp(jnp.sum(lp[2:3] * lp[3:4], axis=-1, keepdims=True)) + lam_init)
    o = (acc[0][:, :LANES] * (1.0 / acc[0][:, LANES:])
         - acc[1][:, :LANES] * (lam / acc[1][:, LANES:]))
    o_ref[0] = (_row_rms(o, og_ref[...]) * (1.0 - lam_init)).astype(o_ref.dtype)


def _attn_a_call(qa, ka, va, pos_c, pos_r, slopes, lam_p, out_g, lam_init, tq, mono):
    b, h, s, _ = qa.shape
    kv_spec = lambda w: pl.BlockSpec((1, 1, s, w), lambda i, j, t: (i, j, 0, 0))
    return pl.pallas_call(
        functools.partial(_attn_a_kernel, lam_init=lam_init, mono=mono),
        grid=(b, h, s // tq),
        scratch_shapes=[pltpu.VMEM((2, s // tq, tq, tq), F32)],
        in_specs=[pl.BlockSpec((1, 1, tq, LANES), lambda i, j, t: (i, j, t, 0)), kv_spec(LANES), kv_spec(2 * LANES),
                  pl.BlockSpec((1, tq, 1), lambda i, j, t: (i, t, 0)),
                  pl.BlockSpec((1, s // tq, 1, tq), lambda i, j, t: (i, 0, 0, 0)),
                  pl.BlockSpec((1, 1, LANES), lambda i, j, t: (j, 0, 0)),
                  pl.BlockSpec(lam_p.shape, lambda i, j, t: (0, 0)),
                  pl.BlockSpec(out_g.shape, lambda i, j, t: (0, 0))],
        out_specs=pl.BlockSpec((1, tq, LANES), lambda i, j, t: (i, t, j)),
        out_shape=jax.ShapeDtypeStruct((b, s, h * LANES), BF16),
        compiler_params=_cparams(("parallel", "parallel", "arbitrary")),
        name="attn_a",
    )(qa, ka, va, pos_c, pos_r, slopes, lam_p, out_g)


def _attn_b_kernel(q_ref, k_ref, v_ref, o_ref):
    tq = q_ref.shape[2]
    es = []
    for j in range(2):
        s = lax.dot_general(q_ref[0, j], k_ref[0, j], _NT, preferred_element_type=F32)
        es.append(jnp.exp2(s - jnp.max(s, axis=-1, keepdims=True)).astype(BF16))
    acc = _dot(jnp.concatenate(es, axis=0), v_ref[0, 0])
    lane = lax.broadcasted_iota(jnp.int32, (tq, LANES), 1)
    o_ref[0] = jnp.where(lane < B_VDIM, acc[:tq, :LANES] * (1.0 / acc[:tq, LANES:]),
                         acc[tq:, :LANES] * (1.0 / acc[tq:, LANES:])).astype(o_ref.dtype)


def _attn_b_call(qb, kb, vb, tq):
    b, h, s, _ = qb.shape
    return pl.pallas_call(
        _attn_b_kernel,
        grid=(b, h // 2, s // tq),
        in_specs=[pl.BlockSpec((1, 2, tq, LANES), lambda i, j, t: (i, j, t, 0)),
                  pl.BlockSpec((1, 2, s, LANES), lambda i, j, t: (i, j, 0, 0)),
                  pl.BlockSpec((1, 1, s, 2 * LANES), lambda i, j, t: (i, j, 0, 0))],
        out_specs=pl.BlockSpec((1, tq, LANES), lambda i, j, t: (i, t, j)),
        out_shape=jax.ShapeDtypeStruct((b, s, (h // 2) * LANES), BF16),
        compiler_params=_cparams(("parallel", "parallel", "arbitrary")),
        name="attn_b",
    )(qb, kb, vb)


def _attn_c_kernel(q_ref, kp_ref, ko_ref, kn_ref, vp_ref, vo_ref, vn_ref, slope_ref, sink_ref, o_ref, *, seq):
    w = WINDOW
    nsub = ko_ref.shape[2] // w
    n0 = pl.program_id(2) * nsub
    kcat = jnp.concatenate([kp_ref[0, 0], ko_ref[0, 0], kn_ref[0, 0]], axis=0)
    vcat = jnp.concatenate([vp_ref[0, 0], vo_ref[0, 0], vn_ref[0, 0]], axis=0)
    lane = lax.broadcasted_iota(jnp.int32, (w, LANES), 1)
    r_idx = lax.broadcasted_iota(jnp.int32, (w, 3 * w), 0)
    c_idx = lax.broadcasted_iota(jnp.int32, (w, 3 * w), 1)
    arel = jnp.abs(c_idx - w - r_idx)
    dist = arel.astype(F32)
    slopes = slope_ref[0]
    sinks = sink_ref[0]
    bias4 = jnp.concatenate([jnp.where(arel <= w, -slopes[r:r + 1, :1] * dist, -1e30) for r in range(C_REP)], axis=0)
    sink4 = jnp.concatenate([jnp.broadcast_to(sinks[r:r + 1, :], (w, LANES)) for r in range(C_REP)], axis=0)
    c_row = lax.broadcasted_iota(jnp.int32, (1, 3 * w), 1)
    scs = []
    for i in range(nsub):
        parts = []
        for p in range(2):
            q = q_ref[0, p, i * w:(i + 1) * w, :]
            zero = jnp.zeros_like(q)
            parts += [jnp.where(lane < C_DIM, q, zero), jnp.where(lane >= C_DIM, q, zero)]
        qz = jnp.concatenate(parts, axis=0)
        s = lax.dot_general(qz, kcat[i * w:(i + 3) * w], _NT, preferred_element_type=F32)
        kidx = (n0 + i - 1) * w + c_row
        edge = jnp.where(kidx >= 0, jnp.where(kidx < seq, 0.0, -1e30), -1e30)
        scs.append(s + bias4 + edge)
    sc = jnp.concatenate(scs, axis=0)
    sk = jnp.concatenate([sink4] * nsub, axis=0)
    m = jnp.maximum(jnp.broadcast_to(jnp.max(sc, axis=-1, keepdims=True), sk.shape), sk)
    e = jnp.exp(sc - jnp.concatenate([m] * 3, axis=1)).astype(BF16)
    tail = jnp.exp(sk - m)
    for i in range(nsub):
        rows = slice(i * 4 * w, (i + 1) * 4 * w)
        acc = _dot(e[rows], vcat[i * w:(i + 3) * w])
        o = acc[:, :LANES] * (1.0 / (acc[:, LANES:] + tail[rows]))
        pair0 = jnp.where(lane < C_DIM, o[0:w], o[w:2 * w])
        pair1 = jnp.where(lane < C_DIM, o[2 * w:3 * w], o[3 * w:4 * w])
        o_ref[0, i * w:(i + 1) * w, :] = jnp.concatenate([pair0, pair1], axis=1).astype(o_ref.dtype)


def _attn_c_call(qc, kc, vc, slopes, sinks):
    b, _, s, _ = qc.shape
    nb = s // WINDOW
    nsub = min(16, nb)
    tq = nsub * WINDOW
    prev = lambda wd: pl.BlockSpec((1, 1, WINDOW, wd), lambda i, g, n: (i, g, jnp.maximum(n * nsub - 1, 0), 0))
    own = lambda wd: pl.BlockSpec((1, 1, tq, wd), lambda i, g, n: (i, g, n, 0))
    nxt = lambda wd: pl.BlockSpec((1, 1, WINDOW, wd), lambda i, g, n: (i, g, jnp.minimum((n + 1) * nsub, nb - 1), 0))
    kw, vw = kc.shape[-1], vc.shape[-1]
    per_group = pl.BlockSpec((1, C_REP, LANES), lambda i, g, n: (g, 0, 0))
    return pl.pallas_call(
        functools.partial(_attn_c_kernel, seq=s),
        grid=(b, C_KV_HEADS, s // tq),
        in_specs=[pl.BlockSpec((1, 2, tq, LANES), lambda i, g, n: (i, g, n, 0)),
                  prev(kw), own(kw), nxt(kw), prev(vw), own(vw), nxt(vw), per_group, per_group],
        out_specs=pl.BlockSpec((1, tq, 2 * LANES), lambda i, g, n: (i, n, g)),
        out_shape=jax.ShapeDtypeStruct((b, s, C_HEADS * C_DIM), BF16),
        compiler_params=_cparams(("parallel", "parallel", "arbitrary")),
        name="attn_c",
    )(qc, kc, kc, kc, vc, vc, vc, slopes, sinks)


def _merge_kernel(x_ref, gmix_ref, wg_ref, oa_ref, ob_ref, oc_ref, wa_ref, wb_ref, wc_ref, wo_ref,
                  gffn_ref, wr_ref, xn_o, h2_o, aff_o):
    d = D_MODEL
    x = x_ref[0]
    hb = _row_rms(x, gmix_ref[...]).astype(BF16)
    g = _sigmoid(_dot(hb, wg_ref[...]))
    merged = (g[:, :d] * _dot(oa_ref[0], wa_ref[...]) + g[:, d:2 * d] * _dot(ob_ref[0], wb_ref[...])
              + g[:, 2 * d:] * _dot(oc_ref[0], wc_ref[...]))
    xn = x + _dot(merged.astype(BF16), wo_ref[...])
    xn_o[0] = xn
    h2 = _row_rms(xn, gffn_ref[...])
    h2_o[0] = _pack_rows(h2)
    logits = lax.dot_general(wr_ref[...], h2, _NT, preferred_element_type=F32,
                             precision=lax.Precision.HIGHEST)
    ex = jnp.exp(logits - jnp.max(logits, axis=0, keepdims=True))
    aff_o[0] = ex / jnp.sum(ex, axis=0, keepdims=True)


def _merge_call(x, oa, ob, oc, p, tm, batch0, nb):
    _, s, d = x.shape
    full = lambda a: pl.BlockSpec(a.shape, lambda i, j: (0,) * a.ndim, pipeline_mode=pl.Buffered(1))
    tok_in = lambda w: pl.BlockSpec((1, tm, w), lambda i, j: (i + batch0, j, 0))
    tok = lambda w: pl.BlockSpec((1, tm, w), lambda i, j: (i, j, 0))
    return pl.pallas_call(
        _merge_kernel,
        grid=(nb, s // tm),
        in_specs=[tok_in(d), full(p["gmix"]), full(p["w_gate"]), tok_in(512), tok_in(512), tok_in(512),
                  full(p["w_a"]), full(p["w_b"]), full(p["w_c"]), full(p["w_o"]), full(p["gffn"]), full(p["w_rt"])],
        out_specs=[tok(d), tok(d // 2), pl.BlockSpec((1, N_EXPERTS, tm), lambda i, j: (i, 0, j))],
        out_shape=[jax.ShapeDtypeStruct((nb, s, d), F32), jax.ShapeDtypeStruct((nb, s, d // 2), F32),
                   jax.ShapeDtypeStruct((nb, N_EXPERTS, s), F32)],
        compiler_params=_cparams(("parallel", "parallel")),
        name="merge",
    )(x, p["gmix"], p["w_gate"], oa, ob, oc, p["w_a"], p["w_b"], p["w_c"], p["w_o"], p["gffn"], p["w_rt"])


def _cumsum_lanes(mask01, chunk):
    rows, s = mask01.shape
    tri = jnp.where(lax.broadcasted_iota(jnp.int32, (chunk, chunk), 0)
                    <= lax.broadcasted_iota(jnp.int32, (chunk, chunk), 1), 1.0, 0.0).astype(BF16)
    carry = jnp.zeros((rows, 1), F32)
    outs = []
    for c in range(s // chunk):
        cs = _dot(mask01[:, c * chunk:(c + 1) * chunk], tri) + carry
        outs.append(cs)
        carry = cs[:, chunk - 1:chunk]
    return jnp.concatenate(outs, axis=1)


def _topk_kernel(aff_ref, pos_o, dest_o, start_o, *, cap, chunk):
    bits = pltpu.bitcast(aff_ref[0], jnp.int32)
    rows = bits.shape[0]
    capf = float(cap)

    def count(mask):
        return jnp.sum(jnp.where(mask, 1.0, 0.0), axis=1, keepdims=True)

    def body(_, c):
        lo, hi = c
        mid = lo + ((hi - lo + 1) >> 1)
        ok = count(bits >= mid) >= capf
        return jnp.where(ok, mid, lo), jnp.where(ok, hi, mid - 1)

    lo0 = jnp.zeros((rows, 1), jnp.int32)
    hi0 = jnp.full((rows, 1), 0x7F800000, jnp.int32)
    thr, _ = lax.fori_loop(0, 31, body, (lo0, hi0))
    gt = bits > thr
    eq = bits == thr
    need = capf - count(gt)
    eq_rank = _cumsum_lanes(jnp.where(eq, 1.0, 0.0).astype(BF16), chunk)
    sel = jnp.where(gt, 1.0, jnp.where(eq, jnp.where(eq_rank <= need, 1.0, 0.0), 0.0))
    sel_b = sel.astype(BF16)
    slot = _cumsum_lanes(sel_b, chunk) - 1.0
    pos_o[0] = jnp.where(sel > 0.0, slot, -1.0).astype(jnp.int32)
    per_token = jnp.broadcast_to(jnp.sum(sel, axis=0, keepdims=True), (8, sel.shape[1]))
    before_token = (_cumsum_lanes(per_token.astype(BF16), chunk) - per_token)[0:1]
    lower = jnp.where(lax.broadcasted_iota(jnp.int32, (rows, rows), 1)
                      < lax.broadcasted_iota(jnp.int32, (rows, rows), 0), 1.0, 0.0).astype(BF16)
    before_expert = _dot(lower, sel_b)
    dest_o[0] = jnp.where(sel > 0.0, before_token + before_expert, -1.0).astype(jnp.int32)
    start_o[0] = before_token.astype(jnp.int32)


def _topk_call(aff_t, cap):
    b, e, s = aff_t.shape
    spec = pl.BlockSpec((1, e, s), lambda i: (i, 0, 0))
    return pl.pallas_call(
        functools.partial(_topk_kernel, cap=cap, chunk=min(512, s)),
        grid=(b,),
        in_specs=[spec],
        out_specs=[spec, spec, pl.BlockSpec((1, 1, s), lambda i: (i, 0, 0))],
        out_shape=[jax.ShapeDtypeStruct((b, e, s), jnp.int32), jax.ShapeDtypeStruct((b, e, s), jnp.int32),
                   jax.ShapeDtypeStruct((b, 1, s), jnp.int32)],
        compiler_params=_cparams(("parallel",)),
        name="topk",
    )(aff_t)


def _sc_gather_call(table, pos2, aff2, cap, batch0):
    npairs, s = pos2.shape
    width = table.shape[1]
    workers = SC_CORES * SC_SUBCORES
    per_worker = npairs // workers
    nchunk = cap // SC_GATHER_ROWS
    assert npairs % workers == 0 and cap % SC_GATHER_ROWS == 0 and s % SC_LANES == 0
    mesh = plsc.VectorSubcoreMesh(core_axis_name="c", subcore_axis_name="s")

    @functools.partial(
        pl.kernel, mesh=mesh,
        out_type=[jax.ShapeDtypeStruct((npairs * cap, width), F32),
                  jax.ShapeDtypeStruct((npairs * cap * SC_LANES,), F32)],
        scratch_types=[pltpu.VMEM((s,), jnp.int32), pltpu.VMEM((s,), F32),
                       pltpu.VMEM((nchunk, SC_GATHER_ROWS), jnp.int32), pltpu.VMEM((cap,), F32),
                       pltpu.VMEM((cap * SC_LANES,), F32), pltpu.VMEM((SC_GATHER_ROWS, width), F32),
                       pltpu.SemaphoreType.DMA],
        compiler_params=pltpu.CompilerParams(needs_layout_passes=False),
        name="sc_gather",
    )
    def gather(table_hbm, pos_hbm, aff_hbm, rows_hbm, asel_hbm, pos_v, aff_v, idx_v, aslot_v, asplat_v, rows_v, sem):
        wid = lax.axis_index("s") * SC_CORES + lax.axis_index("c")
        lanes = lax.iota(jnp.int32, SC_LANES)

        @pl.loop(0, per_worker)
        def _(j):
            pair = wid * per_worker + j
            row0 = (pair // N_EXPERTS + batch0) * s
            pltpu.sync_copy(pos_hbm.at[pair], pos_v)
            pltpu.sync_copy(aff_hbm.at[pair], aff_v)

            @pl.loop(0, s // SC_LANES)
            def _(i):
                sl = pl.ds(i * SC_LANES, SC_LANES)
                slot = pos_v[sl]
                chosen = slot >= 0
                token_row = lanes + (i * SC_LANES + row0)
                plsc.store_scatter(idx_v, [slot >> (SC_GATHER_ROWS.bit_length() - 1), slot & (SC_GATHER_ROWS - 1)], token_row, mask=chosen)
                plsc.store_scatter(aslot_v, [slot], aff_v[sl], mask=chosen)

            @pl.loop(0, cap)
            def _(r):
                asplat_v[pl.ds(r * SC_LANES, SC_LANES)] = plsc.load_gather(
                    aslot_v, [jnp.full((SC_LANES,), r, jnp.int32)])

            pltpu.sync_copy(asplat_v, asel_hbm.at[pl.ds(pair * (cap * SC_LANES), cap * SC_LANES)])
            for c in range(nchunk):
                pltpu.async_copy(table_hbm.at[idx_v.at[c]], rows_v, sem).wait()
                pltpu.sync_copy(rows_v, rows_hbm.at[pl.ds(pair * cap + c * SC_GATHER_ROWS, SC_GATHER_ROWS)])

    return gather(table, pos2, aff2)


def _expert_kernel(xin_ref, asel_ref, wg_ref, wu_ref, wd_ref, y_o, wg_s, wu_s, wd_s):
    @pl.when(pl.program_id(1) == 0)
    def _():
        wg_s[...] = wg_ref[0, 0].astype(BF16)
        wu_s[...] = wu_ref[0, 0].astype(BF16)
        wd_s[...] = wd_ref[0, 0].astype(BF16)

    xin = _unpack_rows(xin_ref[0, 0])
    g = _dot(xin, wg_s[...])
    u = _dot(xin, wu_s[...])
    hid = (g * _sigmoid(g) * u).astype(BF16)
    y_o[0, 0] = _pack_rows(_dot(hid, wd_s[...]) * asel_ref[0, 0][:, :1])


def _expert_call(xin, asel, wg, wu, wd, layer):
    b, e, cap, _ = xin.shape
    d, f = wg.shape[-2:]
    tokens = lambda w: pl.BlockSpec((1, 1, cap, w), lambda j, i: (i, j, 0, 0))
    weight = lambda r, c: pl.BlockSpec((1, 1, r, c), lambda j, i: (layer, j, 0, 0))
    return pl.pallas_call(
        _expert_kernel,
        grid=(e, b),
        in_specs=[tokens(xin.shape[-1]), tokens(asel.shape[-1]), weight(d, f), weight(d, f), weight(f, d)],
        out_specs=tokens(d // 2),
        out_shape=jax.ShapeDtypeStruct((b, e, cap, d // 2), F32),
        scratch_shapes=[pltpu.VMEM((d, f), BF16), pltpu.VMEM((d, f), BF16), pltpu.VMEM((f, d), BF16)],
        compiler_params=_cparams(("parallel", "arbitrary")),
        name="expert",
    )(xin, asel, wg, wu, wd)


def _sc_regroup_call(y_rows, dest2, pos2, cap):
    npairs, s = dest2.shape
    width = y_rows.shape[1]
    workers = SC_CORES * SC_SUBCORES
    batches = npairs // N_EXPERTS
    per_batch = N_EXPERTS * cap
    split = workers // batches
    span = per_batch // split
    nchunk = span // SC_GATHER_ROWS
    assert workers % batches == 0 and per_batch % split == 0 and span % SC_GATHER_ROWS == 0 and s % SC_LANES == 0
    mesh = plsc.VectorSubcoreMesh(core_axis_name="c", subcore_axis_name="s")

    @functools.partial(
        pl.kernel, mesh=mesh,
        out_type=[jax.ShapeDtypeStruct((npairs * cap, width), F32),
                  jax.ShapeDtypeStruct((npairs * cap,), jnp.int32)],
        scratch_types=[pltpu.VMEM((s,), jnp.int32), pltpu.VMEM((s,), jnp.int32),
                       pltpu.VMEM((nchunk, SC_GATHER_ROWS), jnp.int32), pltpu.VMEM((span,), jnp.int32),
                       pltpu.VMEM((SC_GATHER_ROWS, width), F32), pltpu.SemaphoreType.DMA],
        compiler_params=pltpu.CompilerParams(needs_layout_passes=False),
        name="sc_regroup",
    )
    def regroup(y_hbm, dest_hbm, pos_hbm, rows_hbm, tok_hbm, dest_v, pos_v, src_v, tok_v, rows_v, sem):
        wid = lax.axis_index("s") * SC_CORES + lax.axis_index("c")
        batch = wid // split
        first = (wid % split) * span
        lanes = lax.iota(jnp.int32, SC_LANES)

        @pl.loop(0, N_EXPERTS)
        def _(e):
            pair = batch * N_EXPERTS + e
            pltpu.sync_copy(dest_hbm.at[pair], dest_v)
            pltpu.sync_copy(pos_hbm.at[pair], pos_v)

            @pl.loop(0, s // SC_LANES)
            def _(i):
                sl = pl.ds(i * SC_LANES, SC_LANES)
                local = dest_v[sl] - first
                mine = (local >= 0) & (local < span)
                plsc.store_scatter(src_v, [local >> (SC_GATHER_ROWS.bit_length() - 1), local & (SC_GATHER_ROWS - 1)],
                                   pos_v[sl] + pair * cap, mask=mine)
                plsc.store_scatter(tok_v, [local], lanes + i * SC_LANES, mask=mine)

        out0 = batch * per_batch + first
        pltpu.sync_copy(tok_v, tok_hbm.at[pl.ds(out0, span)])
        for c in range(nchunk):
            pltpu.async_copy(y_hbm.at[src_v.at[c]], rows_v, sem).wait()
            pltpu.sync_copy(rows_v, rows_hbm.at[pl.ds(out0 + c * SC_GATHER_ROWS, SC_GATHER_ROWS)])

    return regroup(y_rows, dest2, pos2)


def _combine_kernel(start_ref, x_ref, tok_ref, rows_ref, *rest, ntile, ck):
    o_ref = rest[-1]
    b, j = pl.program_id(0), pl.program_id(1)
    tt = x_ref.shape[1]
    lo = start_ref[b * (ntile + 1) + j]
    hi = start_ref[b * (ntile + 1) + j + 1]
    c_lo = lo // ck
    c_hi = jnp.where(hi > lo, (hi - 1) // ck + 1, c_lo)
    tokens = j * tt + lax.broadcasted_iota(jnp.int32, (tt, ck), 0)
    o_ref[0] = x_ref[0]

    def body(c, carry):
        onehot = jnp.where(tok_ref[0, c] == tokens, 1.0, 0.0).astype(BF16)
        o_ref[0] += _dot(onehot, _unpack_rows(rows_ref[0, c]))
        return carry

    lax.fori_loop(c_lo, c_hi, body, 0)


def _combine_call(xn, starts, tok, rows, tt, ck, batch, batch0, earlier):
    nb, s, d = xn.shape
    nchunk = tok.shape[1] // ck
    ntile = s // tt
    tile = pl.BlockSpec((1, tt, d), lambda i, t, st: (i + batch0, t, 0))
    in_specs = [pl.BlockSpec((1, tt, d), lambda i, t, st: (i, t, 0)),
                pl.BlockSpec((1, nchunk, 1, ck), lambda i, t, st: (i, 0, 0, 0)),
                pl.BlockSpec((1, nchunk, ck, rows.shape[-1]), lambda i, t, st: (i, 0, 0, 0))]
    operands = [starts, xn, tok.reshape(nb, nchunk, 1, ck), rows.reshape(nb, nchunk, ck, rows.shape[-1])]
    aliases = {}
    if earlier is not None:
        in_specs.append(pl.BlockSpec(memory_space=pl.ANY))
        operands.append(earlier)
        aliases = {len(operands) - 1: 0}
    return pl.pallas_call(
        functools.partial(_combine_kernel, ntile=ntile, ck=ck),
        grid_spec=pltpu.PrefetchScalarGridSpec(
            num_scalar_prefetch=1, grid=(nb, ntile), in_specs=in_specs, out_specs=tile),
        out_shape=jax.ShapeDtypeStruct((batch, s, d), F32),
        input_output_aliases=aliases,
        compiler_params=_cparams(("parallel", "arbitrary")),
        name="combine",
    )(*operands)


def _block_diag(n, blk):
    i = np.arange(n)
    return jnp.asarray((i[:, None] // blk) == (i[None, :] // blk), dtype=BF16)


def _head_slots(w, heads, width):
    r = w.shape[0]
    return jnp.pad(w.reshape(r, heads, width), ((0, 0), (0, 0), (0, LANES - width))).reshape(r, heads * LANES)


def _rotate_half_cols(w):
    half = B_ROPE // 2
    return jnp.concatenate([-w[..., half:], w[..., :half]], axis=-1)


def _layer_params(l, w_in, norm_mix_g, diff_qk_g, mla_cq_g, w_uq, mla_ckv_g, w_ukv, mla_qk_g, swa_qk_g,
                  w_branch_a, w_branch_b, w_branch_c, w_o, norm_ffn_g, w_router):
    d = D_MODEL
    wi = w_in[l]
    off = np.cumsum([0, 512, 512, 512, B_Q_LORA, B_KV_LORA, B_ROPE, 512, 128, 128, 3 * d])
    piece = lambda k: wi[:, off[k]:off[k + 1]]
    maps_to_heads = lambda w: w.reshape(d, 2, A_HEADS, A_DIM).transpose(0, 2, 1, 3).reshape(d, 512)
    dup = lambda w: jnp.concatenate([w.reshape(d, C_KV_HEADS, 1, C_DIM)] * 2, axis=2).reshape(d, 256)
    kr = piece(5)
    rope_slot = lambda w: jnp.pad(w, ((0, 0), (B_NOPE, LANES - B_QK)))
    w_proj = jnp.concatenate(
        [maps_to_heads(piece(0)), maps_to_heads(piece(1)), piece(2), piece(3), piece(4), piece(6),
         dup(piece(7)), dup(piece(8)), rope_slot(kr), rope_slot(_rotate_half_cols(kr))], axis=1).astype(BF16)
    assert w_proj.shape == (d, _PROJ_COLS)

    wq = w_uq[l].reshape(B_Q_LORA, B_HEADS, B_QK)
    wq_rot = jnp.concatenate([jnp.zeros_like(wq[..., :B_NOPE]), _rotate_half_cols(wq[..., B_NOPE:])], axis=-1)
    w_uq_x = jnp.concatenate([_head_slots(wq.reshape(B_Q_LORA, -1), B_HEADS, B_QK),
                              _head_slots(wq_rot.reshape(B_Q_LORA, -1), B_HEADS, B_QK)], axis=1).astype(BF16)
    wkv = w_ukv[l].reshape(B_KV_LORA, B_HEADS, B_NOPE + B_VDIM)
    w_ukv_x = jnp.concatenate([_head_slots(wkv[..., :B_NOPE].reshape(B_KV_LORA, -1), B_HEADS, B_NOPE),
                               wkv[..., B_NOPE:].reshape(B_KV_LORA, -1)], axis=1).astype(BF16)

    row = lambda v: v.reshape(1, -1).astype(F32)
    slot_gain = lambda g: jnp.tile(jnp.pad(g, (0, LANES - B_QK)), B_HEADS)
    return {
        "gmix": row(norm_mix_g[l]), "w_in": w_proj, "w_uq": w_uq_x, "w_ukv": w_ukv_x,
        "e64": _block_diag(SEG_TILE, 64), "e128": _block_diag(SEG_TILE, LANES),
        "gqa": row(jnp.tile(diff_qk_g[l, 0], 8) * (A_DIM ** -0.5 * LOG2E)), "gka": row(jnp.tile(diff_qk_g[l, 1], 8)),
        "gcq": row(mla_cq_g[l]), "gckv": row(mla_ckv_g[l]),
        "gqb": row(slot_gain(mla_qk_g[l, 0]) * (B_QK ** -0.5 * LOG2E)), "gkb": row(slot_gain(mla_qk_g[l, 1])),
        "gqc": row(jnp.tile(swa_qk_g[l, 0], 8) * (C_DIM ** -0.5)), "gkc": row(jnp.tile(swa_qk_g[l, 1], 4)),
        "w_gate": piece(9).astype(BF16),
        "w_a": w_branch_a[l].astype(BF16), "w_b": w_branch_b[l].astype(BF16), "w_c": w_branch_c[l].astype(BF16),
        "w_o": w_o[l].astype(BF16), "gffn": row(norm_ffn_g[l]), "w_rt": w_router[l].T.astype(F32),
    }


def _rope_slot_tables(positions):
    inv = 1.0 / (ROPE_THETA ** (jnp.arange(0, B_ROPE, 2, dtype=F32) / B_ROPE))
    ang = positions.astype(F32)[..., None] * inv
    cos, sin = lax.optimization_barrier((jnp.cos(ang), jnp.sin(ang)))
    ones = jnp.ones(ang.shape[:-1] + (B_NOPE,), F32)
    pad = jnp.zeros(ang.shape[:-1] + (LANES - B_QK,), F32)
    return (jnp.concatenate([ones, cos, cos, pad], axis=-1),
            jnp.concatenate([jnp.zeros_like(ones), sin, sin, pad], axis=-1))


def _alibi_slopes(n):
    return 2.0 ** (-8.0 * jnp.arange(1, n + 1, dtype=F32) / n)


def kernel(x, positions, norm_mix_g, w_in, diff_qk_g, diff_lambda, diff_out_g, mla_cq_g, w_uq, mla_ckv_g, w_ukv,
           mla_qk_g, swa_qk_g, swa_sink, w_branch_a, w_branch_b, w_branch_c, w_o, norm_ffn_g, w_router,
           w_exp_gate, w_exp_up, w_exp_down):
    b, s, d = x.shape
    depth = w_in.shape[0]
    cap = max(1, EC_CAPACITY * s // N_EXPERTS)
    tm_proj = min(512, s)
    tq = min(256, s)
    tq_a = min(512, s)
    tm_merge = min(512, s)
    tt = min(256, s)
    sc_workers = SC_CORES * SC_SUBCORES
    groups = 2 if (b % 2 == 0 and (b // 2) * N_EXPERTS % sc_workers == 0 and sc_workers % (b // 2) == 0) else 1

    cos_t, sin_t = _rope_slot_tables(positions)
    pos_f = positions.astype(F32)
    pos_c, pos_r = pos_f[:, :, None], pos_f.reshape(b, s // tq_a, 1, tq_a)
    monotone = jnp.all(positions[:, 1:] >= positions[:, :-1])
    lane_bcast = lambda v: jnp.broadcast_to(v[..., None], v.shape + (LANES,)).astype(F32)
    slopes_a = lane_bcast(_alibi_slopes(A_HEADS)[:, None] * LOG2E)
    slopes_c = lane_bcast(_alibi_slopes(C_HEADS).reshape(C_KV_HEADS, C_REP))

    for l in range(depth):
        p = _layer_params(l, w_in, norm_mix_g, diff_qk_g, mla_cq_g, w_uq, mla_ckv_g, w_ukv, mla_qk_g, swa_qk_g,
                          w_branch_a, w_branch_b, w_branch_c, w_o, norm_ffn_g, w_router)
        qa, ka, va, qb, kb, vb, qc, kc, vc = _proj_call(x, cos_t, sin_t, p, tm_proj)
        lam_init = 0.8 - 0.6 * math.exp(-0.3 * l)
        attn_a = functools.partial(_attn_a_call, qa, ka, va, pos_c, pos_r, slopes_a, diff_lambda[l].astype(F32),
                                   diff_out_g[l].reshape(1, -1).astype(F32), lam_init, tq_a)
        oa = lax.cond(monotone, functools.partial(attn_a, True), functools.partial(attn_a, False))
        ob = _attn_b_call(qb, kb, vb, min(512, s))
        oc = _attn_c_call(qc, kc, vc, slopes_c, lane_bcast(swa_sink[l].reshape(C_KV_HEADS, C_REP)))
        x_in, x = x, None
        nb = b // groups
        for grp in range(groups):
            b0 = grp * nb
            xn, h2, aff_t = _merge_call(x_in, oa, ob, oc, p, tm_merge, b0, nb)
            pos, dest, start = _topk_call(aff_t, cap)
            starts = jnp.concatenate([start[:, 0, ::tt], jnp.full((nb, 1), N_EXPERTS * cap, jnp.int32)], axis=1)
            pos2 = pos.reshape(nb * N_EXPERTS, s)
            rows, asel = _sc_gather_call(h2.reshape(nb * s, d // 2), pos2, aff_t.reshape(nb * N_EXPERTS, s), cap, 0)
            y = _expert_call(rows.reshape(nb, N_EXPERTS, cap, d // 2), asel.reshape(nb, N_EXPERTS, cap, SC_LANES),
                             w_exp_gate, w_exp_up, w_exp_down, l)
            y_rows, y_tok = _sc_regroup_call(y.reshape(nb * N_EXPERTS * cap, d // 2),
                                             dest.reshape(nb * N_EXPERTS, s), pos2, cap)
            x = _combine_call(xn, starts.reshape(-1), y_tok.reshape(nb, N_EXPERTS * cap),
                              y_rows.reshape(nb, N_EXPERTS * cap, d // 2), tt, min(256, N_EXPERTS * cap), b, b0, x)
    return x
```

```python
import functools
import math

import numpy as np
import jax
import jax.numpy as jnp
from jax import lax
from jax.experimental import pallas as pl
from jax.experimental.pallas import tpu as pltpu
from jax.experimental.pallas import tpu_sc as plsc

F32 = jnp.float32
BF16 = jnp.bfloat16

D_MODEL = 1024
EPS = 1e-6
A_HEADS = 4
A_DIM = 64
B_HEADS = 8
B_NOPE = 64
B_ROPE = 32
B_VDIM = 64
B_QK = B_NOPE + B_ROPE
B_Q_LORA = 384
B_KV_LORA = 256
ROPE_THETA = 10000.0
C_HEADS = 8
C_KV_HEADS = 2
C_REP = C_HEADS // C_KV_HEADS
C_DIM = 64
WINDOW = 128
N_EXPERTS = 16
EC_CAPACITY = 2
LOG2E = math.log2(math.e)
SC_CORES, SC_SUBCORES, SC_LANES = 2, 16, 16
SC_GATHER_ROWS = 128
SEG_TILE = 256
LANES = 128

_QA = 0
_KA = 512
_VA = 1024
_CQ = 1536
_CKV = 1920
_QC = 2176
_KC = 2688
_VC = 2944
_KR = 3200
_KRR = 3328
_PROJ_COLS = 3456

VMEM_LIMIT = 56 * 1024 * 1024

_NT = (((1,), (1,)), ((), ()))


def _cparams(sem):
    return pltpu.CompilerParams(dimension_semantics=sem, vmem_limit_bytes=VMEM_LIMIT)


def _dot(a, b):
    return jnp.dot(a, b, preferred_element_type=F32)


def _seg_sum(x2, e):
    hi = x2.astype(BF16)
    lo = (x2 - hi.astype(F32)).astype(BF16)
    e2 = jnp.concatenate([e, e], axis=0)
    slabs = [_dot(jnp.concatenate([hi[:, c:c + SEG_TILE], lo[:, c:c + SEG_TILE]], axis=1), e2)
             for c in range(0, x2.shape[1], SEG_TILE)]
    return slabs[0] if len(slabs) == 1 else jnp.concatenate(slabs, axis=1)


def _pack_rows(x):
    half = x.shape[1] // 2
    xb = x.astype(BF16).astype(F32)
    lo = pltpu.bitcast(xb[:, :half], jnp.uint32) >> 16
    hi = pltpu.bitcast(xb[:, half:], jnp.uint32) & jnp.uint32(0xFFFF0000)
    return pltpu.bitcast(lo | hi, F32)


def _unpack_rows(words):
    bits = pltpu.bitcast(words, jnp.uint32)
    return jnp.concatenate([pltpu.bitcast(bits << 16, F32), pltpu.bitcast(bits & jnp.uint32(0xFFFF0000), F32)],
                           axis=1).astype(BF16)


def _sigmoid(x):
    return 0.5 * jnp.tanh(0.5 * x) + 0.5


def _row_rms(x, g):
    return x * lax.rsqrt(jnp.mean(x * x, axis=-1, keepdims=True) + EPS) * g


def _proj_kernel(x_ref, gmix_ref, w_ref, wuq_ref, wukv_ref, e64_ref, e128_ref,
                 gqa_ref, gka_ref, gcq_ref, gckv_ref, gqb_ref, gkb_ref, gqc_ref, gkc_ref,
                 cos_ref, sin_ref,
                 qa_o, ka_o, va_o, qb_o, kb_o, vb_o, qc_o, kc_o, vc_o):
    hb = _row_rms(x_ref[0], gmix_ref[...]).astype(BF16)

    projected = _dot(hb, w_ref[...])

    def proj(a, n):
        return projected[:, a:a + n]

    e64 = e64_ref[...]
    e128 = e128_ref[...]

    def seg_norm(v, e, width, g):
        return v * lax.rsqrt(_seg_sum(v * v, e) * (1.0 / width) + EPS) * g

    def store_slots(o_ref, v, n):
        for j in range(n):
            o_ref[0, j] = v[:, LANES * j:LANES * (j + 1)].astype(o_ref.dtype)

    ones_slot = jnp.ones((hb.shape[0], LANES), F32)

    def store_value_slots(o_ref, v, n):
        for j in range(n):
            o_ref[0, j] = jnp.concatenate([v[:, LANES * j:LANES * (j + 1)], ones_slot], axis=1).astype(o_ref.dtype)

    store_slots(qa_o, seg_norm(proj(_QA, 512), e64, A_DIM, gqa_ref[...]), A_HEADS)
    store_slots(ka_o, seg_norm(proj(_KA, 512), e64, A_DIM, gka_ref[...]), A_HEADS)
    store_value_slots(va_o, proj(_VA, 512), A_HEADS)

    cos_t = cos_ref[0]
    sin_t = sin_ref[0]
    cos8 = jnp.concatenate([cos_t] * B_HEADS, axis=1)
    sin8 = jnp.concatenate([sin_t] * B_HEADS, axis=1)
    cq = _row_rms(proj(_CQ, B_Q_LORA), gcq_ref[...]).astype(BF16)
    q2 = _dot(cq, wuq_ref[...])
    qb = q2[:, :1024] * cos8 + q2[:, 1024:] * sin8
    store_slots(qb_o, seg_norm(qb, e128, B_QK, gqb_ref[...]), B_HEADS)
    ckv = _row_rms(proj(_CKV, B_KV_LORA), gckv_ref[...]).astype(BF16)
    kv = _dot(ckv, wukv_ref[...])
    kr = proj(_KR, LANES) * cos_t + proj(_KRR, LANES) * sin_t
    kb = kv[:, :1024] + jnp.concatenate([kr] * B_HEADS, axis=1)
    store_slots(kb_o, seg_norm(kb, e128, B_QK, gkb_ref[...]), B_HEADS)
    store_value_slots(vb_o, kv[:, 1024:], B_HEADS // 2)

    store_slots(qc_o, seg_norm(proj(_QC, 512), e64, C_DIM, gqc_ref[...]), C_HEADS // 2)
    store_slots(kc_o, seg_norm(proj(_KC, 256), e64, C_DIM, gkc_ref[...]), C_KV_HEADS)
    store_value_slots(vc_o, proj(_VC, 256), C_KV_HEADS)


def _proj_call(x, cos_t, sin_t, p, tm):
    b, s, d = x.shape
    full = lambda a: pl.BlockSpec(a.shape, lambda i, j: (0,) * a.ndim, pipeline_mode=pl.Buffered(1))
    slot = lambda nw: pl.BlockSpec((1, nw[0], tm, nw[1]), lambda i, j: (i, 0, j, 0))
    tok = lambda w: pl.BlockSpec((1, tm, w), lambda i, j: (i, j, 0))
    consts = [p["gmix"], p["w_in"], p["w_uq"], p["w_ukv"], p["e64"], p["e128"],
              p["gqa"], p["gka"], p["gcq"], p["gckv"], p["gqb"], p["gkb"], p["gqc"], p["gkc"]]
    slots = [(A_HEADS, LANES), (A_HEADS, LANES), (A_HEADS, 2 * LANES), (B_HEADS, LANES), (B_HEADS, LANES),
             (B_HEADS // 2, 2 * LANES), (C_HEADS // 2, LANES), (C_KV_HEADS, LANES), (C_KV_HEADS, 2 * LANES)]
    return pl.pallas_call(
        _proj_kernel,
        grid=(b, s // tm),
        in_specs=[tok(d)] + [full(a) for a in consts] + [tok(LANES), tok(LANES)],
        out_specs=[slot(nw) for nw in slots],
        out_shape=[jax.ShapeDtypeStruct((b, nw[0], s, nw[1]), BF16) for nw in slots],
        compiler_params=_cparams(("parallel", "parallel")),
        name="proj",
    )(x, *consts, cos_t, sin_t)


def _attn_a_kernel(q_ref, k_ref, v_ref, pc_ref, pr_ref, slope_ref, lam_ref, og_ref, o_ref, s_scr, *, lam_init, mono):
    t = pl.program_id(2)
    q = q_ref[0, 0]
    tq = q.shape[0]
    ck = tq
    nck = k_ref.shape[2] // ck
    lane = lax.broadcasted_iota(jnp.int32, q.shape, 1)
    zero = jnp.zeros_like(q)
    qm = [jnp.where(lane < A_DIM, q, zero), jnp.where(lane >= A_DIM, q, zero)]
    slope = slope_ref[0][:, :1]
    a = slope * pc_ref[0]
    a_lanes = jnp.broadcast_to(a, (tq, LANES))

    def lane_fold_max(x):
        out = x[:, :LANES]
        for j in range(1, ck // LANES):
            out = jnp.maximum(out, x[:, j * LANES:(j + 1) * LANES])
        return out

    mx = [jnp.full((tq, LANES), -jnp.inf, F32)] * 2
    for d in range(nck):
        if mono:
            c = t if d == 0 else lax.rem(t + d, nck)
            k_c = k_ref[0, 0, pl.ds(pl.multiple_of(c * ck, ck), ck), :]
        else:
            c = d
            k_c = k_ref[0, 0, d * ck:(d + 1) * ck, :]
        b = slope * pr_ref[0, c]
        if mono and d > 0:
            sign = jnp.where(t + d < nck, -1.0, 1.0)
            row_part, col_part = sign * b, sign * a_lanes
            for m in range(2):
                sc = lax.dot_general(qm[m], k_c, _NT, preferred_element_type=F32) + row_part
                s_scr[m, c] = sc
                mx[m] = jnp.maximum(mx[m], lane_fold_max(sc) - col_part)
        else:
            bias = jnp.abs(a - b)
            for m in range(2):
                sc = lax.dot_general(qm[m], k_c, _NT, preferred_element_type=F32) - bias
                s_scr[m, c] = sc
                mx[m] = jnp.maximum(mx[m], lane_fold_max(sc))
    v = v_ref[0, 0]
    acc = []
    for m in range(2):
        row_max = jnp.broadcast_to(jnp.max(mx[m], axis=-1, keepdims=True), (tq, LANES))
        es = []
        for c in range(nck):
            if mono:
                sign = jnp.where(c < t, 1.0, jnp.where(c > t, -1.0, 0.0))
                stab = row_max + sign * a_lanes
            else:
                stab = row_max
            es.append(jnp.exp2(s_scr[m, c] - jnp.concatenate([stab] * (ck // LANES), axis=1)).astype(BF16))
        e = es[0] if nck == 1 else jnp.concatenate(es, axis=1)
        acc.append(_dot(e, v))
    lp = lam_ref[...]
    lam = (jnp.exp(jnp.sum(lp[0:1] * lp[1:2], axis=-1, keepdims=True))
           - jnp.exp(jnp.sum(lp[2:3] * lp[3:4], axis=-1, keepdims=True)) + lam_init)
    o = (acc[0][:, :LANES] * (1.0 / acc[0][:, LANES:])
         - acc[1][:, :LANES] * (lam / acc[1][:, LANES:]))
    o_ref[0] = (_row_rms(o, og_ref[...]) * (1.0 - lam_init)).astype(o_ref.dtype)


def _attn_a_call(qa, ka, va, pos_c, pos_r, slopes, lam_p, out_g, lam_init, tq, mono):
    b, h, s, _ = qa.shape
    kv_spec = lambda w: pl.BlockSpec((1, 1, s, w), lambda i, j, t: (i, j, 0, 0))
    return pl.pallas_call(
        functools.partial(_attn_a_kernel, lam_init=lam_init, mono=mono),
        grid=(b, h, s // tq),
        scratch_shapes=[pltpu.VMEM((2, s // tq, tq, tq), F32)],
        in_specs=[pl.BlockSpec((1, 1, tq, LANES), lambda i, j, t: (i, j, t, 0)), kv_spec(LANES), kv_spec(2 * LANES),
                  pl.BlockSpec((1, tq, 1), lambda i, j, t: (i, t, 0)),
                  pl.BlockSpec((1, s // tq, 1, tq), lambda i, j, t: (i, 0, 0, 0)),
                  pl.BlockSpec((1, 1, LANES), lambda i, j, t: (j, 0, 0)),
                  pl.BlockSpec(lam_p.shape, lambda i, j, t: (0, 0)),
                  pl.BlockSpec(out_g.shape, lambda i, j, t: (0, 0))],
        out_specs=pl.BlockSpec((1, tq, LANES), lambda i, j, t: (i, t, j)),
        out_shape=jax.ShapeDtypeStruct((b, s, h * LANES), BF16),
        compiler_params=_cparams(("parallel", "parallel", "arbitrary")),
        name="attn_a",
    )(qa, ka, va, pos_c, pos_r, slopes, lam_p, out_g)


def _attn_b_kernel(q_ref, k_ref, v_ref, o_ref):
    tq = q_ref.shape[2]
    es = []
    for j in range(2):
        s = lax.dot_general(q_ref[0, j], k_ref[0, j], _NT, preferred_element_type=F32)
        es.append(jnp.exp2(s - jnp.max(s, axis=-1, keepdims=True)).astype(BF16))
    acc = _dot(jnp.concatenate(es, axis=0), v_ref[0, 0])
    lane = lax.broadcasted_iota(jnp.int32, (tq, LANES), 1)
    o_ref[0] = jnp.where(lane < B_VDIM, acc[:tq, :LANES] * (1.0 / acc[:tq, LANES:]),
                         acc[tq:, :LANES] * (1.0 / acc[tq:, LANES:])).astype(o_ref.dtype)


def _attn_b_call(qb, kb, vb, tq):
    b, h, s, _ = qb.shape
    return pl.pallas_call(
        _attn_b_kernel,
        grid=(b, h // 2, s // tq),
        in_specs=[pl.BlockSpec((1, 2, tq, LANES), lambda i, j, t: (i, j, t, 0)),
                  pl.BlockSpec((1, 2, s, LANES), lambda i, j, t: (i, j, 0, 0)),
                  pl.BlockSpec((1, 1, s, 2 * LANES), lambda i, j, t: (i, j, 0, 0))],
        out_specs=pl.BlockSpec((1, tq, LANES), lambda i, j, t: (i, t, j)),
        out_shape=jax.ShapeDtypeStruct((b, s, (h // 2) * LANES), BF16),
        compiler_params=_cparams(("parallel", "parallel", "arbitrary")),
        name="attn_b",
    )(qb, kb, vb)


def _attn_c_kernel(q_ref, kp_ref, ko_ref, kn_ref, vp_ref, vo_ref, vn_ref, slope_ref, sink_ref, o_ref, *, seq):
    w = WINDOW
    nsub = ko_ref.shape[2] // w
    n0 = pl.program_id(2) * nsub
    kcat = jnp.concatenate([kp_ref[0, 0], ko_ref[0, 0], kn_ref[0, 0]], axis=0)
    vcat = jnp.concatenate([vp_ref[0, 0], vo_ref[0, 0], vn_ref[0, 0]], axis=0)
    lane = lax.broadcasted_iota(jnp.int32, (w, LANES), 1)
    r_idx = lax.broadcasted_iota(jnp.int32, (w, 3 * w), 0)
    c_idx = lax.broadcasted_iota(jnp.int32, (w, 3 * w), 1)
    arel = jnp.abs(c_idx - w - r_idx)
    dist = arel.astype(F32)
    slopes = slope_ref[0]
    sinks = sink_ref[0]
    bias4 = jnp.concatenate([jnp.where(arel <= w, -slopes[r:r + 1, :1] * dist, -1e30) for r in range(C_REP)], axis=0)
    sink4 = jnp.concatenate([jnp.broadcast_to(sinks[r:r + 1, :], (w, LANES)) for r in range(C_REP)], axis=0)
    c_row = lax.broadcasted_iota(jnp.int32, (1, 3 * w), 1)
    scs = []
    for i in range(nsub):
        parts = []
        for p in range(2):
            q = q_ref[0, p, i * w:(i + 1) * w, :]
            zero = jnp.zeros_like(q)
            parts += [jnp.where(lane < C_DIM, q, zero), jnp.where(lane >= C_DIM, q, zero)]
        qz = jnp.concatenate(parts, axis=0)
        s = lax.dot_general(qz, kcat[i * w:(i + 3) * w], _NT, preferred_element_type=F32)
        kidx = (n0 + i - 1) * w + c_row
        edge = jnp.where(kidx >= 0, jnp.where(kidx < seq, 0.0, -1e30), -1e30)
        scs.append(s + bias4 + edge)
    sc = jnp.concatenate(scs, axis=0)
    sk = jnp.concatenate([sink4] * nsub, axis=0)
    m = jnp.maximum(jnp.broadcast_to(jnp.max(sc, axis=-1, keepdims=True), sk.shape), sk)
    e = jnp.exp(sc - jnp.concatenate([m] * 3, axis=1)).astype(BF16)
    tail = jnp.exp(sk - m)
    for i in range(nsub):
        rows = slice(i * 4 * w, (i + 1) * 4 * w)
        acc = _dot(e[rows], vcat[i * w:(i + 3) * w])
        o = acc[:, :LANES] * (1.0 / (acc[:, LANES:] + tail[rows]))
        pair0 = jnp.where(lane < C_DIM, o[0:w], o[w:2 * w])
        pair1 = jnp.where(lane < C_DIM, o[2 * w:3 * w], o[3 * w:4 * w])
        o_ref[0, i * w:(i + 1) * w, :] = jnp.concatenate([pair0, pair1], axis=1).astype(o_ref.dtype)


def _attn_c_call(qc, kc, vc, slopes, sinks):
    b, _, s, _ = qc.shape
    nb = s // WINDOW
    nsub = min(16, nb)
    tq = nsub * WINDOW
    prev = lambda wd: pl.BlockSpec((1, 1, WINDOW, wd), lambda i, g, n: (i, g, jnp.maximum(n * nsub - 1, 0), 0))
    own = lambda wd: pl.BlockSpec((1, 1, tq, wd), lambda i, g, n: (i, g, n, 0))
    nxt = lambda wd: pl.BlockSpec((1, 1, WINDOW, wd), lambda i, g, n: (i, g, jnp.minimum((n + 1) * nsub, nb - 1), 0))
    kw, vw = kc.shape[-1], vc.shape[-1]
    per_group = pl.BlockSpec((1, C_REP, LANES), lambda i, g, n: (g, 0, 0))
    return pl.pallas_call(
        functools.partial(_attn_c_kernel, seq=s),
        grid=(b, C_KV_HEADS, s // tq),
        in_specs=[pl.BlockSpec((1, 2, tq, LANES), lambda i, g, n: (i, g, n, 0)),
                  prev(kw), own(kw), nxt(kw), prev(vw), own(vw), nxt(vw), per_group, per_group],
        out_specs=pl.BlockSpec((1, tq, 2 * LANES), lambda i, g, n: (i, n, g)),
        out_shape=jax.ShapeDtypeStruct((b, s, C_HEADS * C_DIM), BF16),
        compiler_params=_cparams(("parallel", "parallel", "arbitrary")),
        name="attn_c",
    )(qc, kc, kc, kc, vc, vc, vc, slopes, sinks)


def _merge_kernel(x_ref, gmix_ref, wg_ref, oa_ref, ob_ref, oc_ref, wa_ref, wb_ref, wc_ref, wo_ref,
                  gffn_ref, wr_ref, xn_o, h2_o, aff_o):
    d = D_MODEL
    x = x_ref[0]
    hb = _row_rms(x, gmix_ref[...]).astype(BF16)
    g = _sigmoid(_dot(hb, wg_ref[...]))
    merged = (g[:, :d] * _dot(oa_ref[0], wa_ref[...]) + g[:, d:2 * d] * _dot(ob_ref[0], wb_ref[...])
              + g[:, 2 * d:] * _dot(oc_ref[0], wc_ref[...]))
    xn = x + _dot(merged.astype(BF16), wo_ref[...])
    xn_o[0] = xn
    h2 = _row_rms(xn, gffn_ref[...])
    h2_o[0] = _pack_rows(h2)
    logits = lax.dot_general(wr_ref[...], h2, _NT, preferred_element_type=F32,
                             precision=lax.Precision.HIGHEST)
    ex = jnp.exp(logits - jnp.max(logits, axis=0, keepdims=True))
    aff_o[0] = ex / jnp.sum(ex, axis=0, keepdims=True)


def _merge_call(x, oa, ob, oc, p, tm, batch0, nb):
    _, s, d = x.shape
    full = lambda a: pl.BlockSpec(a.shape, lambda i, j: (0,) * a.ndim, pipeline_mode=pl.Buffered(1))
    tok_in = lambda w: pl.BlockSpec((1, tm, w), lambda i, j: (i + batch0, j, 0))
    tok = lambda w: pl.BlockSpec((1, tm, w), lambda i, j: (i, j, 0))
    return pl.pallas_call(
        _merge_kernel,
        grid=(nb, s // tm),
        in_specs=[tok_in(d), full(p["gmix"]), full(p["w_gate"]), tok_in(512), tok_in(512), tok_in(512),
                  full(p["w_a"]), full(p["w_b"]), full(p["w_c"]), full(p["w_o"]), full(p["gffn"]), full(p["w_rt"])],
        out_specs=[tok(d), tok(d // 2), pl.BlockSpec((1, N_EXPERTS, tm), lambda i, j: (i, 0, j))],
        out_shape=[jax.ShapeDtypeStruct((nb, s, d), F32), jax.ShapeDtypeStruct((nb, s, d // 2), F32),
                   jax.ShapeDtypeStruct((nb, N_EXPERTS, s), F32)],
        compiler_params=_cparams(("parallel", "parallel")),
        name="merge",
    )(x, p["gmix"], p["w_gate"], oa, ob, oc, p["w_a"], p["w_b"], p["w_c"], p["w_o"], p["gffn"], p["w_rt"])


def _cumsum_lanes(mask01, chunk):
    rows, s = mask01.shape
    tri = jnp.where(lax.broadcasted_iota(jnp.int32, (chunk, chunk), 0)
                    <= lax.broadcasted_iota(jnp.int32, (chunk, chunk), 1), 1.0, 0.0).astype(BF16)
    carry = jnp.zeros((rows, 1), F32)
    outs = []
    for c in range(s // chunk):
        cs = _dot(mask01[:, c * chunk:(c + 1) * chunk], tri) + carry
        outs.append(cs)
        carry = cs[:, chunk - 1:chunk]
    return jnp.concatenate(outs, axis=1)


def _topk_kernel(aff_ref, pos_o, dest_o, start_o, *, cap, chunk):
    bits = pltpu.bitcast(aff_ref[0], jnp.int32)
    rows = bits.shape[0]
    capf = float(cap)

    def count(mask):
        return jnp.sum(jnp.where(mask, 1.0, 0.0), axis=1, keepdims=True)

    def body(_, c):
        lo, hi = c
        mid = lo + ((hi - lo + 1) >> 1)
        ok = count(bits >= mid) >= capf
        return jnp.where(ok, mid, lo), jnp.where(ok, hi, mid - 1)

    lo0 = jnp.zeros((rows, 1), jnp.int32)
    hi0 = jnp.full((rows, 1), 0x7F800000, jnp.int32)
    thr, _ = lax.fori_loop(0, 31, body, (lo0, hi0))
    gt = bits > thr
    eq = bits == thr
    need = capf - count(gt)
    eq_rank = _cumsum_lanes(jnp.where(eq, 1.0, 0.0).astype(BF16), chunk)
    sel = jnp.where(gt, 1.0, jnp.where(eq, jnp.where(eq_rank <= need, 1.0, 0.0), 0.0))
    sel_b = sel.astype(BF16)
    slot = _cumsum_lanes(sel_b, chunk) - 1.0
    pos_o[0] = jnp.where(sel > 0.0, slot, -1.0).astype(jnp.int32)
    per_token = jnp.broadcast_to(jnp.sum(sel, axis=0, keepdims=True), (8, sel.shape[1]))
    before_token = (_cumsum_lanes(per_token.astype(BF16), chunk) - per_token)[0:1]
    lower = jnp.where(lax.broadcasted_iota(jnp.int32, (rows, rows), 1)
                      < lax.broadcasted_iota(jnp.int32, (rows, rows), 0), 1.0, 0.0).astype(BF16)
    before_expert = _dot(lower, sel_b)
    dest_o[0] = jnp.where(sel > 0.0, before_token + before_expert, -1.0).astype(jnp.int32)
    start_o[0] = before_token.astype(jnp.int32)


def _topk_call(aff_t, cap):
    b, e, s = aff_t.shape
    spec = pl.BlockSpec((1, e, s), lambda i: (i, 0, 0))
    return pl.pallas_call(
        functools.partial(_topk_kernel, cap=cap, chunk=min(512, s)),
        grid=(b,),
        in_specs=[spec],
        out_specs=[spec, spec, pl.BlockSpec((1, 1, s), lambda i: (i, 0, 0))],
        out_shape=[jax.ShapeDtypeStruct((b, e, s), jnp.int32), jax.ShapeDtypeStruct((b, e, s), jnp.int32),
                   jax.ShapeDtypeStruct((b, 1, s), jnp.int32)],
        compiler_params=_cparams(("parallel",)),
        name="topk",
    )(aff_t)


def _sc_gather_call(table, pos2, aff2, cap):
    npairs, s = pos2.shape
    width = table.shape[1]
    workers = SC_CORES * SC_SUBCORES
    per_worker = npairs // workers
    nchunk = cap // SC_GATHER_ROWS
    assert npairs % workers == 0 and cap % SC_GATHER_ROWS == 0 and s % SC_LANES == 0
    mesh = plsc.VectorSubcoreMesh(core_axis_name="c", subcore_axis_name="s")

    @functools.partial(
        pl.kernel, mesh=mesh,
        out_type=[jax.ShapeDtypeStruct((npairs * cap, width), F32),
                  jax.ShapeDtypeStruct((npairs * cap * SC_LANES,), F32)],
        scratch_types=[pltpu.VMEM((s,), jnp.int32), pltpu.VMEM((s,), F32),
                       pltpu.VMEM((nchunk, SC_GATHER_ROWS), jnp.int32), pltpu.VMEM((cap,), F32),
                       pltpu.VMEM((cap * SC_LANES,), F32), pltpu.VMEM((SC_GATHER_ROWS, width), F32),
                       pltpu.SemaphoreType.DMA],
        compiler_params=pltpu.CompilerParams(needs_layout_passes=False),
        name="sc_gather",
    )
    def gather(table_hbm, pos_hbm, aff_hbm, rows_hbm, asel_hbm, pos_v, aff_v, idx_v, aslot_v, asplat_v, rows_v, sem):
        wid = lax.axis_index("s") * SC_CORES + lax.axis_index("c")
        lanes = lax.iota(jnp.int32, SC_LANES)

        @pl.loop(0, per_worker)
        def _(j):
            pair = wid * per_worker + j
            row0 = (pair // N_EXPERTS) * s
            pltpu.sync_copy(pos_hbm.at[pair], pos_v)
            pltpu.sync_copy(aff_hbm.at[pair], aff_v)

            @pl.loop(0, s // SC_LANES)
            def _(i):
                sl = pl.ds(i * SC_LANES, SC_LANES)
                slot = pos_v[sl]
                chosen = slot >= 0
                token_row = lanes + (i * SC_LANES + row0)
                plsc.store_scatter(idx_v, [slot >> (SC_GATHER_ROWS.bit_length() - 1), slot & (SC_GATHER_ROWS - 1)], token_row, mask=chosen)
                plsc.store_scatter(aslot_v, [slot], aff_v[sl], mask=chosen)

            @pl.loop(0, cap)
            def _(r):
                asplat_v[pl.ds(r * SC_LANES, SC_LANES)] = plsc.load_gather(
                    aslot_v, [jnp.full((SC_LANES,), r, jnp.int32)])

            pltpu.sync_copy(asplat_v, asel_hbm.at[pl.ds(pair * (cap * SC_LANES), cap * SC_LANES)])
            for c in range(nchunk):
                pltpu.async_copy(table_hbm.at[idx_v.at[c]], rows_v, sem).wait()
                pltpu.sync_copy(rows_v, rows_hbm.at[pl.ds(pair * cap + c * SC_GATHER_ROWS, SC_GATHER_ROWS)])

    return gather(table, pos2, aff2)


def _expert_kernel(xin_ref, asel_ref, wg_ref, wu_ref, wd_ref, y_o, wg_s, wu_s, wd_s):
    @pl.when(pl.program_id(1) == 0)
    def _():
        wg_s[...] = wg_ref[0, 0].astype(BF16)
        wu_s[...] = wu_ref[0, 0].astype(BF16)
        wd_s[...] = wd_ref[0, 0].astype(BF16)

    xin = _unpack_rows(xin_ref[0, 0])
    g = _dot(xin, wg_s[...])
    u = _dot(xin, wu_s[...])
    hid = (g * _sigmoid(g) * u).astype(BF16)
    y_o[0, 0] = _pack_rows(_dot(hid, wd_s[...]) * asel_ref[0, 0][:, :1])


def _expert_call(xin, asel, wg, wu, wd, layer):
    b, e, cap, _ = xin.shape
    d, f = wg.shape[-2:]
    tokens = lambda w: pl.BlockSpec((1, 1, cap, w), lambda j, i: (i, j, 0, 0))
    weight = lambda r, c: pl.BlockSpec((1, 1, r, c), lambda j, i: (layer, j, 0, 0))
    return pl.pallas_call(
        _expert_kernel,
        grid=(e, b),
        in_specs=[tokens(xin.shape[-1]), tokens(asel.shape[-1]), weight(d, f), weight(d, f), weight(f, d)],
        out_specs=tokens(d // 2),
        out_shape=jax.ShapeDtypeStruct((b, e, cap, d // 2), F32),
        scratch_shapes=[pltpu.VMEM((d, f), BF16), pltpu.VMEM((d, f), BF16), pltpu.VMEM((f, d), BF16)],
        compiler_params=_cparams(("parallel", "arbitrary")),
        name="expert",
    )(xin, asel, wg, wu, wd)


def _sc_regroup_call(y_rows, dest2, pos2, cap):
    npairs, s = dest2.shape
    width = y_rows.shape[1]
    workers = SC_CORES * SC_SUBCORES
    batches = npairs // N_EXPERTS
    per_batch = N_EXPERTS * cap
    split = workers // batches
    span = per_batch // split
    nchunk = span // SC_GATHER_ROWS
    assert workers % batches == 0 and per_batch % split == 0 and span % SC_GATHER_ROWS == 0 and s % SC_LANES == 0
    mesh = plsc.VectorSubcoreMesh(core_axis_name="c", subcore_axis_name="s")

    @functools.partial(
        pl.kernel, mesh=mesh,
        out_type=[jax.ShapeDtypeStruct((npairs * cap, width), F32),
                  jax.ShapeDtypeStruct((npairs * cap,), jnp.int32)],
        scratch_types=[pltpu.VMEM((s,), jnp.int32), pltpu.VMEM((s,), jnp.int32),
                       pltpu.VMEM((nchunk, SC_GATHER_ROWS), jnp.int32), pltpu.VMEM((span,), jnp.int32),
                       pltpu.VMEM((SC_GATHER_ROWS, width), F32), pltpu.SemaphoreType.DMA],
        compiler_params=pltpu.CompilerParams(needs_layout_passes=False),
        name="sc_regroup",
    )
    def regroup(y_hbm, dest_hbm, pos_hbm, rows_hbm, tok_hbm, dest_v, pos_v, src_v, tok_v, rows_v, sem):
        wid = lax.axis_index("s") * SC_CORES + lax.axis_index("c")
        batch = wid // split
        first = (wid % split) * span
        lanes = lax.iota(jnp.int32, SC_LANES)

        @pl.loop(0, N_EXPERTS)
        def _(e):
            pair = batch * N_EXPERTS + e
            pltpu.sync_copy(dest_hbm.at[pair], dest_v)
            pltpu.sync_copy(pos_hbm.at[pair], pos_v)

            @pl.loop(0, s // SC_LANES)
            def _(i):
                sl = pl.ds(i * SC_LANES, SC_LANES)
                local = dest_v[sl] - first
                mine = (local >= 0) & (local < span)
                plsc.store_scatter(src_v, [local >> (SC_GATHER_ROWS.bit_length() - 1), local & (SC_GATHER_ROWS - 1)],
                                   pos_v[sl] + pair * cap, mask=mine)
                plsc.store_scatter(tok_v, [local], lanes + i * SC_LANES, mask=mine)

        out0 = batch * per_batch + first
        pltpu.sync_copy(tok_v, tok_hbm.at[pl.ds(out0, span)])
        for c in range(nchunk):
            pltpu.async_copy(y_hbm.at[src_v.at[c]], rows_v, sem).wait()
            pltpu.sync_copy(rows_v, rows_hbm.at[pl.ds(out0 + c * SC_GATHER_ROWS, SC_GATHER_ROWS)])

    return regroup(y_rows, dest2, pos2)


def _combine_kernel(start_ref, x_ref, tok_ref, rows_ref, *rest, ntile, ck):
    o_ref = rest[-1]
    b, j = pl.program_id(0), pl.program_id(1)
    tt = x_ref.shape[1]
    lo = start_ref[b * (ntile + 1) + j]
    hi = start_ref[b * (ntile + 1) + j + 1]
    c_lo = lo // ck
    c_hi = jnp.where(hi > lo, (hi - 1) // ck + 1, c_lo)
    tokens = j * tt + lax.broadcasted_iota(jnp.int32, (tt, ck), 0)
    o_ref[0] = x_ref[0]

    def body(c, carry):
        onehot = jnp.where(tok_ref[0, c] == tokens, 1.0, 0.0).astype(BF16)
        o_ref[0] += _dot(onehot, _unpack_rows(rows_ref[0, c]))
        return carry

    lax.fori_loop(c_lo, c_hi, body, 0)


def _combine_call(xn, starts, tok, rows, tt, ck, batch, batch0, earlier):
    nb, s, d = xn.shape
    nchunk = tok.shape[1] // ck
    ntile = s // tt
    tile = pl.BlockSpec((1, tt, d), lambda i, t, st: (i + batch0, t, 0))
    in_specs = [pl.BlockSpec((1, tt, d), lambda i, t, st: (i, t, 0)),
                pl.BlockSpec((1, nchunk, 1, ck), lambda i, t, st: (i, 0, 0, 0)),
                pl.BlockSpec((1, nchunk, ck, rows.shape[-1]), lambda i, t, st: (i, 0, 0, 0))]
    operands = [starts, xn, tok.reshape(nb, nchunk, 1, ck), rows.reshape(nb, nchunk, ck, rows.shape[-1])]
    aliases = {}
    if earlier is not None:
        in_specs.append(pl.BlockSpec(memory_space=pl.ANY))
        operands.append(earlier)
        aliases = {len(operands) - 1: 0}
    return pl.pallas_call(
        functools.partial(_combine_kernel, ntile=ntile, ck=ck),
        grid_spec=pltpu.PrefetchScalarGridSpec(
            num_scalar_prefetch=1, grid=(nb, ntile), in_specs=in_specs, out_specs=tile),
        out_shape=jax.ShapeDtypeStruct((batch, s, d), F32),
        input_output_aliases=aliases,
        compiler_params=_cparams(("parallel", "arbitrary")),
        name="combine",
    )(*operands)


def _block_diag(n, blk):
    i = np.arange(n)
    return jnp.asarray((i[:, None] // blk) == (i[None, :] // blk), dtype=BF16)


def _head_slots(w, heads, width):
    r = w.shape[0]
    return jnp.pad(w.reshape(r, heads, width), ((0, 0), (0, 0), (0, LANES - width))).reshape(r, heads * LANES)


def _rotate_half_cols(w):
    half = B_ROPE // 2
    return jnp.concatenate([-w[..., half:], w[..., :half]], axis=-1)


def _layer_params(l, w_in, norm_mix_g, diff_qk_g, mla_cq_g, w_uq, mla_ckv_g, w_ukv, mla_qk_g, swa_qk_g,
                  w_branch_a, w_branch_b, w_branch_c, w_o, norm_ffn_g, w_router):
    d = D_MODEL
    wi = w_in[l]
    off = np.cumsum([0, 512, 512, 512, B_Q_LORA, B_KV_LORA, B_ROPE, 512, 128, 128, 3 * d])
    piece = lambda k: wi[:, off[k]:off[k + 1]]
    maps_to_heads = lambda w: w.reshape(d, 2, A_HEADS, A_DIM).transpose(0, 2, 1, 3).reshape(d, 512)
    dup = lambda w: jnp.concatenate([w.reshape(d, C_KV_HEADS, 1, C_DIM)] * 2, axis=2).reshape(d, 256)
    kr = piece(5)
    rope_slot = lambda w: jnp.pad(w, ((0, 0), (B_NOPE, LANES - B_QK)))
    w_proj = jnp.concatenate(
        [maps_to_heads(piece(0)), maps_to_heads(piece(1)), piece(2), piece(3), piece(4), piece(6),
         dup(piece(7)), dup(piece(8)), rope_slot(kr), rope_slot(_rotate_half_cols(kr))], axis=1).astype(BF16)
    assert w_proj.shape == (d, _PROJ_COLS)

    wq = w_uq[l].reshape(B_Q_LORA, B_HEADS, B_QK)
    wq_rot = jnp.concatenate([jnp.zeros_like(wq[..., :B_NOPE]), _rotate_half_cols(wq[..., B_NOPE:])], axis=-1)
    w_uq_x = jnp.concatenate([_head_slots(wq.reshape(B_Q_LORA, -1), B_HEADS, B_QK),
                              _head_slots(wq_rot.reshape(B_Q_LORA, -1), B_HEADS, B_QK)], axis=1).astype(BF16)
    wkv = w_ukv[l].reshape(B_KV_LORA, B_HEADS, B_NOPE + B_VDIM)
    w_ukv_x = jnp.concatenate([_head_slots(wkv[..., :B_NOPE].reshape(B_KV_LORA, -1), B_HEADS, B_NOPE),
                               wkv[..., B_NOPE:].reshape(B_KV_LORA, -1)], axis=1).astype(BF16)

    row = lambda v: v.reshape(1, -1).astype(F32)
    slot_gain = lambda g: jnp.tile(jnp.pad(g, (0, LANES - B_QK)), B_HEADS)
    return {
        "gmix": row(norm_mix_g[l]), "w_in": w_proj, "w_uq": w_uq_x, "w_ukv": w_ukv_x,
        "e64": _block_diag(SEG_TILE, 64), "e128": _block_diag(SEG_TILE, LANES),
        "gqa": row(jnp.tile(diff_qk_g[l, 0], 8) * (A_DIM ** -0.5 * LOG2E)), "gka": row(jnp.tile(diff_qk_g[l, 1], 8)),
        "gcq": row(mla_cq_g[l]), "gckv": row(mla_ckv_g[l]),
        "gqb": row(slot_gain(mla_qk_g[l, 0]) * (B_QK ** -0.5 * LOG2E)), "gkb": row(slot_gain(mla_qk_g[l, 1])),
        "gqc": row(jnp.tile(swa_qk_g[l, 0], 8) * (C_DIM ** -0.5)), "gkc": row(jnp.tile(swa_qk_g[l, 1], 4)),
        "w_gate": piece(9).astype(BF16),
        "w_a": w_branch_a[l].astype(BF16), "w_b": w_branch_b[l].astype(BF16), "w_c": w_branch_c[l].astype(BF16),
        "w_o": w_o[l].astype(BF16), "gffn": row(norm_ffn_g[l]), "w_rt": w_router[l].T.astype(F32),
    }


def _rope_slot_tables(positions):
    inv = 1.0 / (ROPE_THETA ** (jnp.arange(0, B_ROPE, 2, dtype=F32) / B_ROPE))
    ang = positions.astype(F32)[..., None] * inv
    cos, sin = lax.optimization_barrier((jnp.cos(ang), jnp.sin(ang)))
    ones = jnp.ones(ang.shape[:-1] + (B_NOPE,), F32)
    pad = jnp.zeros(ang.shape[:-1] + (LANES - B_QK,), F32)
    return (jnp.concatenate([ones, cos, cos, pad], axis=-1),
            jnp.concatenate([jnp.zeros_like(ones), sin, sin, pad], axis=-1))


def _alibi_slopes(n):
    return 2.0 ** (-8.0 * jnp.arange(1, n + 1, dtype=F32) / n)


def kernel(x, positions, norm_mix_g, w_in, diff_qk_g, diff_lambda, diff_out_g, mla_cq_g, w_uq, mla_ckv_g, w_ukv,
           mla_qk_g, swa_qk_g, swa_sink, w_branch_a, w_branch_b, w_branch_c, w_o, norm_ffn_g, w_router,
           w_exp_gate, w_exp_up, w_exp_down):
    b, s, d = x.shape
    depth = w_in.shape[0]
    cap = max(1, EC_CAPACITY * s // N_EXPERTS)
    tm_proj = min(512, s)
    tq = min(256, s)
    tq_a = min(512, s)
    tm_merge = min(512, s)
    tt = min(256, s)
    sc_workers = SC_CORES * SC_SUBCORES
    groups = 2 if (b % 2 == 0 and (b // 2) * N_EXPERTS % sc_workers == 0 and sc_workers % (b // 2) == 0) else 1

    cos_t, sin_t = _rope_slot_tables(positions)
    pos_f = positions.astype(F32)
    pos_c, pos_r = pos_f[:, :, None], pos_f.reshape(b, s // tq_a, 1, tq_a)
    monotone = jnp.all(positions[:, 1:] >= positions[:, :-1])
    lane_bcast = lambda v: jnp.broadcast_to(v[..., None], v.shape + (LANES,)).astype(F32)
    slopes_a = lane_bcast(_alibi_slopes(A_HEADS)[:, None] * LOG2E)
    slopes_c = lane_bcast(_alibi_slopes(C_HEADS).reshape(C_KV_HEADS, C_REP))

    for l in range(depth):
        p = _layer_params(l, w_in, norm_mix_g, diff_qk_g, mla_cq_g, w_uq, mla_ckv_g, w_ukv, mla_qk_g, swa_qk_g,
                          w_branch_a, w_branch_b, w_branch_c, w_o, norm_ffn_g, w_router)
        qa, ka, va, qb, kb, vb, qc, kc, vc = _proj_call(x, cos_t, sin_t, p, tm_proj)
        lam_init = 0.8 - 0.6 * math.exp(-0.3 * l)
        attn_a = functools.partial(_attn_a_call, qa, ka, va, pos_c, pos_r, slopes_a, diff_lambda[l].astype(F32),
                                   diff_out_g[l].reshape(1, -1).astype(F32), lam_init, tq_a)
        oa = lax.cond(monotone, functools.partial(attn_a, True), functools.partial(attn_a, False))
        ob = _attn_b_call(qb, kb, vb, min(512, s))
        oc = _attn_c_call(qc, kc, vc, slopes_c, lane_bcast(swa_sink[l].reshape(C_KV_HEADS, C_REP)))
        x_in, x = x, None
        nb = b // groups
        for grp in range(groups):
            b0 = grp * nb
            xn, h2, aff_t = _merge_call(x_in, oa, ob, oc, p, tm_merge, b0, nb)
            pos, dest, start = _topk_call(aff_t, cap)
            starts = jnp.concatenate([start[:, 0, ::tt], jnp.full((nb, 1), N_EXPERTS * cap, jnp.int32)], axis=1)
            pos2 = pos.reshape(nb * N_EXPERTS, s)
            rows, asel = _sc_gather_call(h2.reshape(nb * s, d // 2), pos2, aff_t.reshape(nb * N_EXPERTS, s), cap)
            y = _expert_call(rows.reshape(nb, N_EXPERTS, cap, d // 2), asel.reshape(nb, N_EXPERTS, cap, SC_LANES),
                             w_exp_gate, w_exp_up, w_exp_down, l)
            y_rows, y_tok = _sc_regroup_call(y.reshape(nb * N_EXPERTS * cap, d // 2),
                                             dest.reshape(nb * N_EXPERTS, s), pos2, cap)
            x = _combine_call(xn, starts.reshape(-1), y_tok.reshape(nb, N_EXPERTS * cap),
                              y_rows.reshape(nb, N_EXPERTS * cap, d // 2), tt, min(256, N_EXPERTS * cap), b, b0, x)
    return x
```

```python
import functools
import math

import numpy as np
import jax
import jax.numpy as jnp
from jax import lax
from jax.experimental import pallas as pl
from jax.experimental.pallas import tpu as pltpu
from jax.experimental.pallas import tpu_sc as plsc

F32 = jnp.float32
BF16 = jnp.bfloat16

D_MODEL = 1024
EPS = 1e-6
A_HEADS = 4
A_DIM = 64
B_HEADS = 8
B_NOPE = 64
B_ROPE = 32
B_VDIM = 64
B_QK = B_NOPE + B_ROPE
B_Q_LORA = 384
B_KV_LORA = 256
ROPE_THETA = 10000.0
C_HEADS = 8
C_KV_HEADS = 2
C_REP = C_HEADS // C_KV_HEADS
C_DIM = 64
WINDOW = 128
N_EXPERTS = 16
EC_CAPACITY = 2
LOG2E = math.log2(math.e)
SC_CORES, SC_SUBCORES, SC_LANES = 2, 16, 16
SC_GATHER_ROWS = 128
SEG_TILE = 256
LANES = 128

_QA = 0
_KA = 512
_VA = 1024
_CQ = 1536
_CKV = 1920
_QC = 2176
_KC = 2688
_VC = 2944
_KR = 3200
_KRR = 3328
_PROJ_COLS = 3456

VMEM_LIMIT = 56 * 1024 * 1024

_NT = (((1,), (1,)), ((), ()))


def _cparams(sem):
    return pltpu.CompilerParams(dimension_semantics=sem, vmem_limit_bytes=VMEM_LIMIT)


def _dot(a, b):
    return jnp.dot(a, b, preferred_element_type=F32)


def _seg_sum(x2, e):
    hi = x2.astype(BF16)
    lo = (x2 - hi.astype(F32)).astype(BF16)
    e2 = jnp.concatenate([e, e], axis=0)
    slabs = [_dot(jnp.concatenate([hi[:, c:c + SEG_TILE], lo[:, c:c + SEG_TILE]], axis=1), e2)
             for c in range(0, x2.shape[1], SEG_TILE)]
    return slabs[0] if len(slabs) == 1 else jnp.concatenate(slabs, axis=1)


def _pack_rows(x):
    half = x.shape[1] // 2
    xb = x.astype(BF16).astype(F32)
    lo = pltpu.bitcast(xb[:, :half], jnp.uint32) >> 16
    hi = pltpu.bitcast(xb[:, half:], jnp.uint32) & jnp.uint32(0xFFFF0000)
    return pltpu.bitcast(lo | hi, F32)


def _unpack_rows(words):
    bits = pltpu.bitcast(words, jnp.uint32)
    return jnp.concatenate([pltpu.bitcast(bits << 16, F32), pltpu.bitcast(bits & jnp.uint32(0xFFFF0000), F32)],
                           axis=1).astype(BF16)


def _sigmoid(x):
    return 0.5 * jnp.tanh(0.5 * x) + 0.5


def _row_rms(x, g):
    return x * lax.rsqrt(jnp.mean(x * x, axis=-1, keepdims=True) + EPS) * g


def _proj_kernel(x_ref, gmix_ref, w_ref, wuq_ref, wukv_ref, e64_ref, e128_ref,
                 gqa_ref, gka_ref, gcq_ref, gckv_ref, gqb_ref, gkb_ref, gqc_ref, gkc_ref,
                 cos_ref, sin_ref,
                 qa_o, ka_o, va_o, qb_o, kb_o, vb_o, qc_o, kc_o, vc_o):
    hb = _row_rms(x_ref[0], gmix_ref[...]).astype(BF16)

    projected = _dot(hb, w_ref[...])

    def proj(a, n):
        return projected[:, a:a + n]

    e64 = e64_ref[...]
    e128 = e128_ref[...]

    def seg_norm(v, e, width, g):
        return v * lax.rsqrt(_seg_sum(v * v, e) * (1.0 / width) + EPS) * g

    def store_slots(o_ref, v, n):
        for j in range(n):
            o_ref[0, j] = v[:, LANES * j:LANES * (j + 1)].astype(o_ref.dtype)

    ones_slot = jnp.ones((hb.shape[0], LANES), F32)

    def store_value_slots(o_ref, v, n):
        for j in range(n):
            o_ref[0, j] = jnp.concatenate([v[:, LANES * j:LANES * (j + 1)], ones_slot], axis=1).astype(o_ref.dtype)

    store_slots(qa_o, seg_norm(proj(_QA, 512), e64, A_DIM, gqa_ref[...]), A_HEADS)
    store_slots(ka_o, seg_norm(proj(_KA, 512), e64, A_DIM, gka_ref[...]), A_HEADS)
    store_value_slots(va_o, proj(_VA, 512), A_HEADS)

    cos_t = cos_ref[0]
    sin_t = sin_ref[0]
    cos8 = jnp.concatenate([cos_t] * B_HEADS, axis=1)
    sin8 = jnp.concatenate([sin_t] * B_HEADS, axis=1)
    cq = _row_rms(proj(_CQ, B_Q_LORA), gcq_ref[...]).astype(BF16)
    q2 = _dot(cq, wuq_ref[...])
    qb = q2[:, :1024] * cos8 + q2[:, 1024:] * sin8
    store_slots(qb_o, seg_norm(qb, e128, B_QK, gqb_ref[...]), B_HEADS)
    ckv = _row_rms(proj(_CKV, B_KV_LORA), gckv_ref[...]).astype(BF16)
    kv = _dot(ckv, wukv_ref[...])
    kr = proj(_KR, LANES) * cos_t + proj(_KRR, LANES) * sin_t
    kb = kv[:, :1024] + jnp.concatenate([kr] * B_HEADS, axis=1)
    store_slots(kb_o, seg_norm(kb, e128, B_QK, gkb_ref[...]), B_HEADS)
    store_value_slots(vb_o, kv[:, 1024:], B_HEADS // 2)

    store_slots(qc_o, seg_norm(proj(_QC, 512), e64, C_DIM, gqc_ref[...]), C_HEADS // 2)
    store_slots(kc_o, seg_norm(proj(_KC, 256), e64, C_DIM, gkc_ref[...]), C_KV_HEADS)
    store_value_slots(vc_o, proj(_VC, 256), C_KV_HEADS)


def _proj_call(x, cos_t, sin_t, p, tm):
    b, s, d = x.shape
    full = lambda a: pl.BlockSpec(a.shape, lambda i, j: (0,) * a.ndim, pipeline_mode=pl.Buffered(1))
    slot = lambda nw: pl.BlockSpec((1, nw[0], tm, nw[1]), lambda i, j: (i, 0, j, 0))
    tok = lambda w: pl.BlockSpec((1, tm, w), lambda i, j: (i, j, 0))
    consts = [p["gmix"], p["w_in"], p["w_uq"], p["w_ukv"], p["e64"], p["e128"],
              p["gqa"], p["gka"], p["gcq"], p["gckv"], p["gqb"], p["gkb"], p["gqc"], p["gkc"]]
    slots = [(A_HEADS, LANES), (A_HEADS, LANES), (A_HEADS, 2 * LANES), (B_HEADS, LANES), (B_HEADS, LANES),
             (B_HEADS // 2, 2 * LANES), (C_HEADS // 2, LANES), (C_KV_HEADS, LANES), (C_KV_HEADS, 2 * LANES)]
    return pl.pallas_call(
        _proj_kernel,
        grid=(b, s // tm),
        in_specs=[tok(d)] + [full(a) for a in consts] + [tok(LANES), tok(LANES)],
        out_specs=[slot(nw) for nw in slots],
        out_shape=[jax.ShapeDtypeStruct((b, nw[0], s, nw[1]), BF16) for nw in slots],
        compiler_params=_cparams(("parallel", "parallel")),
        name="proj",
    )(x, *consts, cos_t, sin_t)


def _attn_a_kernel(q_ref, k_ref, v_ref, pc_ref, pr_ref, slope_ref, lam_ref, og_ref, o_ref, s_scr, *, lam_init, mono):
    t = pl.program_id(2)
    q = q_ref[0, 0]
    tq = q.shape[0]
    ck = tq
    nck = k_ref.shape[2] // ck
    lane = lax.broadcasted_iota(jnp.int32, q.shape, 1)
    zero = jnp.zeros_like(q)
    qm = [jnp.where(lane < A_DIM, q, zero), jnp.where(lane >= A_DIM, q, zero)]
    slope = slope_ref[0][:, :1]
    a = slope * pc_ref[0]
    a_lanes = jnp.broadcast_to(a, (tq, LANES))

    def lane_fold_max(x):
        out = x[:, :LANES]
        for j in range(1, ck // LANES):
            out = jnp.maximum(out, x[:, j * LANES:(j + 1) * LANES])
        return out

    mx = [jnp.full((tq, LANES), -jnp.inf, F32)] * 2
    for d in range(nck):
        if mono:
            c = t if d == 0 else lax.rem(t + d, nck)
            k_c = k_ref[0, 0, pl.ds(pl.multiple_of(c * ck, ck), ck), :]
        else:
            c = d
            k_c = k_ref[0, 0, d * ck:(d + 1) * ck, :]
        b = slope * pr_ref[0, c]
        if mono and d > 0:
            sign = jnp.where(t + d < nck, -1.0, 1.0)
            row_part, col_part = sign * b, sign * a_lanes
            for m in range(2):
                sc = lax.dot_general(qm[m], k_c, _NT, preferred_element_type=F32) + row_part
                s_scr[m, c] = sc
                mx[m] = jnp.maximum(mx[m], lane_fold_max(sc) - col_part)
        else:
            bias = jnp.abs(a - b)
            for m in range(2):
                sc = lax.dot_general(qm[m], k_c, _NT, preferred_element_type=F32) - bias
                s_scr[m, c] = sc
                mx[m] = jnp.maximum(mx[m], lane_fold_max(sc))
    v = v_ref[0, 0]
    acc = []
    for m in range(2):
        row_max = jnp.broadcast_to(jnp.max(mx[m], axis=-1, keepdims=True), (tq, LANES))
        es = []
        for c in range(nck):
            if mono:
                sign = jnp.where(c < t, 1.0, jnp.where(c > t, -1.0, 0.0))
                stab = row_max + sign * a_lanes
            else:
                stab = row_max
            es.append(jnp.exp2(s_scr[m, c] - jnp.concatenate([stab] * (ck // LANES), axis=1)).astype(BF16))
        e = es[0] if nck == 1 else jnp.concatenate(es, axis=1)
        acc.append(_dot(e, v))
    lp = lam_ref[...]
    lam = (jnp.exp(jnp.sum(lp[0:1] * lp[1:2], axis=-1, keepdims=True))
           - jnp.exp(jnp.sum(lp[2:3] * lp[3:4], axis=-1, keepdims=True)) + lam_init)
    o = (acc[0][:, :LANES] * (1.0 / acc[0][:, LANES:])
         - acc[1][:, :LANES] * (lam / acc[1][:, LANES:]))
    o_ref[0] = (_row_rms(o, og_ref[...]) * (1.0 - lam_init)).astype(o_ref.dtype)


def _attn_a_call(qa, ka, va, pos_c, pos_r, slopes, lam_p, out_g, lam_init, tq, mono):
    b, h, s, _ = qa.shape
    kv_spec = lambda w: pl.BlockSpec((1, 1, s, w), lambda i, j, t: (i, j, 0, 0))
    return pl.pallas_call(
        functools.partial(_attn_a_kernel, lam_init=lam_init, mono=mono),
        grid=(b, h, s // tq),
        scratch_shapes=[pltpu.VMEM((2, s // tq, tq, tq), F32)],
        in_specs=[pl.BlockSpec((1, 1, tq, LANES), lambda i, j, t: (i, j, t, 0)), kv_spec(LANES), kv_spec(2 * LANES),
                  pl.BlockSpec((1, tq, 1), lambda i, j, t: (i, t, 0)),
                  pl.BlockSpec((1, s // tq, 1, tq), lambda i, j, t: (i, 0, 0, 0)),
                  pl.BlockSpec((1, 1, LANES), lambda i, j, t: (j, 0, 0)),
                  pl.BlockSpec(lam_p.shape, lambda i, j, t: (0, 0)),
                  pl.BlockSpec(out_g.shape, lambda i, j, t: (0, 0))],
        out_specs=pl.BlockSpec((1, tq, LANES), lambda i, j, t: (i, t, j)),
        out_shape=jax.ShapeDtypeStruct((b, s, h * LANES), BF16),
        compiler_params=_cparams(("parallel", "parallel", "arbitrary")),
        name="attn_a",
    )(qa, ka, va, pos_c, pos_r, slopes, lam_p, out_g)


def _attn_b_kernel(q_ref, k_ref, v_ref, o_ref):
    tq = q_ref.shape[2]
    es = []
    for j in range(2):
        s = lax.dot_general(q_ref[0, j], k_ref[0, j], _NT, preferred_element_type=F32)
        es.append(jnp.exp2(s - jnp.max(s, axis=-1, keepdims=True)).astype(BF16))
    acc = _dot(jnp.concatenate(es, axis=0), v_ref[0, 0])
    lane = lax.broadcasted_iota(jnp.int32, (tq, LANES), 1)
    o_ref[0] = jnp.where(lane < B_VDIM, acc[:tq, :LANES] * (1.0 / acc[:tq, LANES:]),
                         acc[tq:, :LANES] * (1.0 / acc[tq:, LANES:])).astype(o_ref.dtype)


def _attn_b_call(qb, kb, vb, tq):
    b, h, s, _ = qb.shape
    return pl.pallas_call(
        _attn_b_kernel,
        grid=(b, h // 2, s // tq),
        in_specs=[pl.BlockSpec((1, 2, tq, LANES), lambda i, j, t: (i, j, t, 0)),
                  pl.BlockSpec((1, 2, s, LANES), lambda i, j, t: (i, j, 0, 0)),
                  pl.BlockSpec((1, 1, s, 2 * LANES), lambda i, j, t: (i, j, 0, 0))],
        out_specs=pl.BlockSpec((1, tq, LANES), lambda i, j, t: (i, t, j)),
        out_shape=jax.ShapeDtypeStruct((b, s, (h // 2) * LANES), BF16),
        compiler_params=_cparams(("parallel", "parallel", "arbitrary")),
        name="attn_b",
    )(qb, kb, vb)


def _attn_c_kernel(q_ref, kp_ref, ko_ref, kn_ref, vp_ref, vo_ref, vn_ref, slope_ref, sink_ref, o_ref, *, seq):
    w = WINDOW
    nsub = ko_ref.shape[2] // w
    n0 = pl.program_id(2) * nsub
    kcat = jnp.concatenate([kp_ref[0, 0], ko_ref[0, 0], kn_ref[0, 0]], axis=0)
    vcat = jnp.concatenate([vp_ref[0, 0], vo_ref[0, 0], vn_ref[0, 0]], axis=0)
    lane = lax.broadcasted_iota(jnp.int32, (w, LANES), 1)
    r_idx = lax.broadcasted_iota(jnp.int32, (w, 3 * w), 0)
    c_idx = lax.broadcasted_iota(jnp.int32, (w, 3 * w), 1)
    arel = jnp.abs(c_idx - w - r_idx)
    dist = arel.astype(F32)
    slopes = slope_ref[0]
    sinks = sink_ref[0]
    bias4 = jnp.concatenate([jnp.where(arel <= w, -slopes[r:r + 1, :1] * dist, -1e30) for r in range(C_REP)], axis=0)
    sink4 = jnp.concatenate([jnp.broadcast_to(sinks[r:r + 1, :], (w, LANES)) for r in range(C_REP)], axis=0)
    c_row = lax.broadcasted_iota(jnp.int32, (1, 3 * w), 1)
    scs = []
    for i in range(nsub):
        parts = []
        for p in range(2):
            q = q_ref[0, p, i * w:(i + 1) * w, :]
            zero = jnp.zeros_like(q)
            parts += [jnp.where(lane < C_DIM, q, zero), jnp.where(lane >= C_DIM, q, zero)]
        qz = jnp.concatenate(parts, axis=0)
        s = lax.dot_general(qz, kcat[i * w:(i + 3) * w], _NT, preferred_element_type=F32)
        kidx = (n0 + i - 1) * w + c_row
        edge = jnp.where(kidx >= 0, jnp.where(kidx < seq, 0.0, -1e30), -1e30)
        scs.append(s + bias4 + edge)
    sc = jnp.concatenate(scs, axis=0)
    sk = jnp.concatenate([sink4] * nsub, axis=0)
    m = jnp.maximum(jnp.broadcast_to(jnp.max(sc, axis=-1, keepdims=True), sk.shape), sk)
    e = jnp.exp(sc - jnp.concatenate([m] * 3, axis=1)).astype(BF16)
    tail = jnp.exp(sk - m)
    for i in range(nsub):
        rows = slice(i * 4 * w, (i + 1) * 4 * w)
        acc = _dot(e[rows], vcat[i * w:(i + 3) * w])
        o = acc[:, :LANES] * (1.0 / (acc[:, LANES:] + tail[rows]))
        pair0 = jnp.where(lane < C_DIM, o[0:w], o[w:2 * w])
        pair1 = jnp.where(lane < C_DIM, o[2 * w:3 * w], o[3 * w:4 * w])
        o_ref[0, i * w:(i + 1) * w, :] = jnp.concatenate([pair0, pair1], axis=1).astype(o_ref.dtype)


def _attn_c_call(qc, kc, vc, slopes, sinks):
    b, _, s, _ = qc.shape
    nb = s // WINDOW
    nsub = min(16, nb)
    tq = nsub * WINDOW
    prev = lambda wd: pl.BlockSpec((1, 1, WINDOW, wd), lambda i, g, n: (i, g, jnp.maximum(n * nsub - 1, 0), 0))
    own = lambda wd: pl.BlockSpec((1, 1, tq, wd), lambda i, g, n: (i, g, n, 0))
    nxt = lambda wd: pl.BlockSpec((1, 1, WINDOW, wd), lambda i, g, n: (i, g, jnp.minimum((n + 1) * nsub, nb - 1), 0))
    kw, vw = kc.shape[-1], vc.shape[-1]
    per_group = pl.BlockSpec((1, C_REP, LANES), lambda i, g, n: (g, 0, 0))
    return pl.pallas_call(
        functools.partial(_attn_c_kernel, seq=s),
        grid=(b, C_KV_HEADS, s // tq),
        in_specs=[pl.BlockSpec((1, 2, tq, LANES), lambda i, g, n: (i, g, n, 0)),
                  prev(kw), own(kw), nxt(kw), prev(vw), own(vw), nxt(vw), per_group, per_group],
        out_specs=pl.BlockSpec((1, tq, 2 * LANES), lambda i, g, n: (i, n, g)),
        out_shape=jax.ShapeDtypeStruct((b, s, C_HEADS * C_DIM), BF16),
        compiler_params=_cparams(("parallel", "parallel", "arbitrary")),
        name="attn_c",
    )(qc, kc, kc, kc, vc, vc, vc, slopes, sinks)


def _merge_kernel(x_ref, gmix_ref, wg_ref, oa_ref, ob_ref, oc_ref, wa_ref, wb_ref, wc_ref, wo_ref,
                  gffn_ref, wr_ref, xn_o, h2_o, aff_o):
    d = D_MODEL
    x = x_ref[0]
    hb = _row_rms(x, gmix_ref[...]).astype(BF16)
    g = _sigmoid(_dot(hb, wg_ref[...]))
    merged = (g[:, :d] * _dot(oa_ref[0], wa_ref[...]) + g[:, d:2 * d] * _dot(ob_ref[0], wb_ref[...])
              + g[:, 2 * d:] * _dot(oc_ref[0], wc_ref[...]))
    xn = x + _dot(merged.astype(BF16), wo_ref[...])
    xn_o[0] = xn
    h2 = _row_rms(xn, gffn_ref[...])
    h2_o[0] = _pack_rows(h2)
    logits = lax.dot_general(wr_ref[...], h2, _NT, preferred_element_type=F32,
                             precision=lax.Precision.HIGHEST)
    ex = jnp.exp(logits - jnp.max(logits, axis=0, keepdims=True))
    aff_o[0] = ex / jnp.sum(ex, axis=0, keepdims=True)


def _merge_call(x, oa, ob, oc, p, tm, batch0, nb):
    _, s, d = x.shape
    full = lambda a: pl.BlockSpec(a.shape, lambda i, j: (0,) * a.ndim, pipeline_mode=pl.Buffered(1))
    tok_in = lambda w: pl.BlockSpec((1, tm, w), lambda i, j: (i + batch0, j, 0))
    tok = lambda w: pl.BlockSpec((1, tm, w), lambda i, j: (i, j, 0))
    return pl.pallas_call(
        _merge_kernel,
        grid=(nb, s // tm),
        in_specs=[tok_in(d), full(p["gmix"]), full(p["w_gate"]), tok_in(512), tok_in(512), tok_in(512),
                  full(p["w_a"]), full(p["w_b"]), full(p["w_c"]), full(p["w_o"]), full(p["gffn"]), full(p["w_rt"])],
        out_specs=[tok(d), tok(d // 2), pl.BlockSpec((1, N_EXPERTS, tm), lambda i, j: (i, 0, j))],
        out_shape=[jax.ShapeDtypeStruct((nb, s, d), F32), jax.ShapeDtypeStruct((nb, s, d // 2), F32),
                   jax.ShapeDtypeStruct((nb, N_EXPERTS, s), F32)],
        compiler_params=_cparams(("parallel", "parallel")),
        name="merge",
    )(x, p["gmix"], p["w_gate"], oa, ob, oc, p["w_a"], p["w_b"], p["w_c"], p["w_o"], p["gffn"], p["w_rt"])


def _cumsum_lanes(mask01, chunk):
    rows, s = mask01.shape
    tri = jnp.where(lax.broadcasted_iota(jnp.int32, (chunk, chunk), 0)
                    <= lax.broadcasted_iota(jnp.int32, (chunk, chunk), 1), 1.0, 0.0).astype(BF16)
    carry = jnp.zeros((rows, 1), F32)
    outs = []
    for c in range(s // chunk):
        cs = _dot(mask01[:, c * chunk:(c + 1) * chunk], tri) + carry
        outs.append(cs)
        carry = cs[:, chunk - 1:chunk]
    return jnp.concatenate(outs, axis=1)


def _topk_kernel(aff_ref, pos_o, dest_o, start_o, *, cap, chunk):
    bits = pltpu.bitcast(aff_ref[0], jnp.int32)
    rows = bits.shape[0]
    capf = float(cap)

    def count(mask):
        return jnp.sum(jnp.where(mask, 1.0, 0.0), axis=1, keepdims=True)

    def body(_, c):
        lo, hi = c
        mid = lo + ((hi - lo + 1) >> 1)
        ok = count(bits >= mid) >= capf
        return jnp.where(ok, mid, lo), jnp.where(ok, hi, mid - 1)

    lo0 = jnp.zeros((rows, 1), jnp.int32)
    hi0 = jnp.full((rows, 1), 0x7F800000, jnp.int32)
    thr, _ = lax.fori_loop(0, 31, body, (lo0, hi0))
    gt = bits > thr
    eq = bits == thr
    need = capf - count(gt)
    eq_rank = _cumsum_lanes(jnp.where(eq, 1.0, 0.0).astype(BF16), chunk)
    sel = jnp.where(gt, 1.0, jnp.where(eq, jnp.where(eq_rank <= need, 1.0, 0.0), 0.0))
    sel_b = sel.astype(BF16)
    slot = _cumsum_lanes(sel_b, chunk) - 1.0
    pos_o[0] = jnp.where(sel > 0.0, slot, -1.0).astype(jnp.int32)
    per_token = jnp.broadcast_to(jnp.sum(sel, axis=0, keepdims=True), (8, sel.shape[1]))
    before_token = (_cumsum_lanes(per_token.astype(BF16), chunk) - per_token)[0:1]
    lower = jnp.where(lax.broadcasted_iota(jnp.int32, (rows, rows), 1)
                      < lax.broadcasted_iota(jnp.int32, (rows, rows), 0), 1.0, 0.0).astype(BF16)
    before_expert = _dot(lower, sel_b)
    dest_o[0] = jnp.where(sel > 0.0, before_token + before_expert, -1.0).astype(jnp.int32)
    start_o[0] = before_token.astype(jnp.int32)


def _topk_call(aff_t, cap):
    b, e, s = aff_t.shape
    spec = pl.BlockSpec((1, e, s), lambda i: (i, 0, 0))
    return pl.pallas_call(
        functools.partial(_topk_kernel, cap=cap, chunk=min(512, s)),
        grid=(b,),
        in_specs=[spec],
        out_specs=[spec, spec, pl.BlockSpec((1, 1, s), lambda i: (i, 0, 0))],
        out_shape=[jax.ShapeDtypeStruct((b, e, s), jnp.int32), jax.ShapeDtypeStruct((b, e, s), jnp.int32),
                   jax.ShapeDtypeStruct((b, 1, s), jnp.int32)],
        compiler_params=_cparams(("parallel",)),
        name="topk",
    )(aff_t)


def _sc_gather_call(table, pos2, aff2, cap):
    npairs, s = pos2.shape
    width = table.shape[1]
    workers = SC_CORES * SC_SUBCORES
    per_worker = npairs // workers
    nchunk = cap // SC_GATHER_ROWS
    assert npairs % workers == 0 and cap % SC_GATHER_ROWS == 0 and s % SC_LANES == 0
    mesh = plsc.VectorSubcoreMesh(core_axis_name="c", subcore_axis_name="s")

    @functools.partial(
        pl.kernel, mesh=mesh,
        out_type=[jax.ShapeDtypeStruct((npairs * cap, width), F32),
                  jax.ShapeDtypeStruct((npairs * cap * SC_LANES,), F32)],
        scratch_types=[pltpu.VMEM((s,), jnp.int32), pltpu.VMEM((s,), F32),
                       pltpu.VMEM((nchunk, SC_GATHER_ROWS), jnp.int32), pltpu.VMEM((cap,), F32),
                       pltpu.VMEM((cap * SC_LANES,), F32), pltpu.VMEM((SC_GATHER_ROWS, width), F32),
                       pltpu.SemaphoreType.DMA],
        compiler_params=pltpu.CompilerParams(needs_layout_passes=False),
        name="sc_gather",
    )
    def gather(table_hbm, pos_hbm, aff_hbm, rows_hbm, asel_hbm, pos_v, aff_v, idx_v, aslot_v, asplat_v, rows_v, sem):
        wid = lax.axis_index("s") * SC_CORES + lax.axis_index("c")
        lanes = lax.iota(jnp.int32, SC_LANES)

        @pl.loop(0, per_worker)
        def _(j):
            pair = wid * per_worker + j
            row0 = (pair // N_EXPERTS) * s
            pltpu.sync_copy(pos_hbm.at[pair], pos_v)
            pltpu.sync_copy(aff_hbm.at[pair], aff_v)

            @pl.loop(0, s // SC_LANES)
            def _(i):
                sl = pl.ds(i * SC_LANES, SC_LANES)
                slot = pos_v[sl]
                chosen = slot >= 0
                token_row = lanes + (i * SC_LANES + row0)
                plsc.store_scatter(idx_v, [slot >> (SC_GATHER_ROWS.bit_length() - 1), slot & (SC_GATHER_ROWS - 1)], token_row, mask=chosen)
                plsc.store_scatter(aslot_v, [slot], aff_v[sl], mask=chosen)

            @pl.loop(0, cap)
            def _(r):
                asplat_v[pl.ds(r * SC_LANES, SC_LANES)] = plsc.load_gather(
                    aslot_v, [jnp.full((SC_LANES,), r, jnp.int32)])

            pltpu.sync_copy(asplat_v, asel_hbm.at[pl.ds(pair * (cap * SC_LANES), cap * SC_LANES)])
            for c in range(nchunk):
                pltpu.async_copy(table_hbm.at[idx_v.at[c]], rows_v, sem).wait()
                pltpu.sync_copy(rows_v, rows_hbm.at[pl.ds(pair * cap + c * SC_GATHER_ROWS, SC_GATHER_ROWS)])

    return gather(table, pos2, aff2)


def _expert_kernel(xin_ref, asel_ref, wg_ref, wu_ref, wd_ref, y_o, wg_s, wu_s, wd_s):
    @pl.when(pl.program_id(1) == 0)
    def _():
        wg_s[...] = wg_ref[0, 0].astype(BF16)
        wu_s[...] = wu_ref[0, 0].astype(BF16)
        wd_s[...] = wd_ref[0, 0].astype(BF16)

    xin = _unpack_rows(xin_ref[0, 0])
    g = _dot(xin, wg_s[...])
    u = _dot(xin, wu_s[...])
    hid = (g * _sigmoid(g) * u).astype(BF16)
    y_o[0, 0] = _pack_rows(_dot(hid, wd_s[...]) * asel_ref[0, 0][:, :1])


def _expert_call(xin, asel, wg, wu, wd, layer):
    b, e, cap, _ = xin.shape
    d, f = wg.shape[-2:]
    tokens = lambda w: pl.BlockSpec((1, 1, cap, w), lambda j, i: (i, j, 0, 0))
    weight = lambda r, c: pl.BlockSpec((1, 1, r, c), lambda j, i: (layer, j, 0, 0))
    return pl.pallas_call(
        _expert_kernel,
        grid=(e, b),
        in_specs=[tokens(xin.shape[-1]), tokens(asel.shape[-1]), weight(d, f), weight(d, f), weight(f, d)],
        out_specs=tokens(d // 2),
        out_shape=jax.ShapeDtypeStruct((b, e, cap, d // 2), F32),
        scratch_shapes=[pltpu.VMEM((d, f), BF16), pltpu.VMEM((d, f), BF16), pltpu.VMEM((f, d), BF16)],
        compiler_params=_cparams(("parallel", "arbitrary")),
        name="expert",
    )(xin, asel, wg, wu, wd)


def _sc_regroup_call(y_rows, dest2, pos2, cap):
    npairs, s = dest2.shape
    width = y_rows.shape[1]
    workers = SC_CORES * SC_SUBCORES
    batches = npairs // N_EXPERTS
    per_batch = N_EXPERTS * cap
    split = workers // batches
    span = per_batch // split
    nchunk = span // SC_GATHER_ROWS
    assert workers % batches == 0 and per_batch % split == 0 and span % SC_GATHER_ROWS == 0 and s % SC_LANES == 0
    mesh = plsc.VectorSubcoreMesh(core_axis_name="c", subcore_axis_name="s")

    @functools.partial(
        pl.kernel, mesh=mesh,
        out_type=[jax.ShapeDtypeStruct((npairs * cap, width), F32),
                  jax.ShapeDtypeStruct((npairs * cap,), jnp.int32)],
        scratch_types=[pltpu.VMEM((s,), jnp.int32), pltpu.VMEM((s,), jnp.int32),
                       pltpu.VMEM((nchunk, SC_GATHER_ROWS), jnp.int32), pltpu.VMEM((span,), jnp.int32),
                       pltpu.VMEM((SC_GATHER_ROWS, width), F32), pltpu.SemaphoreType.DMA],
        compiler_params=pltpu.CompilerParams(needs_layout_passes=False),
        name="sc_regroup",
    )
    def regroup(y_hbm, dest_hbm, pos_hbm, rows_hbm, tok_hbm, dest_v, pos_v, src_v, tok_v, rows_v, sem):
        wid = lax.axis_index("s") * SC_CORES + lax.axis_index("c")
        batch = wid // split
        first = (wid % split) * span
        lanes = lax.iota(jnp.int32, SC_LANES)

        @pl.loop(0, N_EXPERTS)
        def _(e):
            pair = batch * N_EXPERTS + e
            pltpu.sync_copy(dest_hbm.at[pair], dest_v)
            pltpu.sync_copy(pos_hbm.at[pair], pos_v)

            @pl.loop(0, s // SC_LANES)
            def _(i):
                sl = pl.ds(i * SC_LANES, SC_LANES)
                local = dest_v[sl] - first
                mine = (local >= 0) & (local < span)
                plsc.store_scatter(src_v, [local >> (SC_GATHER_ROWS.bit_length() - 1), local & (SC_GATHER_ROWS - 1)],
                                   pos_v[sl] + pair * cap, mask=mine)
                plsc.store_scatter(tok_v, [local], lanes + i * SC_LANES, mask=mine)

        out0 = batch * per_batch + first
        pltpu.sync_copy(tok_v, tok_hbm.at[pl.ds(out0, span)])
        for c in range(nchunk):
            pltpu.async_copy(y_hbm.at[src_v.at[c]], rows_v, sem).wait()
            pltpu.sync_copy(rows_v, rows_hbm.at[pl.ds(out0 + c * SC_GATHER_ROWS, SC_GATHER_ROWS)])

    return regroup(y_rows, dest2, pos2)


def _combine_kernel(start_ref, x_ref, tok_ref, rows_ref, *rest, ntile, ck):
    o_ref = rest[-1]
    b, j = pl.program_id(0), pl.program_id(1)
    tt = x_ref.shape[1]
    lo = start_ref[b * (ntile + 1) + j]
    hi = start_ref[b * (ntile + 1) + j + 1]
    c_lo = lo // ck
    c_hi = jnp.where(hi > lo, (hi - 1) // ck + 1, c_lo)
    tokens = j * tt + lax.broadcasted_iota(jnp.int32, (tt, ck), 0)
    o_ref[0] = x_ref[0]

    def body(c, carry):
        onehot = jnp.where(tok_ref[0, c] == tokens, 1.0, 0.0).astype(BF16)
        o_ref[0] += _dot(onehot, _unpack_rows(rows_ref[0, c]))
        return carry

    lax.fori_loop(c_lo, c_hi, body, 0)


def _combine_call(xn, starts, tok, rows, tt, ck, batch, batch0, earlier):
    nb, s, d = xn.shape
    nchunk = tok.shape[1] // ck
    ntile = s // tt
    tile = pl.BlockSpec((1, tt, d), lambda i, t, st: (i + batch0, t, 0))
    in_specs = [pl.BlockSpec((1, tt, d), lambda i, t, st: (i, t, 0)),
                pl.BlockSpec((1, nchunk, 1, ck), lambda i, t, st: (i, 0, 0, 0)),
                pl.BlockSpec((1, nchunk, ck, rows.shape[-1]), lambda i, t, st: (i, 0, 0, 0))]
    operands = [starts, xn, tok.reshape(nb, nchunk, 1, ck), rows.reshape(nb, nchunk, ck, rows.shape[-1])]
    aliases = {}
    if earlier is not None:
        in_specs.append(pl.BlockSpec(memory_space=pl.ANY))
        operands.append(earlier)
        aliases = {len(operands) - 1: 0}
    return pl.pallas_call(
        functools.partial(_combine_kernel, ntile=ntile, ck=ck),
        grid_spec=pltpu.PrefetchScalarGridSpec(
            num_scalar_prefetch=1, grid=(nb, ntile), in_specs=in_specs, out_specs=tile),
        out_shape=jax.ShapeDtypeStruct((batch, s, d), F32),
        input_output_aliases=aliases,
        compiler_params=_cparams(("parallel", "arbitrary")),
        name="combine",
    )(*operands)


def _block_diag(n, blk):
    i = np.arange(n)
    return jnp.asarray((i[:, None] // blk) == (i[None, :] // blk), dtype=BF16)


def _head_slots(w, heads, width):
    r = w.shape[0]
    return jnp.pad(w.reshape(r, heads, width), ((0, 0), (0, 0), (0, LANES - width))).reshape(r, heads * LANES)


def _rotate_half_cols(w):
    half = B_ROPE // 2
    return jnp.concatenate([-w[..., half:], w[..., :half]], axis=-1)


def _layer_params(l, w_in, norm_mix_g, diff_qk_g, mla_cq_g, w_uq, mla_ckv_g, w_ukv, mla_qk_g, swa_qk_g,
                  w_branch_a, w_branch_b, w_branch_c, w_o, norm_ffn_g, w_router):
    d = D_MODEL
    wi = w_in[l]
    off = np.cumsum([0, 512, 512, 512, B_Q_LORA, B_KV_LORA, B_ROPE, 512, 128, 128, 3 * d])
    piece = lambda k: wi[:, off[k]:off[k + 1]]
    maps_to_heads = lambda w: w.reshape(d, 2, A_HEADS, A_DIM).transpose(0, 2, 1, 3).reshape(d, 512)
    dup = lambda w: jnp.concatenate([w.reshape(d, C_KV_HEADS, 1, C_DIM)] * 2, axis=2).reshape(d, 256)
    kr = piece(5)
    rope_slot = lambda w: jnp.pad(w, ((0, 0), (B_NOPE, LANES - B_QK)))
    w_proj = jnp.concatenate(
        [maps_to_heads(piece(0)), maps_to_heads(piece(1)), piece(2), piece(3), piece(4), piece(6),
         dup(piece(7)), dup(piece(8)), rope_slot(kr), rope_slot(_rotate_half_cols(kr))], axis=1).astype(BF16)
    assert w_proj.shape == (d, _PROJ_COLS)

    wq = w_uq[l].reshape(B_Q_LORA, B_HEADS, B_QK)
    wq_rot = jnp.concatenate([jnp.zeros_like(wq[..., :B_NOPE]), _rotate_half_cols(wq[..., B_NOPE:])], axis=-1)
    w_uq_x = jnp.concatenate([_head_slots(wq.reshape(B_Q_LORA, -1), B_HEADS, B_QK),
                              _head_slots(wq_rot.reshape(B_Q_LORA, -1), B_HEADS, B_QK)], axis=1).astype(BF16)
    wkv = w_ukv[l].reshape(B_KV_LORA, B_HEADS, B_NOPE + B_VDIM)
    w_ukv_x = jnp.concatenate([_head_slots(wkv[..., :B_NOPE].reshape(B_KV_LORA, -1), B_HEADS, B_NOPE),
                               wkv[..., B_NOPE:].reshape(B_KV_LORA, -1)], axis=1).astype(BF16)

    row = lambda v: v.reshape(1, -1).astype(F32)
    slot_gain = lambda g: jnp.tile(jnp.pad(g, (0, LANES - B_QK)), B_HEADS)
    return {
        "gmix": row(norm_mix_g[l]), "w_in": w_proj, "w_uq": w_uq_x, "w_ukv": w_ukv_x,
        "e64": _block_diag(SEG_TILE, 64), "e128": _block_diag(SEG_TILE, LANES),
        "gqa": row(jnp.tile(diff_qk_g[l, 0], 8) * (A_DIM ** -0.5 * LOG2E)), "gka": row(jnp.tile(diff_qk_g[l, 1], 8)),
        "gcq": row(mla_cq_g[l]), "gckv": row(mla_ckv_g[l]),
        "gqb": row(slot_gain(mla_qk_g[l, 0]) * (B_QK ** -0.5 * LOG2E)), "gkb": row(slot_gain(mla_qk_g[l, 1])),
        "gqc": row(jnp.tile(swa_qk_g[l, 0], 8) * (C_DIM ** -0.5)), "gkc": row(jnp.tile(swa_qk_g[l, 1], 4)),
        "w_gate": piece(9).astype(BF16),
        "w_a": w_branch_a[l].astype(BF16), "w_b": w_branch_b[l].astype(BF16), "w_c": w_branch_c[l].astype(BF16),
        "w_o": w_o[l].astype(BF16), "gffn": row(norm_ffn_g[l]), "w_rt": w_router[l].T.astype(F32),
    }


def _rope_slot_tables(positions):
    inv = 1.0 / (ROPE_THETA ** (jnp.arange(0, B_ROPE, 2, dtype=F32) / B_ROPE))
    ang = positions.astype(F32)[..., None] * inv
    cos, sin = lax.optimization_barrier((jnp.cos(ang), jnp.sin(ang)))
    ones = jnp.ones(ang.shape[:-1] + (B_NOPE,), F32)
    pad = jnp.zeros(ang.shape[:-1] + (LANES - B_QK,), F32)
    return (jnp.concatenate([ones, cos, cos, pad], axis=-1),
            jnp.concatenate([jnp.zeros_like(ones), sin, sin, pad], axis=-1))


def _alibi_slopes(n):
    return 2.0 ** (-8.0 * jnp.arange(1, n + 1, dtype=F32) / n)


def kernel(x, positions, norm_mix_g, w_in, diff_qk_g, diff_lambda, diff_out_g, mla_cq_g, w_uq, mla_ckv_g, w_ukv,
           mla_qk_g, swa_qk_g, swa_sink, w_branch_a, w_branch_b, w_branch_c, w_o, norm_ffn_g, w_router,
           w_exp_gate, w_exp_up, w_exp_down):
    b, s, d = x.shape
    depth = w_in.shape[0]
    cap = max(1, EC_CAPACITY * s // N_EXPERTS)
    tm_proj = min(512, s)
    tq = min(256, s)
    tq_a = min(1024, s)
    tm_merge = min(512, s)
    tt = min(256, s)
    sc_workers = SC_CORES * SC_SUBCORES
    groups = 2 if (b % 2 == 0 and (b // 2) * N_EXPERTS % sc_workers == 0 and sc_workers % (b // 2) == 0) else 1

    cos_t, sin_t = _rope_slot_tables(positions)
    pos_f = positions.astype(F32)
    pos_c, pos_r = pos_f[:, :, None], pos_f.reshape(b, s // tq_a, 1, tq_a)
    monotone = jnp.all(positions[:, 1:] >= positions[:, :-1])
    lane_bcast = lambda v: jnp.broadcast_to(v[..., None], v.shape + (LANES,)).astype(F32)
    slopes_a = lane_bcast(_alibi_slopes(A_HEADS)[:, None] * LOG2E)
    slopes_c = lane_bcast(_alibi_slopes(C_HEADS).reshape(C_KV_HEADS, C_REP))

    for l in range(depth):
        p = _layer_params(l, w_in, norm_mix_g, diff_qk_g, mla_cq_g, w_uq, mla_ckv_g, w_ukv, mla_qk_g, swa_qk_g,
                          w_branch_a, w_branch_b, w_branch_c, w_o, norm_ffn_g, w_router)
        qa, ka, va, qb, kb, vb, qc, kc, vc = _proj_call(x, cos_t, sin_t, p, tm_proj)
        lam_init = 0.8 - 0.6 * math.exp(-0.3 * l)
        attn_a = functools.partial(_attn_a_call, qa, ka, va, pos_c, pos_r, slopes_a, diff_lambda[l].astype(F32),
                                   diff_out_g[l].reshape(1, -1).astype(F32), lam_init, tq_a)
        oa = lax.cond(monotone, functools.partial(attn_a, True), functools.partial(attn_a, False))
        ob = _attn_b_call(qb, kb, vb, min(512, s))
        oc = _attn_c_call(qc, kc, vc, slopes_c, lane_bcast(swa_sink[l].reshape(C_KV_HEADS, C_REP)))
        x_in, x = x, None
        nb = b // groups
        for grp in range(groups):
            b0 = grp * nb
            xn, h2, aff_t = _merge_call(x_in, oa, ob, oc, p, tm_merge, b0, nb)
            pos, dest, start = _topk_call(aff_t, cap)
            starts = jnp.concatenate([start[:, 0, ::tt], jnp.full((nb, 1), N_EXPERTS * cap, jnp.int32)], axis=1)
            pos2 = pos.reshape(nb * N_EXPERTS, s)
            rows, asel = _sc_gather_call(h2.reshape(nb * s, d // 2), pos2, aff_t.reshape(nb * N_EXPERTS, s), cap)
            y = _expert_call(rows.reshape(nb, N_EXPERTS, cap, d // 2), asel.reshape(nb, N_EXPERTS, cap, SC_LANES),
                             w_exp_gate, w_exp_up, w_exp_down, l)
            y_rows, y_tok = _sc_regroup_call(y.reshape(nb * N_EXPERTS * cap, d // 2),
                                             dest.reshape(nb * N_EXPERTS, s), pos2, cap)
            x = _combine_call(xn, starts.reshape(-1), y_tok.reshape(nb, N_EXPERTS * cap),
                              y_rows.reshape(nb, N_EXPERTS * cap, d // 2), tt, min(256, N_EXPERTS * cap), b, b0, x)
    return x
```

```python
import functools
import math

import numpy as np
import jax
import jax.numpy as jnp
from jax import lax
from jax.experimental import pallas as pl
from jax.experimental.pallas import tpu as pltpu
from jax.experimental.pallas import tpu_sc as plsc

F32 = jnp.float32
BF16 = jnp.bfloat16

D_MODEL = 1024
EPS = 1e-6
A_HEADS = 4
A_DIM = 64
B_HEADS = 8
B_NOPE = 64
B_ROPE = 32
B_VDIM = 64
B_QK = B_NOPE + B_ROPE
B_Q_LORA = 384
B_KV_LORA = 256
ROPE_THETA = 10000.0
C_HEADS = 8
C_KV_HEADS = 2
C_REP = C_HEADS // C_KV_HEADS
C_DIM = 64
WINDOW = 128
N_EXPERTS = 16
EC_CAPACITY = 2
LOG2E = math.log2(math.e)
SC_CORES, SC_SUBCORES, SC_LANES = 2, 16, 16
SC_GATHER_ROWS = 128
SEG_TILE = 256
LANES = 128

_QA = 0
_KA = 512
_VA = 1024
_CQ = 1536
_CKV = 1920
_QC = 2176
_KC = 2688
_VC = 2944
_KR = 3200
_KRR = 3328
_PROJ_COLS = 3456

VMEM_LIMIT = 56 * 1024 * 1024

_NT = (((1,), (1,)), ((), ()))


def _cparams(sem):
    return pltpu.CompilerParams(dimension_semantics=sem, vmem_limit_bytes=VMEM_LIMIT)


def _dot(a, b):
    return jnp.dot(a, b, preferred_element_type=F32)


def _seg_sum(x2, e):
    hi = x2.astype(BF16)
    lo = (x2 - hi.astype(F32)).astype(BF16)
    e2 = jnp.concatenate([e, e], axis=0)
    slabs = [_dot(jnp.concatenate([hi[:, c:c + SEG_TILE], lo[:, c:c + SEG_TILE]], axis=1), e2)
             for c in range(0, x2.shape[1], SEG_TILE)]
    return slabs[0] if len(slabs) == 1 else jnp.concatenate(slabs, axis=1)


def _pack_rows(x):
    half = x.shape[1] // 2
    xb = x.astype(BF16).astype(F32)
    lo = pltpu.bitcast(xb[:, :half], jnp.uint32) >> 16
    hi = pltpu.bitcast(xb[:, half:], jnp.uint32) & jnp.uint32(0xFFFF0000)
    return pltpu.bitcast(lo | hi, F32)


def _unpack_rows(words):
    bits = pltpu.bitcast(words, jnp.uint32)
    return jnp.concatenate([pltpu.bitcast(bits << 16, F32), pltpu.bitcast(bits & jnp.uint32(0xFFFF0000), F32)],
                           axis=1).astype(BF16)


def _sigmoid(x):
    return 0.5 * jnp.tanh(0.5 * x) + 0.5


def _row_rms(x, g):
    return x * lax.rsqrt(jnp.mean(x * x, axis=-1, keepdims=True) + EPS) * g


def _proj_kernel(x_ref, gmix_ref, w_ref, wuq_ref, wukv_ref, e64_ref, e128_ref,
                 gqa_ref, gka_ref, gcq_ref, gckv_ref, gqb_ref, gkb_ref, gqc_ref, gkc_ref,
                 cos_ref, sin_ref,
                 qa_o, ka_o, va_o, qb_o, kb_o, vb_o, qc_o, kc_o, vc_o):
    hb = _row_rms(x_ref[0], gmix_ref[...]).astype(BF16)

    projected = _dot(hb, w_ref[...])

    def proj(a, n):
        return projected[:, a:a + n]

    e64 = e64_ref[...]
    e128 = e128_ref[...]

    def seg_norm(v, e, width, g):
        return v * lax.rsqrt(_seg_sum(v * v, e) * (1.0 / width) + EPS) * g

    def store_slots(o_ref, v, n):
        for j in range(n):
            o_ref[0, j] = v[:, LANES * j:LANES * (j + 1)].astype(o_ref.dtype)

    ones_slot = jnp.ones((hb.shape[0], LANES), F32)

    def store_value_slots(o_ref, v, n):
        for j in range(n):
            o_ref[0, j] = jnp.concatenate([v[:, LANES * j:LANES * (j + 1)], ones_slot], axis=1).astype(o_ref.dtype)

    store_slots(qa_o, seg_norm(proj(_QA, 512), e64, A_DIM, gqa_ref[...]), A_HEADS)
    store_slots(ka_o, seg_norm(proj(_KA, 512), e64, A_DIM, gka_ref[...]), A_HEADS)
    store_value_slots(va_o, proj(_VA, 512), A_HEADS)

    cos_t = cos_ref[0]
    sin_t = sin_ref[0]
    cos8 = jnp.concatenate([cos_t] * B_HEADS, axis=1)
    sin8 = jnp.concatenate([sin_t] * B_HEADS, axis=1)
    cq = _row_rms(proj(_CQ, B_Q_LORA), gcq_ref[...]).astype(BF16)
    q2 = _dot(cq, wuq_ref[...])
    qb = q2[:, :1024] * cos8 + q2[:, 1024:] * sin8
    store_slots(qb_o, seg_norm(qb, e128, B_QK, gqb_ref[...]), B_HEADS)
    ckv = _row_rms(proj(_CKV, B_KV_LORA), gckv_ref[...]).astype(BF16)
    kv = _dot(ckv, wukv_ref[...])
    kr = proj(_KR, LANES) * cos_t + proj(_KRR, LANES) * sin_t
    kb = kv[:, :1024] + jnp.concatenate([kr] * B_HEADS, axis=1)
    store_slots(kb_o, seg_norm(kb, e128, B_QK, gkb_ref[...]), B_HEADS)
    store_value_slots(vb_o, kv[:, 1024:], B_HEADS // 2)

    store_slots(qc_o, seg_norm(proj(_QC, 512), e64, C_DIM, gqc_ref[...]), C_HEADS // 2)
    store_slots(kc_o, seg_norm(proj(_KC, 256), e64, C_DIM, gkc_ref[...]), C_KV_HEADS)
    store_value_slots(vc_o, proj(_VC, 256), C_KV_HEADS)


def _proj_call(x, cos_t, sin_t, p, tm):
    b, s, d = x.shape
    full = lambda a: pl.BlockSpec(a.shape, lambda i, j: (0,) * a.ndim, pipeline_mode=pl.Buffered(1))
    slot = lambda nw: pl.BlockSpec((1, nw[0], tm, nw[1]), lambda i, j: (i, 0, j, 0))
    tok = lambda w: pl.BlockSpec((1, tm, w), lambda i, j: (i, j, 0))
    consts = [p["gmix"], p["w_in"], p["w_uq"], p["w_ukv"], p["e64"], p["e128"],
              p["gqa"], p["gka"], p["gcq"], p["gckv"], p["gqb"], p["gkb"], p["gqc"], p["gkc"]]
    slots = [(A_HEADS, LANES), (A_HEADS, LANES), (A_HEADS, 2 * LANES), (B_HEADS, LANES), (B_HEADS, LANES),
             (B_HEADS // 2, 2 * LANES), (C_HEADS // 2, LANES), (C_KV_HEADS, LANES), (C_KV_HEADS, 2 * LANES)]
    return pl.pallas_call(
        _proj_kernel,
        grid=(b, s // tm),
        in_specs=[tok(d)] + [full(a) for a in consts] + [tok(LANES), tok(LANES)],
        out_specs=[slot(nw) for nw in slots],
        out_shape=[jax.ShapeDtypeStruct((b, nw[0], s, nw[1]), BF16) for nw in slots],
        compiler_params=_cparams(("parallel", "parallel")),
        name="proj",
    )(x, *consts, cos_t, sin_t)


def _attn_a_kernel(q_ref, k_ref, v_ref, pc_ref, pr_ref, slope_ref, lam_ref, og_ref, o_ref, s_scr, *, lam_init, mono):
    t = pl.program_id(2)
    q = q_ref[0, 0]
    tq = q.shape[0]
    ck = tq
    nck = k_ref.shape[2] // ck
    lane = lax.broadcasted_iota(jnp.int32, q.shape, 1)
    zero = jnp.zeros_like(q)
    qm = [jnp.where(lane < A_DIM, q, zero), jnp.where(lane >= A_DIM, q, zero)]
    slope = slope_ref[0][:, :1]
    a = slope * pc_ref[0]
    a_lanes = jnp.broadcast_to(a, (tq, LANES))

    def lane_fold_max(x):
        out = x[:, :LANES]
        for j in range(1, ck // LANES):
            out = jnp.maximum(out, x[:, j * LANES:(j + 1) * LANES])
        return out

    mx = [jnp.full((tq, LANES), -jnp.inf, F32)] * 2
    for d in range(nck):
        if mono:
            c = t if d == 0 else lax.rem(t + d, nck)
            k_c = k_ref[0, 0, pl.ds(pl.multiple_of(c * ck, ck), ck), :]
        else:
            c = d
            k_c = k_ref[0, 0, d * ck:(d + 1) * ck, :]
        b = slope * pr_ref[0, c]
        if mono and d > 0:
            sign = jnp.where(t + d < nck, -1.0, 1.0)
            row_part, col_part = sign * b, sign * a_lanes
            for m in range(2):
                sc = lax.dot_general(qm[m], k_c, _NT, preferred_element_type=F32) + row_part
                s_scr[m, c] = sc
                mx[m] = jnp.maximum(mx[m], lane_fold_max(sc) - col_part)
        else:
            bias = jnp.abs(a - b)
            for m in range(2):
                sc = lax.dot_general(qm[m], k_c, _NT, preferred_element_type=F32) - bias
                s_scr[m, c] = sc
                mx[m] = jnp.maximum(mx[m], lane_fold_max(sc))
    v = v_ref[0, 0]
    acc = []
    for m in range(2):
        row_max = jnp.broadcast_to(jnp.max(mx[m], axis=-1, keepdims=True), (tq, LANES))
        es = []
        for c in range(nck):
            if mono:
                sign = jnp.where(c < t, 1.0, jnp.where(c > t, -1.0, 0.0))
                stab = row_max + sign * a_lanes
            else:
                stab = row_max
            es.append(jnp.exp2(s_scr[m, c] - jnp.concatenate([stab] * (ck // LANES), axis=1)).astype(BF16))
        e = es[0] if nck == 1 else jnp.concatenate(es, axis=1)
        acc.append(_dot(e, v))
    lp = lam_ref[...]
    lam = (jnp.exp(jnp.sum(lp[0:1] * lp[1:2], axis=-1, keepdims=True))
           - jnp.exp(jnp.sum(lp[2:3] * lp[3:4], axis=-1, keepdims=True)) + lam_init)
    o = (acc[0][:, :LANES] * (1.0 / acc[0][:, LANES:])
         - acc[1][:, :LANES] * (lam / acc[1][:, LANES:]))
    o_ref[0] = (_row_rms(o, og_ref[...]) * (1.0 - lam_init)).astype(o_ref.dtype)


def _attn_a_call(qa, ka, va, pos_c, pos_r, slopes, lam_p, out_g, lam_init, tq, mono):
    b, h, s, _ = qa.shape
    kv_spec = lambda w: pl.BlockSpec((1, 1, s, w), lambda i, j, t: (i, j, 0, 0))
    return pl.pallas_call(
        functools.partial(_attn_a_kernel, lam_init=lam_init, mono=mono),
        grid=(b, h, s // tq),
        scratch_shapes=[pltpu.VMEM((2, s // tq, tq, tq), F32)],
        in_specs=[pl.BlockSpec((1, 1, tq, LANES), lambda i, j, t: (i, j, t, 0)), kv_spec(LANES), kv_spec(2 * LANES),
                  pl.BlockSpec((1, tq, 1), lambda i, j, t: (i, t, 0)),
                  pl.BlockSpec((1, s // tq, 1, tq), lambda i, j, t: (i, 0, 0, 0)),
                  pl.BlockSpec((1, 1, LANES), lambda i, j, t: (j, 0, 0)),
                  pl.BlockSpec(lam_p.shape, lambda i, j, t: (0, 0)),
                  pl.BlockSpec(out_g.shape, lambda i, j, t: (0, 0))],
        out_specs=pl.BlockSpec((1, tq, LANES), lambda i, j, t: (i, t, j)),
        out_shape=jax.ShapeDtypeStruct((b, s, h * LANES), BF16),
        compiler_params=_cparams(("parallel", "parallel", "arbitrary")),
        name="attn_a",
    )(qa, ka, va, pos_c, pos_r, slopes, lam_p, out_g)


def _attn_b_kernel(q_ref, k_ref, v_ref, o_ref):
    tq = q_ref.shape[2]
    es = []
    for j in range(2):
        s = lax.dot_general(q_ref[0, j], k_ref[0, j], _NT, preferred_element_type=F32)
        es.append(jnp.exp2(s - jnp.max(s, axis=-1, keepdims=True)).astype(BF16))
    acc = _dot(jnp.concatenate(es, axis=0), v_ref[0, 0])
    lane = lax.broadcasted_iota(jnp.int32, (tq, LANES), 1)
    o_ref[0] = jnp.where(lane < B_VDIM, acc[:tq, :LANES] * (1.0 / acc[:tq, LANES:]),
                         acc[tq:, :LANES] * (1.0 / acc[tq:, LANES:])).astype(o_ref.dtype)


def _attn_b_call(qb, kb, vb, tq):
    b, h, s, _ = qb.shape
    return pl.pallas_call(
        _attn_b_kernel,
        grid=(b, h // 2, s // tq),
        in_specs=[pl.BlockSpec((1, 2, tq, LANES), lambda i, j, t: (i, j, t, 0)),
                  pl.BlockSpec((1, 2, s, LANES), lambda i, j, t: (i, j, 0, 0)),
                  pl.BlockSpec((1, 1, s, 2 * LANES), lambda i, j, t: (i, j, 0, 0))],
        out_specs=pl.BlockSpec((1, tq, LANES), lambda i, j, t: (i, t, j)),
        out_shape=jax.ShapeDtypeStruct((b, s, (h // 2) * LANES), BF16),
        compiler_params=_cparams(("parallel", "parallel", "arbitrary")),
        name="attn_b",
    )(qb, kb, vb)


def _attn_c_kernel(q_ref, kp_ref, ko_ref, kn_ref, vp_ref, vo_ref, vn_ref, slope_ref, sink_ref, o_ref, *, seq):
    w = WINDOW
    nsub = ko_ref.shape[2] // w
    n0 = pl.program_id(2) * nsub
    kcat = jnp.concatenate([kp_ref[0, 0], ko_ref[0, 0], kn_ref[0, 0]], axis=0)
    vcat = jnp.concatenate([vp_ref[0, 0], vo_ref[0, 0], vn_ref[0, 0]], axis=0)
    lane = lax.broadcasted_iota(jnp.int32, (w, LANES), 1)
    r_idx = lax.broadcasted_iota(jnp.int32, (w, 3 * w), 0)
    c_idx = lax.broadcasted_iota(jnp.int32, (w, 3 * w), 1)
    arel = jnp.abs(c_idx - w - r_idx)
    dist = arel.astype(F32)
    slopes = slope_ref[0]
    sinks = sink_ref[0]
    bias4 = jnp.concatenate([jnp.where(arel <= w, -slopes[r:r + 1, :1] * dist, -1e30) for r in range(C_REP)], axis=0)
    sink4 = jnp.concatenate([jnp.broadcast_to(sinks[r:r + 1, :], (w, LANES)) for r in range(C_REP)], axis=0)
    c_row = lax.broadcasted_iota(jnp.int32, (1, 3 * w), 1)
    scs = []
    for i in range(nsub):
        parts = []
        for p in range(2):
            q = q_ref[0, p, i * w:(i + 1) * w, :]
            zero = jnp.zeros_like(q)
            parts += [jnp.where(lane < C_DIM, q, zero), jnp.where(lane >= C_DIM, q, zero)]
        qz = jnp.concatenate(parts, axis=0)
        s = lax.dot_general(qz, kcat[i * w:(i + 3) * w], _NT, preferred_element_type=F32)
        kidx = (n0 + i - 1) * w + c_row
        edge = jnp.where(kidx >= 0, jnp.where(kidx < seq, 0.0, -1e30), -1e30)
        scs.append(s + bias4 + edge)
    sc = jnp.concatenate(scs, axis=0)
    sk = jnp.concatenate([sink4] * nsub, axis=0)
    m = jnp.maximum(jnp.broadcast_to(jnp.max(sc, axis=-1, keepdims=True), sk.shape), sk)
    e = jnp.exp(sc - jnp.concatenate([m] * 3, axis=1)).astype(BF16)
    tail = jnp.exp(sk - m)
    for i in range(nsub):
        rows = slice(i * 4 * w, (i + 1) * 4 * w)
        acc = _dot(e[rows], vcat[i * w:(i + 3) * w])
        o = acc[:, :LANES] * (1.0 / (acc[:, LANES:] + tail[rows]))
        pair0 = jnp.where(lane < C_DIM, o[0:w], o[w:2 * w])
        pair1 = jnp.where(lane < C_DIM, o[2 * w:3 * w], o[3 * w:4 * w])
        o_ref[0, i * w:(i + 1) * w, :] = jnp.concatenate([pair0, pair1], axis=1).astype(o_ref.dtype)


def _attn_c_call(qc, kc, vc, slopes, sinks):
    b, _, s, _ = qc.shape
    nb = s // WINDOW
    nsub = min(32, nb)
    tq = nsub * WINDOW
    prev = lambda wd: pl.BlockSpec((1, 1, WINDOW, wd), lambda i, g, n: (i, g, jnp.maximum(n * nsub - 1, 0), 0))
    own = lambda wd: pl.BlockSpec((1, 1, tq, wd), lambda i, g, n: (i, g, n, 0))
    nxt = lambda wd: pl.BlockSpec((1, 1, WINDOW, wd), lambda i, g, n: (i, g, jnp.minimum((n + 1) * nsub, nb - 1), 0))
    kw, vw = kc.shape[-1], vc.shape[-1]
    per_group = pl.BlockSpec((1, C_REP, LANES), lambda i, g, n: (g, 0, 0))
    return pl.pallas_call(
        functools.partial(_attn_c_kernel, seq=s),
        grid=(b, C_KV_HEADS, s // tq),
        in_specs=[pl.BlockSpec((1, 2, tq, LANES), lambda i, g, n: (i, g, n, 0)),
                  prev(kw), own(kw), nxt(kw), prev(vw), own(vw), nxt(vw), per_group, per_group],
        out_specs=pl.BlockSpec((1, tq, 2 * LANES), lambda i, g, n: (i, n, g)),
        out_shape=jax.ShapeDtypeStruct((b, s, C_HEADS * C_DIM), BF16),
        compiler_params=_cparams(("parallel", "parallel", "arbitrary")),
        name="attn_c",
    )(qc, kc, kc, kc, vc, vc, vc, slopes, sinks)


def _merge_kernel(x_ref, gmix_ref, wg_ref, oa_ref, ob_ref, oc_ref, wa_ref, wb_ref, wc_ref, wo_ref,
                  gffn_ref, wr_ref, xn_o, h2_o, aff_o):
    d = D_MODEL
    x = x_ref[0]
    hb = _row_rms(x, gmix_ref[...]).astype(BF16)
    g = _sigmoid(_dot(hb, wg_ref[...]))
    merged = (g[:, :d] * _dot(oa_ref[0], wa_ref[...]) + g[:, d:2 * d] * _dot(ob_ref[0], wb_ref[...])
              + g[:, 2 * d:] * _dot(oc_ref[0], wc_ref[...]))
    xn = x + _dot(merged.astype(BF16), wo_ref[...])
    xn_o[0] = xn
    h2 = _row_rms(xn, gffn_ref[...])
    h2_o[0] = _pack_rows(h2)
    logits = lax.dot_general(wr_ref[...], h2, _NT, preferred_element_type=F32,
                             precision=lax.Precision.HIGHEST)
    ex = jnp.exp(logits - jnp.max(logits, axis=0, keepdims=True))
    aff_o[0] = ex / jnp.sum(ex, axis=0, keepdims=True)


def _merge_call(x, oa, ob, oc, p, tm, batch0, nb):
    _, s, d = x.shape
    full = lambda a: pl.BlockSpec(a.shape, lambda i, j: (0,) * a.ndim, pipeline_mode=pl.Buffered(1))
    tok_in = lambda w: pl.BlockSpec((1, tm, w), lambda i, j: (i + batch0, j, 0))
    tok = lambda w: pl.BlockSpec((1, tm, w), lambda i, j: (i, j, 0))
    return pl.pallas_call(
        _merge_kernel,
        grid=(nb, s // tm),
        in_specs=[tok_in(d), full(p["gmix"]), full(p["w_gate"]), tok_in(512), tok_in(512), tok_in(512),
                  full(p["w_a"]), full(p["w_b"]), full(p["w_c"]), full(p["w_o"]), full(p["gffn"]), full(p["w_rt"])],
        out_specs=[tok(d), tok(d // 2), pl.BlockSpec((1, N_EXPERTS, tm), lambda i, j: (i, 0, j))],
        out_shape=[jax.ShapeDtypeStruct((nb, s, d), F32), jax.ShapeDtypeStruct((nb, s, d // 2), F32),
                   jax.ShapeDtypeStruct((nb, N_EXPERTS, s), F32)],
        compiler_params=_cparams(("parallel", "parallel")),
        name="merge",
    )(x, p["gmix"], p["w_gate"], oa, ob, oc, p["w_a"], p["w_b"], p["w_c"], p["w_o"], p["gffn"], p["w_rt"])


def _cumsum_lanes(mask01, chunk):
    rows, s = mask01.shape
    tri = jnp.where(lax.broadcasted_iota(jnp.int32, (chunk, chunk), 0)
                    <= lax.broadcasted_iota(jnp.int32, (chunk, chunk), 1), 1.0, 0.0).astype(BF16)
    carry = jnp.zeros((rows, 1), F32)
    outs = []
    for c in range(s // chunk):
        cs = _dot(mask01[:, c * chunk:(c + 1) * chunk], tri) + carry
        outs.append(cs)
        carry = cs[:, chunk - 1:chunk]
    return jnp.concatenate(outs, axis=1)


def _topk_kernel(aff_ref, pos_o, dest_o, start_o, *, cap, chunk):
    bits = pltpu.bitcast(aff_ref[0], jnp.int32)
    rows = bits.shape[0]
    capf = float(cap)

    def count(mask):
        return jnp.sum(jnp.where(mask, 1.0, 0.0), axis=1, keepdims=True)

    def body(_, c):
        lo, hi = c
        mid = lo + ((hi - lo + 1) >> 1)
        ok = count(bits >= mid) >= capf
        return jnp.where(ok, mid, lo), jnp.where(ok, hi, mid - 1)

    lo0 = jnp.zeros((rows, 1), jnp.int32)
    hi0 = jnp.full((rows, 1), 0x7F800000, jnp.int32)
    thr, _ = lax.fori_loop(0, 31, body, (lo0, hi0))
    gt = bits > thr
    eq = bits == thr
    need = capf - count(gt)
    eq_rank = _cumsum_lanes(jnp.where(eq, 1.0, 0.0).astype(BF16), chunk)
    sel = jnp.where(gt, 1.0, jnp.where(eq, jnp.where(eq_rank <= need, 1.0, 0.0), 0.0))
    sel_b = sel.astype(BF16)
    slot = _cumsum_lanes(sel_b, chunk) - 1.0
    pos_o[0] = jnp.where(sel > 0.0, slot, -1.0).astype(jnp.int32)
    per_token = jnp.broadcast_to(jnp.sum(sel, axis=0, keepdims=True), (8, sel.shape[1]))
    before_token = (_cumsum_lanes(per_token.astype(BF16), chunk) - per_token)[0:1]
    lower = jnp.where(lax.broadcasted_iota(jnp.int32, (rows, rows), 1)
                      < lax.broadcasted_iota(jnp.int32, (rows, rows), 0), 1.0, 0.0).astype(BF16)
    before_expert = _dot(lower, sel_b)
    dest_o[0] = jnp.where(sel > 0.0, before_token + before_expert, -1.0).astype(jnp.int32)
    start_o[0] = before_token.astype(jnp.int32)


def _topk_call(aff_t, cap):
    b, e, s = aff_t.shape
    spec = pl.BlockSpec((1, e, s), lambda i: (i, 0, 0))
    return pl.pallas_call(
        functools.partial(_topk_kernel, cap=cap, chunk=min(512, s)),
        grid=(b,),
        in_specs=[spec],
        out_specs=[spec, spec, pl.BlockSpec((1, 1, s), lambda i: (i, 0, 0))],
        out_shape=[jax.ShapeDtypeStruct((b, e, s), jnp.int32), jax.ShapeDtypeStruct((b, e, s), jnp.int32),
                   jax.ShapeDtypeStruct((b, 1, s), jnp.int32)],
        compiler_params=_cparams(("parallel",)),
        name="topk",
    )(aff_t)


def _sc_gather_call(table, pos2, aff2, cap):
    npairs, s = pos2.shape
    width = table.shape[1]
    workers = SC_CORES * SC_SUBCORES
    per_worker = npairs // workers
    nchunk = cap // SC_GATHER_ROWS
    assert npairs % workers == 0 and cap % SC_GATHER_ROWS == 0 and s % SC_LANES == 0
    mesh = plsc.VectorSubcoreMesh(core_axis_name="c", subcore_axis_name="s")

    @functools.partial(
        pl.kernel, mesh=mesh,
        out_type=[jax.ShapeDtypeStruct((npairs * cap, width), F32),
                  jax.ShapeDtypeStruct((npairs * cap * SC_LANES,), F32)],
        scratch_types=[pltpu.VMEM((s,), jnp.int32), pltpu.VMEM((s,), F32),
                       pltpu.VMEM((nchunk, SC_GATHER_ROWS), jnp.int32), pltpu.VMEM((cap,), F32),
                       pltpu.VMEM((cap * SC_LANES,), F32), pltpu.VMEM((SC_GATHER_ROWS, width), F32),
                       pltpu.SemaphoreType.DMA],
        compiler_params=pltpu.CompilerParams(needs_layout_passes=False),
        name="sc_gather",
    )
    def gather(table_hbm, pos_hbm, aff_hbm, rows_hbm, asel_hbm, pos_v, aff_v, idx_v, aslot_v, asplat_v, rows_v, sem):
        wid = lax.axis_index("s") * SC_CORES + lax.axis_index("c")
        lanes = lax.iota(jnp.int32, SC_LANES)

        @pl.loop(0, per_worker)
        def _(j):
            pair = wid * per_worker + j
            row0 = (pair // N_EXPERTS) * s
            pltpu.sync_copy(pos_hbm.at[pair], pos_v)
            pltpu.sync_copy(aff_hbm.at[pair], aff_v)

            @pl.loop(0, s // SC_LANES)
            def _(i):
                sl = pl.ds(i * SC_LANES, SC_LANES)
                slot = pos_v[sl]
                chosen = slot >= 0
                token_row = lanes + (i * SC_LANES + row0)
                plsc.store_scatter(idx_v, [slot >> (SC_GATHER_ROWS.bit_length() - 1), slot & (SC_GATHER_ROWS - 1)], token_row, mask=chosen)
                plsc.store_scatter(aslot_v, [slot], aff_v[sl], mask=chosen)

            @pl.loop(0, cap)
            def _(r):
                asplat_v[pl.ds(r * SC_LANES, SC_LANES)] = plsc.load_gather(
                    aslot_v, [jnp.full((SC_LANES,), r, jnp.int32)])

            pltpu.sync_copy(asplat_v, asel_hbm.at[pl.ds(pair * (cap * SC_LANES), cap * SC_LANES)])
            for c in range(nchunk):
                pltpu.async_copy(table_hbm.at[idx_v.at[c]], rows_v, sem).wait()
                pltpu.sync_copy(rows_v, rows_hbm.at[pl.ds(pair * cap + c * SC_GATHER_ROWS, SC_GATHER_ROWS)])

    return gather(table, pos2, aff2)


def _expert_kernel(xin_ref, asel_ref, wg_ref, wu_ref, wd_ref, y_o, wg_s, wu_s, wd_s):
    @pl.when(pl.program_id(1) == 0)
    def _():
        wg_s[...] = wg_ref[0, 0].astype(BF16)
        wu_s[...] = wu_ref[0, 0].astype(BF16)
        wd_s[...] = wd_ref[0, 0].astype(BF16)

    xin = _unpack_rows(xin_ref[0, 0])
    g = _dot(xin, wg_s[...])
    u = _dot(xin, wu_s[...])
    hid = (g * _sigmoid(g) * u).astype(BF16)
    y_o[0, 0] = _pack_rows(_dot(hid, wd_s[...]) * asel_ref[0, 0][:, :1])


def _expert_call(xin, asel, wg, wu, wd, layer):
    b, e, cap, _ = xin.shape
    d, f = wg.shape[-2:]
    tokens = lambda w: pl.BlockSpec((1, 1, cap, w), lambda j, i: (i, j, 0, 0))
    weight = lambda r, c: pl.BlockSpec((1, 1, r, c), lambda j, i: (layer, j, 0, 0))
    return pl.pallas_call(
        _expert_kernel,
        grid=(e, b),
        in_specs=[tokens(xin.shape[-1]), tokens(asel.shape[-1]), weight(d, f), weight(d, f), weight(f, d)],
        out_specs=tokens(d // 2),
        out_shape=jax.ShapeDtypeStruct((b, e, cap, d // 2), F32),
        scratch_shapes=[pltpu.VMEM((d, f), BF16), pltpu.VMEM((d, f), BF16), pltpu.VMEM((f, d), BF16)],
        compiler_params=_cparams(("parallel", "arbitrary")),
        name="expert",
    )(xin, asel, wg, wu, wd)


def _sc_regroup_call(y_rows, dest2, pos2, cap):
    npairs, s = dest2.shape
    width = y_rows.shape[1]
    workers = SC_CORES * SC_SUBCORES
    batches = npairs // N_EXPERTS
    per_batch = N_EXPERTS * cap
    split = workers // batches
    span = per_batch // split
    nchunk = span // SC_GATHER_ROWS
    assert workers % batches == 0 and per_batch % split == 0 and span % SC_GATHER_ROWS == 0 and s % SC_LANES == 0
    mesh = plsc.VectorSubcoreMesh(core_axis_name="c", subcore_axis_name="s")

    @functools.partial(
        pl.kernel, mesh=mesh,
        out_type=[jax.ShapeDtypeStruct((npairs * cap, width), F32),
                  jax.ShapeDtypeStruct((npairs * cap,), jnp.int32)],
        scratch_types=[pltpu.VMEM((s,), jnp.int32), pltpu.VMEM((s,), jnp.int32),
                       pltpu.VMEM((nchunk, SC_GATHER_ROWS), jnp.int32), pltpu.VMEM((span,), jnp.int32),
                       pltpu.VMEM((SC_GATHER_ROWS, width), F32), pltpu.SemaphoreType.DMA],
        compiler_params=pltpu.CompilerParams(needs_layout_passes=False),
        name="sc_regroup",
    )
    def regroup(y_hbm, dest_hbm, pos_hbm, rows_hbm, tok_hbm, dest_v, pos_v, src_v, tok_v, rows_v, sem):
        wid = lax.axis_index("s") * SC_CORES + lax.axis_index("c")
        batch = wid // split
        first = (wid % split) * span
        lanes = lax.iota(jnp.int32, SC_LANES)

        @pl.loop(0, N_EXPERTS)
        def _(e):
            pair = batch * N_EXPERTS + e
            pltpu.sync_copy(dest_hbm.at[pair], dest_v)
            pltpu.sync_copy(pos_hbm.at[pair], pos_v)

            @pl.loop(0, s // SC_LANES)
            def _(i):
                sl = pl.ds(i * SC_LANES, SC_LANES)
                local = dest_v[sl] - first
                mine = (local >= 0) & (local < span)
                plsc.store_scatter(src_v, [local >> (SC_GATHER_ROWS.bit_length() - 1), local & (SC_GATHER_ROWS - 1)],
                                   pos_v[sl] + pair * cap, mask=mine)
                plsc.store_scatter(tok_v, [local], lanes + i * SC_LANES, mask=mine)

        out0 = batch * per_batch + first
        pltpu.sync_copy(tok_v, tok_hbm.at[pl.ds(out0, span)])
        for c in range(nchunk):
            pltpu.async_copy(y_hbm.at[src_v.at[c]], rows_v, sem).wait()
            pltpu.sync_copy(rows_v, rows_hbm.at[pl.ds(out0 + c * SC_GATHER_ROWS, SC_GATHER_ROWS)])

    return regroup(y_rows, dest2, pos2)


def _combine_kernel(start_ref, x_ref, tok_ref, rows_ref, *rest, ntile, ck):
    o_ref = rest[-1]
    b, j = pl.program_id(0), pl.program_id(1)
    tt = x_ref.shape[1]
    lo = start_ref[b * (ntile + 1) + j]
    hi = start_ref[b * (ntile + 1) + j + 1]
    c_lo = lo // ck
    c_hi = jnp.where(hi > lo, (hi - 1) // ck + 1, c_lo)
    tokens = j * tt + lax.broadcasted_iota(jnp.int32, (tt, ck), 0)
    o_ref[0] = x_ref[0]

    def body(c, carry):
        onehot = jnp.where(tok_ref[0, c] == tokens, 1.0, 0.0).astype(BF16)
        o_ref[0] += _dot(onehot, _unpack_rows(rows_ref[0, c]))
        return carry

    lax.fori_loop(c_lo, c_hi, body, 0)


def _combine_call(xn, starts, tok, rows, tt, ck, batch, batch0, earlier):
    nb, s, d = xn.shape
    nchunk = tok.shape[1] // ck
    ntile = s // tt
    tile = pl.BlockSpec((1, tt, d), lambda i, t, st: (i + batch0, t, 0))
    in_specs = [pl.BlockSpec((1, tt, d), lambda i, t, st: (i, t, 0)),
                pl.BlockSpec((1, nchunk, 1, ck), lambda i, t, st: (i, 0, 0, 0)),
                pl.BlockSpec((1, nchunk, ck, rows.shape[-1]), lambda i, t, st: (i, 0, 0, 0))]
    operands = [starts, xn, tok.reshape(nb, nchunk, 1, ck), rows.reshape(nb, nchunk, ck, rows.shape[-1])]
    aliases = {}
    if earlier is not None:
        in_specs.append(pl.BlockSpec(memory_space=pl.ANY))
        operands.append(earlier)
        aliases = {len(operands) - 1: 0}
    return pl.pallas_call(
        functools.partial(_combine_kernel, ntile=ntile, ck=ck),
        grid_spec=pltpu.PrefetchScalarGridSpec(
            num_scalar_prefetch=1, grid=(nb, ntile), in_specs=in_specs, out_specs=tile),
        out_shape=jax.ShapeDtypeStruct((batch, s, d), F32),
        input_output_aliases=aliases,
        compiler_params=_cparams(("parallel", "arbitrary")),
        name="combine",
    )(*operands)


def _block_diag(n, blk):
    i = np.arange(n)
    return jnp.asarray((i[:, None] // blk) == (i[None, :] // blk), dtype=BF16)


def _head_slots(w, heads, width):
    r = w.shape[0]
    return jnp.pad(w.reshape(r, heads, width), ((0, 0), (0, 0), (0, LANES - width))).reshape(r, heads * LANES)


def _rotate_half_cols(w):
    half = B_ROPE // 2
    return jnp.concatenate([-w[..., half:], w[..., :half]], axis=-1)


def _layer_params(l, w_in, norm_mix_g, diff_qk_g, mla_cq_g, w_uq, mla_ckv_g, w_ukv, mla_qk_g, swa_qk_g,
                  w_branch_a, w_branch_b, w_branch_c, w_o, norm_ffn_g, w_router):
    d = D_MODEL
    wi = w_in[l]
    off = np.cumsum([0, 512, 512, 512, B_Q_LORA, B_KV_LORA, B_ROPE, 512, 128, 128, 3 * d])
    piece = lambda k: wi[:, off[k]:off[k + 1]]
    maps_to_heads = lambda w: w.reshape(d, 2, A_HEADS, A_DIM).transpose(0, 2, 1, 3).reshape(d, 512)
    dup = lambda w: jnp.concatenate([w.reshape(d, C_KV_HEADS, 1, C_DIM)] * 2, axis=2).reshape(d, 256)
    kr = piece(5)
    rope_slot = lambda w: jnp.pad(w, ((0, 0), (B_NOPE, LANES - B_QK)))
    w_proj = jnp.concatenate(
        [maps_to_heads(piece(0)), maps_to_heads(piece(1)), piece(2), piece(3), piece(4), piece(6),
         dup(piece(7)), dup(piece(8)), rope_slot(kr), rope_slot(_rotate_half_cols(kr))], axis=1).astype(BF16)
    assert w_proj.shape == (d, _PROJ_COLS)

    wq = w_uq[l].reshape(B_Q_LORA, B_HEADS, B_QK)
    wq_rot = jnp.concatenate([jnp.zeros_like(wq[..., :B_NOPE]), _rotate_half_cols(wq[..., B_NOPE:])], axis=-1)
    w_uq_x = jnp.concatenate([_head_slots(wq.reshape(B_Q_LORA, -1), B_HEADS, B_QK),
                              _head_slots(wq_rot.reshape(B_Q_LORA, -1), B_HEADS, B_QK)], axis=1).astype(BF16)
    wkv = w_ukv[l].reshape(B_KV_LORA, B_HEADS, B_NOPE + B_VDIM)
    w_ukv_x = jnp.concatenate([_head_slots(wkv[..., :B_NOPE].reshape(B_KV_LORA, -1), B_HEADS, B_NOPE),
                               wkv[..., B_NOPE:].reshape(B_KV_LORA, -1)], axis=1).astype(BF16)

    row = lambda v: v.reshape(1, -1).astype(F32)
    slot_gain = lambda g: jnp.tile(jnp.pad(g, (0, LANES - B_QK)), B_HEADS)
    return {
        "gmix": row(norm_mix_g[l]), "w_in": w_proj, "w_uq": w_uq_x, "w_ukv": w_ukv_x,
        "e64": _block_diag(SEG_TILE, 64), "e128": _block_diag(SEG_TILE, LANES),
        "gqa": row(jnp.tile(diff_qk_g[l, 0], 8) * (A_DIM ** -0.5 * LOG2E)), "gka": row(jnp.tile(diff_qk_g[l, 1], 8)),
        "gcq": row(mla_cq_g[l]), "gckv": row(mla_ckv_g[l]),
        "gqb": row(slot_gain(mla_qk_g[l, 0]) * (B_QK ** -0.5 * LOG2E)), "gkb": row(slot_gain(mla_qk_g[l, 1])),
        "gqc": row(jnp.tile(swa_qk_g[l, 0], 8) * (C_DIM ** -0.5)), "gkc": row(jnp.tile(swa_qk_g[l, 1], 4)),
        "w_gate": piece(9).astype(BF16),
        "w_a": w_branch_a[l].astype(BF16), "w_b": w_branch_b[l].astype(BF16), "w_c": w_branch_c[l].astype(BF16),
        "w_o": w_o[l].astype(BF16), "gffn": row(norm_ffn_g[l]), "w_rt": w_router[l].T.astype(F32),
    }


def _rope_slot_tables(positions):
    inv = 1.0 / (ROPE_THETA ** (jnp.arange(0, B_ROPE, 2, dtype=F32) / B_ROPE))
    ang = positions.astype(F32)[..., None] * inv
    cos, sin = lax.optimization_barrier((jnp.cos(ang), jnp.sin(ang)))
    ones = jnp.ones(ang.shape[:-1] + (B_NOPE,), F32)
    pad = jnp.zeros(ang.shape[:-1] + (LANES - B_QK,), F32)
    return (jnp.concatenate([ones, cos, cos, pad], axis=-1),
            jnp.concatenate([jnp.zeros_like(ones), sin, sin, pad], axis=-1))


def _alibi_slopes(n):
    return 2.0 ** (-8.0 * jnp.arange(1, n + 1, dtype=F32) / n)


def kernel(x, positions, norm_mix_g, w_in, diff_qk_g, diff_lambda, diff_out_g, mla_cq_g, w_uq, mla_ckv_g, w_ukv,
           mla_qk_g, swa_qk_g, swa_sink, w_branch_a, w_branch_b, w_branch_c, w_o, norm_ffn_g, w_router,
           w_exp_gate, w_exp_up, w_exp_down):
    b, s, d = x.shape
    depth = w_in.shape[0]
    cap = max(1, EC_CAPACITY * s // N_EXPERTS)
    tm_proj = min(512, s)
    tq = min(256, s)
    tq_a = min(1024, s)
    tm_merge = min(512, s)
    tt = min(256, s)
    sc_workers = SC_CORES * SC_SUBCORES
    groups = 2 if (b % 2 == 0 and (b // 2) * N_EXPERTS % sc_workers == 0 and sc_workers % (b // 2) == 0) else 1

    cos_t, sin_t = _rope_slot_tables(positions)
    pos_f = positions.astype(F32)
    pos_c, pos_r = pos_f[:, :, None], pos_f.reshape(b, s // tq_a, 1, tq_a)
    monotone = jnp.all(positions[:, 1:] >= positions[:, :-1])
    lane_bcast = lambda v: jnp.broadcast_to(v[..., None], v.shape + (LANES,)).astype(F32)
    slopes_a = lane_bcast(_alibi_slopes(A_HEADS)[:, None] * LOG2E)
    slopes_c = lane_bcast(_alibi_slopes(C_HEADS).reshape(C_KV_HEADS, C_REP))

    for l in range(depth):
        p = _layer_params(l, w_in, norm_mix_g, diff_qk_g, mla_cq_g, w_uq, mla_ckv_g, w_ukv, mla_qk_g, swa_qk_g,
                          w_branch_a, w_branch_b, w_branch_c, w_o, norm_ffn_g, w_router)
        qa, ka, va, qb, kb, vb, qc, kc, vc = _proj_call(x, cos_t, sin_t, p, tm_proj)
        lam_init = 0.8 - 0.6 * math.exp(-0.3 * l)
        attn_a = functools.partial(_attn_a_call, qa, ka, va, pos_c, pos_r, slopes_a, diff_lambda[l].astype(F32),
                                   diff_out_g[l].reshape(1, -1).astype(F32), lam_init, tq_a)
        oa = lax.cond(monotone, functools.partial(attn_a, True), functools.partial(attn_a, False))
        ob = _attn_b_call(qb, kb, vb, min(512, s))
        oc = _attn_c_call(qc, kc, vc, slopes_c, lane_bcast(swa_sink[l].reshape(C_KV_HEADS, C_REP)))
        x_in, x = x, None
        nb = b // groups
        for grp in range(groups):
            b0 = grp * nb
            xn, h2, aff_t = _merge_call(x_in, oa, ob, oc, p, tm_merge, b0, nb)
            pos, dest, start = _topk_call(aff_t, cap)
            starts = jnp.concatenate([start[:, 0, ::tt], jnp.full((nb, 1), N_EXPERTS * cap, jnp.int32)], axis=1)
            pos2 = pos.reshape(nb * N_EXPERTS, s)
            rows, asel = _sc_gather_call(h2.reshape(nb * s, d // 2), pos2, aff_t.reshape(nb * N_EXPERTS, s), cap)
            y = _expert_call(rows.reshape(nb, N_EXPERTS, cap, d // 2), asel.reshape(nb, N_EXPERTS, cap, SC_LANES),
                             w_exp_gate, w_exp_up, w_exp_down, l)
            y_rows, y_tok = _sc_regroup_call(y.reshape(nb * N_EXPERTS * cap, d // 2),
                                             dest.reshape(nb * N_EXPERTS, s), pos2, cap)
            x = _combine_call(xn, starts.reshape(-1), y_tok.reshape(nb, N_EXPERTS * cap),
                              y_rows.reshape(nb, N_EXPERTS * cap, d // 2), tt, min(256, N_EXPERTS * cap), b, b0, x)
    return x
```
